```python
import jax, jax.numpy as jnp
from jax import lax
import numpy as np

D_MODEL = 2048
BATCH = 1
SEQ = 8192
DEPTH = 1
DEC_BATCH = 32
DEC_SEQ = 1
PAST_LEN = 8192
PAGE_SIZE = 128

N_HEADS = 16
N_KV = 4
GROUP = N_HEADS // N_KV
HEAD_DIM = 64
NSA_DIM = N_HEADS * HEAD_DIM
KV_DIM = N_KV * HEAD_DIM
CMP_BLOCK = 64
N_SEL = 16
WINDOW = 512
Q_BLOCK = 128
FORCED_SCORE = 1e3
RW_HEADS = 16
RW_N = 64
RW_DIM = RW_HEADS * RW_N
W_LORA = 96
A_LORA = 96
G_LORA = 256
RW_SIZES = (RW_DIM, W_LORA, RW_DIM, RW_DIM, A_LORA, G_LORA)
RW_COLS = sum(RW_SIZES)
NSA_SIZES = (NSA_DIM, KV_DIM, KV_DIM, KV_DIM, KV_DIM, KV_DIM, KV_DIM, 3 * N_HEADS)
NSA_COLS = sum(NSA_SIZES)
MERGE_COLS = 2 * D_MODEL
IN_COLS = NSA_COLS + RW_COLS + MERGE_COLS
D_FF = 5632
NORM_EPS = 1e-6
GN_EPS = 64e-5

kernel_name = "nsa_rwkv7_macaron_hybrid_step"


def rms_norm(x, g):
    xf = x.astype(jnp.float32)
    y = xf * lax.rsqrt(jnp.mean(xf * xf, axis=-1, keepdims=True) + NORM_EPS)
    return (y * g.astype(jnp.float32)).astype(x.dtype)


def split_last(x, sizes):
    idx = np.cumsum(sizes)[:-1].tolist()
    return jnp.split(x, idx, axis=-1)


def swiglu(h, wg, wu, wd):
    return (jax.nn.silu(h @ wg) * (h @ wu)) @ wd


def masked_softmax(s, mask):
    s = jnp.where(mask, s.astype(jnp.float32), -jnp.inf)
    m = jnp.max(s, axis=-1, keepdims=True)
    m = jnp.where(jnp.isfinite(m), m, 0.0)
    p = jnp.exp(s - m)
    return p / jnp.maximum(jnp.sum(p, axis=-1, keepdims=True), 1e-30)


def alibi_slopes():
    h = jnp.arange(1, N_HEADS + 1, dtype=jnp.float32)
    return jnp.exp2(-8.0 * h / N_HEADS).reshape(N_KV, GROUP)


def gather_pages(pool, table):
    g = pool[table]
    return g.reshape(table.shape[0], -1, N_KV, HEAD_DIM)


def compress(k_raw, w_cmp, pe_cmp):
    B, L = k_raw.shape[:2]
    nb = -(-L // CMP_BLOCK)
    k = jnp.pad(k_raw, ((0, 0), (0, nb * CMP_BLOCK - L), (0, 0), (0, 0)))
    k = k.reshape(B, nb, CMP_BLOCK, N_KV, HEAD_DIM) + pe_cmp[:, None, :]
    return jnp.einsum("bnjkd,jde->bnke", k, w_cmp)


def nsa_attend(q, gates, pos0, kc, vc, ks, vs, kw, vw, win_base, p):
    B, Tq = q.shape[:2]
    L = kc.shape[1]
    nb = -(-L // CMP_BLOCK)
    ck = rms_norm(compress(kc, p["w_cmp_k"], p["pe_cmp_k"]), p["g_kc"])
    cv = compress(vc, p["w_cmp_v"], p["pe_cmp_v"])
    padb = ((0, 0), (0, nb * CMP_BLOCK - L), (0, 0), (0, 0))
    ks_b = jnp.pad(ks, padb).reshape(B, nb, CMP_BLOCK, N_KV, HEAD_DIM).transpose(0, 3, 1, 2, 4)
    vs_b = jnp.pad(vs, padb).reshape(B, nb, CMP_BLOCK, N_KV, HEAD_DIM).transpose(0, 3, 1, 2, 4)
    padw = ((0, 0), (WINDOW, 0), (0, 0), (0, 0))
    kw_p = jnp.pad(kw, padw)
    vw_p = jnp.pad(vw, padw)
    slopes = alibi_slopes()[None, :, :, None, None]
    blk = jnp.arange(nb)
    blk_end = blk * CMP_BLOCK + CMP_BLOCK - 1
    blk_mid = blk * CMP_BLOCK + (CMP_BLOCK - 1) / 2
    n_sel = min(N_SEL, nb)
    qblk = Q_BLOCK if Tq % Q_BLOCK == 0 else Tq
    nqb = Tq // qblk
    scale = HEAD_DIM ** -0.5
    bi = jnp.arange(B)[:, None, None, None]
    ki = jnp.arange(N_KV)[None, :, None, None]
    q_blocks = q.reshape(B, nqb, qblk, N_KV, GROUP, HEAD_DIM).swapaxes(0, 1)
    g_blocks = gates.reshape(B, nqb, qblk, N_KV, GROUP, 3).swapaxes(0, 1)

    def one_block(args):
        i, qb, gb = args
        p0 = pos0 + i * qblk
        t = p0 + jnp.arange(qblk)
        tf = t.astype(jnp.float32)
        s = jnp.einsum("bqkgd,bnkd->bkgqn", qb, ck).astype(jnp.float32) * scale
        s = s - slopes * (tf[:, None] - blk_mid[None, :])
        p_c = masked_softmax(s, blk_end[None, :] <= t[:, None])
        o_c = jnp.einsum("bkgqn,bnkd->bqkgd", p_c.astype(cv.dtype), cv)
        imp = jnp.sum(p_c, axis=2)
        cur = t // CMP_BLOCK
        forced = (blk[None, :] == 0) | (blk[None, :] == cur[:, None]) | (blk[None, :] == cur[:, None] - 1)
        imp = jnp.where(forced, FORCED_SCORE, jnp.where(blk[None, :] * CMP_BLOCK <= t[:, None], imp, -1.0))
        _, idx = lax.top_k(imp, n_sel)
        gk = ks_b[bi, ki, idx].reshape(B, N_KV, qblk, n_sel * CMP_BLOCK, HEAD_DIM)
        gv = vs_b[bi, ki, idx].reshape(B, N_KV, qblk, n_sel * CMP_BLOCK, HEAD_DIM)
        pos = (idx[..., None] * CMP_BLOCK + jnp.arange(CMP_BLOCK)).reshape(B, N_KV, qblk, n_sel * CMP_BLOCK)
        dist_s = tf[None, None, :, None] - pos.astype(jnp.float32)
        s = jnp.einsum("bqkgd,bkqsd->bkgqs", qb, gk).astype(jnp.float32) * scale - slopes * dist_s[:, :, None]
        p_s = masked_softmax(s, (dist_s >= 0)[:, :, None])
        o_s = jnp.einsum("bkgqs,bkqsd->bqkgd", p_s.astype(gv.dtype), gv)
        start = pos0 - win_base + i * qblk
        wk_b = lax.dynamic_slice_in_dim(kw_p, start, WINDOW + qblk, axis=1)
        wv_b = lax.dynamic_slice_in_dim(vw_p, start, WINDOW + qblk, axis=1)
        kpos = p0 - WINDOW + jnp.arange(WINDOW + qblk)
        dist = t[:, None] - kpos[None, :]
        mask_w = (kpos[None, :] >= 0) & (dist >= 0) & (dist < WINDOW)
        s = jnp.einsum("bqkgd,bskd->bkgqs", qb, wk_b).astype(jnp.float32) * scale - slopes * dist
        p_w = masked_softmax(s, mask_w)
        o_w = jnp.einsum("bkgqs,bskd->bqkgd", p_w.astype(wv_b.dtype), wv_b)
        return gb[..., 0:1] * o_c + gb[..., 1:2] * o_s + gb[..., 2:3] * o_w

    out = lax.map(one_block, (jnp.arange(nqb), q_blocks, g_blocks))
    return out.swapaxes(0, 1).reshape(B, Tq, NSA_DIM)


def rwkv7_mix(zr, shift0, wkv0, p):
    B, T, _ = zr.shape
    f32 = jnp.float32
    z_prev = jnp.concatenate([shift0[:, None].astype(zr.dtype), zr[:, :-1]], axis=1)
    zs = zr + (z_prev - zr) * p["rw_mu"]
    r, wl, k, v, al, gl = split_last(zs, RW_SIZES)
    w_log = -jax.nn.softplus(-(p["rw_w0"] + jnp.tanh(wl) @ p["rw_w2"]).astype(f32)) - 0.5
    decay = jnp.exp(-jnp.exp(w_log))
    a = jax.nn.sigmoid((p["rw_a0"] + al @ p["rw_a2"]).astype(f32))
    g = (jax.nn.sigmoid(gl) @ p["rw_g2"]).astype(f32)
    hs = (B, T, RW_HEADS, RW_N)
    r = r.astype(f32).reshape(hs)
    k = k.astype(f32)
    v = v.astype(f32).reshape(hs)
    kk = (k * p["rw_k_k"]).reshape(hs)
    kk = kk / jnp.maximum(jnp.sqrt(jnp.sum(kk * kk, axis=-1, keepdims=True)), 1e-12)
    k = (k * (1.0 + (a - 1.0) * p["rw_k_a"])).reshape(hs)
    a = a.reshape(hs)
    decay = decay.reshape(hs)

    def step(S, inp):
        r_t, d_t, k_t, v_t, kk_t, ka_t = inp
        sa = jnp.einsum("bhvk,bhk->bhv", S, -kk_t)
        S = S * d_t[:, :, None, :] + sa[..., None] * ka_t[:, :, None, :] + v_t[..., None] * k_t[:, :, None, :]
        return S, jnp.einsum("bhvk,bhk->bhv", S, r_t)

    xs = tuple(jnp.swapaxes(u, 0, 1) for u in (r, decay, k, v, kk, kk * a))
    wkv, ys = lax.scan(step, wkv0.astype(f32), xs)
    y = jnp.swapaxes(ys, 0, 1)
    mu = jnp.mean(y, axis=-1, keepdims=True)
    var = jnp.mean(jnp.square(y - mu), axis=-1, keepdims=True)
    y = ((y - mu) * lax.rsqrt(var + GN_EPS)).reshape(B, T, RW_DIM) * p["rw_ln_w"] + p["rw_ln_b"]
    bonus = jnp.sum(r * k * p["rw_r_k"], axis=-1, keepdims=True) * v
    out = (y + bonus.reshape(B, T, RW_DIM)) * g
    return out.astype(zr.dtype), zr[:, -1], wkv.astype(wkv0.dtype)


def decoder_layer(x, pos0, p, past):
    B, T, _ = x.shape
    x = x + 0.5 * swiglu(rms_norm(x, p["n_ffn1"]), p["ffn1_gate"], p["ffn1_up"], p["ffn1_down"])
    h = rms_norm(x, p["n_mix"])
    z = h @ p["w_in"]
    z_nsa, z_rw, z_mrg = split_last(z, (NSA_COLS, RW_COLS, MERGE_COLS))
    q, kc, vc, ks, vs, kw, vw, ga = split_last(z_nsa, NSA_SIZES)
    kv_shape = (B, T, N_KV, HEAD_DIM)
    q = rms_norm(q.reshape(B, T, N_KV, GROUP, HEAD_DIM), p["g_q"])
    kc = kc.reshape(kv_shape)
    vc = vc.reshape(kv_shape)
    vs = vs.reshape(kv_shape)
    vw = vw.reshape(kv_shape)
    ks = rms_norm(ks.reshape(kv_shape), p["g_ks"])
    kw = rms_norm(kw.reshape(kv_shape), p["g_kw"])
    gates = jax.nn.sigmoid(ga.reshape(B, T, N_KV, GROUP, 3))
    if past is None:
        kc_all, vc_all, ks_all, vs_all, kw_all, vw_all = kc, vc, ks, vs, kw, vw
        win_base = 0
        n_buf = min(WINDOW, T)
        shift0 = jnp.zeros((B, RW_COLS), z.dtype)
        wkv0 = jnp.zeros((B, RW_HEADS, RW_N, RW_N), x.dtype)
    else:
        table = past["page_table"]
        kc_all = jnp.concatenate([gather_pages(past["cmp_k"], table).astype(kc.dtype), kc], axis=1)
        vc_all = jnp.concatenate([gather_pages(past["cmp_v"], table).astype(vc.dtype), vc], axis=1)
        ks_all = jnp.concatenate([gather_pages(past["slc_k"], table).astype(ks.dtype), ks], axis=1)
        vs_all = jnp.concatenate([gather_pages(past["slc_v"], table).astype(vs.dtype), vs], axis=1)
        kw_all = jnp.concatenate([past["win_k"].astype(kw.dtype), kw], axis=1)
        vw_all = jnp.concatenate([past["win_v"].astype(vw.dtype), vw], axis=1)
        n_buf = past["win_k"].shape[1]
        win_base = pos0 - n_buf
        shift0 = past["shift"]
        wkv0 = past["wkv"]
    o_a = nsa_attend(q, gates, pos0, kc_all, vc_all, ks_all, vs_all, kw_all, vw_all, win_base, p)
    o_b, shift_new, wkv_new = rwkv7_mix(z_rw, shift0, wkv0, p)
    g_a, g_b = jnp.split(z_mrg, 2, axis=-1)
    m = jax.nn.sigmoid(g_a) * (o_a @ p["w_pa"]) + jax.nn.sigmoid(g_b) * (o_b @ p["w_pb"])
    x = x + m @ p["w_out"]
    x = x + 0.5 * swiglu(rms_norm(x, p["n_ffn2"]), p["ffn2_gate"], p["ffn2_up"], p["ffn2_down"])
    states = (kc, vc, ks, vs, kw_all[:, -n_buf:], vw_all[:, -n_buf:], shift_new, wkv_new)
    return x, states


def setup_inputs(seed: int = 0) -> dict:
    key = jax.random.key(seed)
    keys = iter(jax.random.split(key, 64))

    def nrm(shape, scale=1.0):
        return jax.random.normal(next(keys), shape, jnp.float32) * scale

    def gain(n):
        return 1.0 + nrm((DEPTH, n), 0.02)

    n_pages = PAST_LEN // PAGE_SIZE
    n_phys = (5 * DEC_BATCH * n_pages + 3) // 4
    n_buf = min(WINDOW, PAST_LEN)
    page_table = jax.random.permutation(next(keys), n_phys)[: DEC_BATCH * n_pages].reshape(DEC_BATCH, n_pages).astype(jnp.int32)
    pool = (DEPTH, n_phys, PAGE_SIZE, N_KV, HEAD_DIM)
    wbuf = (DEPTH, DEC_BATCH, n_buf, N_KV, HEAD_DIM)
    return {
        "x_prompt": nrm((BATCH, SEQ, D_MODEL)),
        "x_sample": nrm((DEC_BATCH, DEC_SEQ, D_MODEL)),
        "cache_cmp_k": nrm(pool),
        "cache_cmp_v": nrm(pool),
        "cache_slc_k": nrm(pool),
        "cache_slc_v": nrm(pool),
        "cache_win_k": nrm(wbuf),
        "cache_win_v": nrm(wbuf),
        "state_shift": nrm((DEPTH, DEC_BATCH, RW_COLS)),
        "state_wkv": nrm((DEPTH, DEC_BATCH, RW_HEADS, RW_N, RW_N), 0.3),
        "page_table": page_table,
        "n_ffn1": gain(D_MODEL),
        "ffn1_gate": nrm((DEPTH, D_MODEL, D_FF), D_MODEL ** -0.5),
        "ffn1_up": nrm((DEPTH, D_MODEL, D_FF), D_MODEL ** -0.5),
        "ffn1_down": nrm((DEPTH, D_FF, D_MODEL), D_FF ** -0.5),
        "n_mix": gain(D_MODEL),
        "w_in": nrm((DEPTH, D_MODEL, IN_COLS), D_MODEL ** -0.5),
        "g_q": gain(HEAD_DIM),
        "g_kc": gain(HEAD_DIM),
        "g_ks": gain(HEAD_DIM),
        "g_kw": gain(HEAD_DIM),
        "w_cmp_k": nrm((DEPTH, CMP_BLOCK, HEAD_DIM, HEAD_DIM), (CMP_BLOCK * HEAD_DIM) ** -0.5),
        "pe_cmp_k": nrm((DEPTH, CMP_BLOCK, HEAD_DIM), 0.1),
        "w_cmp_v": nrm((DEPTH, CMP_BLOCK, HEAD_DIM, HEAD_DIM), (CMP_BLOCK * HEAD_DIM) ** -0.5),
        "pe_cmp_v": nrm((DEPTH, CMP_BLOCK, HEAD_DIM), 0.1),
        "rw_mu": jax.random.uniform(next(keys), (DEPTH, RW_COLS), jnp.float32),
        "rw_w0": -2.0 + nrm((DEPTH, RW_DIM), 0.5),
        "rw_w2": nrm((DEPTH, W_LORA, RW_DIM), 0.1 * W_LORA ** -0.5),
        "rw_a0": nrm((DEPTH, RW_DIM), 0.1),
        "rw_a2": nrm((DEPTH, A_LORA, RW_DIM), A_LORA ** -0.5),
        "rw_g2": nrm((DEPTH, G_LORA, RW_DIM), G_LORA ** -0.5),
        "rw_k_k": 0.85 + nrm((DEPTH, RW_DIM), 0.1),
        "rw_k_a": 1.0 + nrm((DEPTH, RW_DIM), 0.1),
        "rw_r_k": nrm((DEPTH, RW_HEADS, RW_N), 0.1),
        "rw_ln_w": gain(RW_DIM),
        "rw_ln_b": nrm((DEPTH, RW_DIM), 0.02),
        "w_pa": nrm((DEPTH, NSA_DIM, D_MODEL), NSA_DIM ** -0.5),
        "w_pb": nrm((DEPTH, RW_DIM, D_MODEL), RW_DIM ** -0.5),
        "w_out": nrm((DEPTH, D_MODEL, D_MODEL), D_MODEL ** -0.5),
        "n_ffn2": gain(D_MODEL),
        "ffn2_gate": nrm((DEPTH, D_MODEL, D_FF), D_MODEL ** -0.5),
        "ffn2_up": nrm((DEPTH, D_MODEL, D_FF), D_MODEL ** -0.5),
        "ffn2_down": nrm((DEPTH, D_FF, D_MODEL), D_FF ** -0.5),
    }


def reference(x_prompt, x_sample, cache_cmp_k, cache_cmp_v, cache_slc_k, cache_slc_v, cache_win_k, cache_win_v,
              state_shift, state_wkv, page_table,
              n_ffn1, ffn1_gate, ffn1_up, ffn1_down, n_mix, w_in, g_q, g_kc, g_ks, g_kw,
              w_cmp_k, pe_cmp_k, w_cmp_v, pe_cmp_v,
              rw_mu, rw_w0, rw_w2, rw_a0, rw_a2, rw_g2, rw_k_k, rw_k_a, rw_r_k, rw_ln_w, rw_ln_b,
              w_pa, w_pb, w_out, n_ffn2, ffn2_gate, ffn2_up, ffn2_down):
    past_len = page_table.shape[1] * cache_cmp_k.shape[2]
    y_prompt, y_sample = x_prompt, x_sample
    st_p, st_s = [], []
    for l in range(DEPTH):
        p = dict(n_ffn1=n_ffn1[l], ffn1_gate=ffn1_gate[l], ffn1_up=ffn1_up[l], ffn1_down=ffn1_down[l],
                 n_mix=n_mix[l], w_in=w_in[l], g_q=g_q[l], g_kc=g_kc[l], g_ks=g_ks[l], g_kw=g_kw[l],
                 w_cmp_k=w_cmp_k[l], pe_cmp_k=pe_cmp_k[l], w_cmp_v=w_cmp_v[l], pe_cmp_v=pe_cmp_v[l],
                 rw_mu=rw_mu[l], rw_w0=rw_w0[l], rw_w2=rw_w2[l], rw_a0=rw_a0[l], rw_a2=rw_a2[l], rw_g2=rw_g2[l],
                 rw_k_k=rw_k_k[l], rw_k_a=rw_k_a[l], rw_r_k=rw_r_k[l], rw_ln_w=rw_ln_w[l], rw_ln_b=rw_ln_b[l],
                 w_pa=w_pa[l], w_pb=w_pb[l], w_out=w_out[l],
                 n_ffn2=n_ffn2[l], ffn2_gate=ffn2_gate[l], ffn2_up=ffn2_up[l], ffn2_down=ffn2_down[l])
        y_prompt, s_p = decoder_layer(y_prompt, 0, p, None)
        past = dict(page_table=page_table, cmp_k=cache_cmp_k[l], cmp_v=cache_cmp_v[l], slc_k=cache_slc_k[l],
                    slc_v=cache_slc_v[l], win_k=cache_win_k[l], win_v=cache_win_v[l], shift=state_shift[l],
                    wkv=state_wkv[l])
        y_sample, s_s = decoder_layer(y_sample, past_len, p, past)
        st_p.append(s_p)
        st_s.append(s_s)
    cmp_k_p, cmp_v_p, slc_k_p, slc_v_p, win_k_p, win_v_p, shift_p, wkv_p = [jnp.stack(c) for c in zip(*st_p)]
    cmp_k_s, cmp_v_s, slc_k_s, slc_v_s, win_k_s, win_v_s, shift_s, wkv_s = [jnp.stack(c) for c in zip(*st_s)]
    return (y_prompt, y_sample,
            cmp_k_p, cmp_v_p, slc_k_p, slc_v_p, win_k_p, win_v_p, shift_p, wkv_p,
            cmp_k_s, cmp_v_s, slc_k_s, slc_v_s, win_k_s, win_v_s, shift_s, wkv_s)
```

```python
import functools

import jax
import jax.numpy as jnp
from jax import lax
from jax.experimental import pallas as pl
from jax.experimental.pallas import tpu as pltpu

f32 = jnp.float32
bf16 = jnp.bfloat16

D_MODEL = 2048
N_HEADS = 16
N_KV = 4
GROUP = 4
HEAD_DIM = 64
NSA_DIM = N_HEADS * HEAD_DIM
KV_DIM = N_KV * HEAD_DIM
CMP_BLOCK = 64
N_SEL = 16
WINDOW = 512
Q_BLOCK = 128
FORCED_SCORE = 1e3
RW_HEADS = 16
RW_N = 64
RW_DIM = RW_HEADS * RW_N
W_LORA = 96
A_LORA = 96
G_LORA = 256
D_FF = 5632
NORM_EPS = 1e-6
GN_EPS = 64e-5
NEG_BIG = -1e30

VMEM_LIMIT = 56 * 1024 * 1024

_SEGS = (
    ("r", 2608, 1024, 1024),
    ("k", 3728, 1024, 1024),
    ("v", 4752, 1024, 1024),
    ("q", 0, 1024, 1024),
    ("g_a", 6128, 2048, 2048),
    ("g_b", 8176, 2048, 2048),
    ("kc", 1024, 256, 256),
    ("vc", 1280, 256, 256),
    ("ks", 1536, 256, 256),
    ("vs", 1792, 256, 256),
    ("kw", 2048, 256, 256),
    ("vw", 2304, 256, 256),
    ("gl", 5872, 256, 256),
    ("ga", 2560, 48, 128),
    ("wl", 3632, 96, 128),
    ("al", 5776, 96, 128),
)
_Z_COLS = 10752


def _seg_offsets():
    offs, o = {}, 0
    for name, _, w, pw in _SEGS:
        assert o % pw == 0
        offs[name] = (o, w, pw)
        o += pw
    return offs, o


_OFF, _USED = _seg_offsets()


_IN_COLS = 10224
_RW_START = 2608
_RW_COLS = 3520


def _pad_cols(x):
    parts = []
    for _, s, w, pw in _SEGS:
        seg = x[..., s:s + w]
        if pw != w:
            seg = jnp.pad(seg, [(0, 0)] * (x.ndim - 1) + [(0, pw - w)])
        parts.append(seg)
    parts.append(jnp.zeros(x.shape[:-1] + (_Z_COLS - _USED,), x.dtype))
    return jnp.concatenate(parts, axis=-1)


def _pad_rw_cols(x):
    pad = [(0, 0)] * (x.ndim - 1) + [(_RW_START, _IN_COLS - _RW_START - _RW_COLS)]
    return _pad_cols(jnp.pad(x, pad))


def _seg(z, name):
    o, w, _ = _OFF[name]
    return z[..., o:o + w]


def _rw_cols(z):
    return jnp.concatenate([_seg(z, n) for n in ("r", "wl", "k", "v", "al", "gl")], axis=-1)


def _cparams(sem, vmem=VMEM_LIMIT):
    return pltpu.CompilerParams(dimension_semantics=sem, vmem_limit_bytes=vmem)


def _row_tile(m, pref):
    return pref if m % pref == 0 else m


def _rms(x, g):
    ms = jnp.mean(x * x, axis=-1, keepdims=True)
    return x * lax.rsqrt(ms + NORM_EPS) * g


def _ffn_kernel(x_ref, g_ref, wg_ref, wu_ref, wd_ref, o_ref, h_ref, acc_ref):
    j = pl.program_id(1)

    @pl.when(j == 0)
    def _():
        h_ref[...] = _rms(x_ref[...], g_ref[...]).astype(bf16)
        acc_ref[...] = jnp.zeros_like(acc_ref)

    h = h_ref[...]
    g = jnp.dot(h, wg_ref[...], preferred_element_type=f32)
    u = jnp.dot(h, wu_ref[...], preferred_element_type=f32)
    a = (g * jax.nn.sigmoid(g) * u).astype(bf16)
    acc_ref[...] += jnp.dot(a, wd_ref[...], preferred_element_type=f32)

    @pl.when(j == pl.num_programs(1) - 1)
    def _():
        o_ref[...] = x_ref[...] + 0.5 * acc_ref[...]


def _ffn(x, gain, wg, wu, wd):
    m, d = x.shape
    ff = wg.shape[1]
    bm = _row_tile(m, 512)
    bf = 512
    return pl.pallas_call(
        _ffn_kernel,
        out_shape=jax.ShapeDtypeStruct((m, d), f32),
        grid=(m // bm, ff // bf),
        in_specs=[
            pl.BlockSpec((bm, d), lambda i, j: (i, 0)),
            pl.BlockSpec((1, d), lambda i, j: (0, 0)),
            pl.BlockSpec((d, bf), lambda i, j: (0, j)),
            pl.BlockSpec((d, bf), lambda i, j: (0, j)),
            pl.BlockSpec((bf, d), lambda i, j: (j, 0)),
        ],
        out_specs=pl.BlockSpec((bm, d), lambda i, j: (i, 0)),
        scratch_shapes=[pltpu.VMEM((bm, d), bf16), pltpu.VMEM((bm, d), f32)],
        compiler_params=_cparams(("parallel", "arbitrary")),
        name="ffn",
    )(x, gain.reshape(1, d), wg, wu, wd)


def _inproj_kernel(x_ref, g_ref, w_ref, o_ref, h_ref):
    @pl.when(pl.program_id(1) == 0)
    def _():
        h_ref[...] = _rms(x_ref[...], g_ref[...]).astype(bf16)

    o_ref[...] = jnp.dot(h_ref[...], w_ref[...], preferred_element_type=f32)


def _inproj(x, gain, w):
    m, d = x.shape
    n = w.shape[1]
    bm = _row_tile(m, 512)
    bn = 512
    return pl.pallas_call(
        _inproj_kernel,
        out_shape=jax.ShapeDtypeStruct((m, n), f32),
        grid=(m // bm, n // bn),
        in_specs=[
            pl.BlockSpec((bm, d), lambda i, j: (i, 0)),
            pl.BlockSpec((1, d), lambda i, j: (0, 0)),
            pl.BlockSpec((d, bn), lambda i, j: (0, j)),
        ],
        out_specs=pl.BlockSpec((bm, bn), lambda i, j: (i, j)),
        scratch_shapes=[pltpu.VMEM((bm, d), bf16)],
        compiler_params=_cparams(("parallel", "arbitrary")),
        name="inproj",
    )(x, gain.reshape(1, d), w)


def _merge_kernel(oa_ref, ob_ref, ga_ref, gb_ref, x_ref, wpa_ref, wpb_ref, wo_ref, o_ref):
    pa = jnp.dot(oa_ref[...], wpa_ref[...], preferred_element_type=f32)
    pb = jnp.dot(ob_ref[...], wpb_ref[...], preferred_element_type=f32)
    mix = jax.nn.sigmoid(ga_ref[...]) * pa + jax.nn.sigmoid(gb_ref[...]) * pb
    o_ref[...] = x_ref[...] + jnp.dot(mix.astype(bf16), wo_ref[...], preferred_element_type=f32)


def _merge(oa, ob, z, x, wpa, wpb, wo):
    m, d = x.shape
    bm = _row_tile(m, 256)
    ca = _OFF["g_a"][0] // d
    cb = _OFF["g_b"][0] // d
    return pl.pallas_call(
        _merge_kernel,
        out_shape=jax.ShapeDtypeStruct((m, d), f32),
        grid=(m // bm,),
        in_specs=[
            pl.BlockSpec((bm, NSA_DIM), lambda i: (i, 0)),
            pl.BlockSpec((bm, RW_DIM), lambda i: (i, 0)),
            pl.BlockSpec((bm, d), lambda i: (i, ca)),
            pl.BlockSpec((bm, d), lambda i: (i, cb)),
            pl.BlockSpec((bm, d), lambda i: (i, 0)),
            pl.BlockSpec((NSA_DIM, d), lambda i: (0, 0)),
            pl.BlockSpec((RW_DIM, d), lambda i: (0, 0)),
            pl.BlockSpec((d, d), lambda i: (0, 0)),
        ],
        out_specs=pl.BlockSpec((bm, d), lambda i: (i, 0)),
        compiler_params=_cparams(("parallel",)),
        name="merge",
    )(oa, ob, z, z, x, wpa, wpb, wo)


def _compress_kernel(x_ref, pe_ref, w_ref, o_ref):
    nb = o_ref.shape[1]
    acc = [jnp.zeros((nb, 128), f32), jnp.zeros((nb, 128), f32)]
    for j in range(CMP_BLOCK):
        pe_j = pe_ref[pl.ds(j, 1), :]
        w_j = w_ref[j]
        for h in range(2):
            xj = x_ref[0, pl.ds(2 * j + h, nb, stride=2 * CMP_BLOCK), :] + pe_j
            acc[h] = acc[h] + jnp.dot(xj.astype(bf16), w_j, preferred_element_type=f32)
    o_ref[0] = jnp.concatenate(acc, axis=-1)


def _compress(x, pe, w):
    b, l, _ = x.shape
    nb = l // CMP_BLOCK
    pe_t = jnp.tile(pe, (1, 2))
    eye = jnp.eye(2, dtype=f32)
    wbd = jnp.einsum("ab,jde->jadbe", eye, w).reshape(CMP_BLOCK, 128, 128).astype(bf16)
    return pl.pallas_call(
        _compress_kernel,
        out_shape=jax.ShapeDtypeStruct((b, nb, KV_DIM), f32),
        grid=(b,),
        in_specs=[
            pl.BlockSpec((1, 2 * l, 128), lambda i: (i, 0, 0)),
            pl.BlockSpec((CMP_BLOCK, 128), lambda i: (0, 0)),
            pl.BlockSpec((CMP_BLOCK, 128, 128), lambda i: (0, 0, 0)),
        ],
        out_specs=pl.BlockSpec((1, nb, KV_DIM), lambda i: (i, 0, 0)),
        compiler_params=_cparams(("parallel",)),
        name="compress",
    )(x.reshape(b, 2 * l, 128), pe_t, wbd)


_TK = 512
_ROWS = GROUP * Q_BLOCK


def _softmax_rows(s):
    m = jnp.max(s, axis=-1, keepdims=True)
    m = jnp.where(jnp.isfinite(m), m, 0.0)
    p = jnp.exp(s - m)
    return p / jnp.maximum(jnp.sum(p, axis=-1, keepdims=True), 1e-30)


def _topk_mask_t(vt, n_sel):
    nb = vt.shape[0]
    bi = lax.broadcasted_iota(jnp.int32, vt.shape, 0)
    sel = jnp.zeros(vt.shape, f32)
    for _ in range(n_sel):
        mx = jnp.max(vt, axis=0, keepdims=True)
        idx = jnp.min(jnp.where(vt == mx, bi, nb), axis=0, keepdims=True)
        hit = bi == idx
        sel = jnp.where(hit, 1.0, sel)
        vt = jnp.where(hit, -jnp.inf, vt)
    return sel


def _nsa_kernel(q_ref, gt_ref, ckt_ref, cv_ref, kst_ref, vs_ref, kwt_ref, vw_ref, e_ref,
                o_ref, a0_ref, a0w_ref):
    kv = pl.program_id(0)
    i = pl.program_id(1)
    nb = ckt_ref.shape[-1]
    n_sel = min(N_SEL, nb)
    wk = WINDOW + Q_BLOCK

    row = lax.broadcasted_iota(jnp.int32, (_ROWS, 1), 0)
    grp = row // Q_BLOCK
    tl = row % Q_BLOCK
    slope = jnp.exp2(-0.5 * (kv * GROUP + grp + 1).astype(f32))
    tlf = tl.astype(f32)

    @pl.when(i == 0)
    def _():
        col = lax.broadcasted_iota(jnp.int32, (1, _TK), 1).astype(f32)
        a0_ref[...] = slope * (tlf - col)
        colw = lax.broadcasted_iota(jnp.int32, (1, wk), 1).astype(f32)
        a0w_ref[...] = slope * (tlf + float(WINDOW) - colw)

    q = q_ref[0, 0]
    t0 = i * Q_BLOCK
    tok = t0 + tl

    blk = lax.broadcasted_iota(jnp.int32, (1, nb), 1)
    blk_mid = (blk * CMP_BLOCK).astype(f32) + (CMP_BLOCK - 1) / 2
    s = jnp.dot(q, ckt_ref[0], preferred_element_type=f32)
    s = s - slope * (tok.astype(f32) - blk_mid)
    s = jnp.where(blk * CMP_BLOCK + (CMP_BLOCK - 1) <= tok, s, -jnp.inf)
    p_c = _softmax_rows(s)
    o_c = jnp.dot(p_c.astype(bf16), cv_ref[0], preferred_element_type=f32)

    imp = p_c[0:Q_BLOCK]
    for g in range(1, GROUP):
        imp = imp + p_c[g * Q_BLOCK:(g + 1) * Q_BLOCK]
    tq = t0 + lax.broadcasted_iota(jnp.int32, (Q_BLOCK, 1), 0)
    cur = tq // CMP_BLOCK
    forced = (blk == 0) | (blk == cur) | (blk == cur - 1)
    imp = jnp.where(forced, FORCED_SCORE, jnp.where(blk * CMP_BLOCK <= tq, imp, -1.0))
    sel_t = _topk_mask_t(imp.T, n_sel)
    bias = jnp.where(sel_t > 0.0, 0.0, NEG_BIG).T
    lb = jnp.concatenate([bias] * GROUP, axis=0).astype(bf16)

    def tile(j, carry, causal):
        m, l, acc = carry
        c0 = pl.multiple_of(j * _TK, _TK)
        s = jnp.dot(q, kst_ref[0, :, pl.ds(c0, _TK)], preferred_element_type=f32)
        s = s + jnp.dot(lb, e_ref[:, pl.ds(c0, _TK)], preferred_element_type=f32)
        s = s - a0_ref[...]
        if causal:
            col = lax.broadcasted_iota(jnp.int32, (1, _TK), 1)
            s = jnp.where(tok - (c0 + col) >= 0, s, -jnp.inf)
        off = slope * (t0 - c0).astype(f32)
        m_new = jnp.maximum(m, jnp.max(s, axis=-1, keepdims=True) - off)
        p = jnp.exp(s - (m_new + off))
        alpha = jnp.exp(m - m_new)
        l = alpha * l + jnp.sum(p, axis=-1, keepdims=True)
        acc = alpha * acc + jnp.dot(p.astype(bf16), vs_ref[0, pl.ds(c0, _TK), :],
                                    preferred_element_type=f32)
        return m_new, l, acc

    jd = t0 // _TK
    init = (jnp.full((_ROWS, 1), -jnp.inf, f32), jnp.zeros((_ROWS, 1), f32),
            jnp.zeros((_ROWS, HEAD_DIM), f32))
    carry = lax.fori_loop(0, jd, lambda j, c: tile(j, c, False), init)
    _, l, acc = tile(jd, carry, True)
    o_s = acc / jnp.maximum(l, 1e-30)

    w0 = pl.multiple_of(t0, Q_BLOCK)
    s = jnp.dot(q, kwt_ref[0, :, pl.ds(w0, wk)], preferred_element_type=f32) - a0w_ref[...]
    colw = lax.broadcasted_iota(jnp.int32, (1, wk), 1)
    dist = tl + WINDOW - colw
    s = jnp.where((t0 - WINDOW + colw >= 0) & (dist >= 0) & (dist < WINDOW), s, -jnp.inf)
    p_w = _softmax_rows(s)
    o_w = jnp.dot(p_w.astype(bf16), vw_ref[0, pl.ds(w0, wk), :], preferred_element_type=f32)

    gt = gt_ref[0, 0]
    o_ref[0, 0] = gt[:, 0:1] * o_c + gt[:, 1:2] * o_s + gt[:, 2:3] * o_w


def _nsa_prompt(q, gates, ck, cv, ks, vs, kw, vw):
    t = q.shape[0]
    nqb = t // Q_BLOCK
    nb = ck.shape[0]
    scale = HEAD_DIM ** -0.5
    qs = (q * scale).astype(bf16).reshape(nqb, Q_BLOCK, N_KV, GROUP, HEAD_DIM)
    qs = qs.transpose(2, 0, 3, 1, 4).reshape(N_KV, nqb, _ROWS, HEAD_DIM)
    gt = gates.reshape(nqb, Q_BLOCK, N_KV, GROUP, 3).transpose(2, 0, 3, 1, 4).reshape(N_KV, nqb, _ROWS, 3)
    ckt = ck.astype(bf16).transpose(1, 2, 0)
    cvh = cv.astype(bf16).transpose(1, 0, 2)
    kst = ks.astype(bf16).transpose(1, 2, 0)
    vsh = vs.astype(bf16).transpose(1, 0, 2)
    kwt = jnp.pad(kw.astype(bf16).transpose(1, 2, 0), ((0, 0), (0, 0), (WINDOW, 0)))
    vwh = jnp.pad(vw.astype(bf16).transpose(1, 0, 2), ((0, 0), (WINDOW, 0), (0, 0)))
    onehot = (jnp.arange(t)[None, :] // CMP_BLOCK == jnp.arange(nb)[:, None]).astype(bf16)
    out = pl.pallas_call(
        _nsa_kernel,
        out_shape=jax.ShapeDtypeStruct((N_KV, nqb, _ROWS, HEAD_DIM), f32),
        grid=(N_KV, nqb),
        in_specs=[
            pl.BlockSpec((1, 1, _ROWS, HEAD_DIM), lambda k, i: (k, i, 0, 0)),
            pl.BlockSpec((1, 1, _ROWS, 3), lambda k, i: (k, i, 0, 0)),
            pl.BlockSpec((1, HEAD_DIM, nb), lambda k, i: (k, 0, 0)),
            pl.BlockSpec((1, nb, HEAD_DIM), lambda k, i: (k, 0, 0)),
            pl.BlockSpec((1, HEAD_DIM, t), lambda k, i: (k, 0, 0)),
            pl.BlockSpec((1, t, HEAD_DIM), lambda k, i: (k, 0, 0)),
            pl.BlockSpec((1, HEAD_DIM, t + WINDOW), lambda k, i: (k, 0, 0)),
            pl.BlockSpec((1, t + WINDOW, HEAD_DIM), lambda k, i: (k, 0, 0)),
            pl.BlockSpec((nb, t), lambda k, i: (0, 0)),
        ],
        out_specs=pl.BlockSpec((1, 1, _ROWS, HEAD_DIM), lambda k, i: (k, i, 0, 0)),
        scratch_shapes=[pltpu.VMEM((_ROWS, _TK), f32), pltpu.VMEM((_ROWS, WINDOW + Q_BLOCK), f32)],
        compiler_params=_cparams(("arbitrary", "arbitrary")),
        name="nsa_prompt",
    )(qs, gt, ckt, cvh, kst, vsh, kwt, vwh, onehot)
    out = out.reshape(N_KV, nqb, GROUP, Q_BLOCK, HEAD_DIM).transpose(1, 3, 0, 2, 4)
    return out.reshape(t, NSA_DIM)


def _rw_prep_kernel(shift_rows, *refs):
    cur, refs = refs[:6], refs[6:]
    prv, refs = refs[:6], refs[6:]
    if shift_rows:
        st0, refs = refs[:6], refs[6:]
    mus, refs = refs[:6], refs[6:]
    (w0_ref, w2_ref, a0_ref, a2_ref, g2_ref, kk_ref, ka_ref, rk_ref,
     ro_ref, ko_ref, vo_ref, kko_ref, kao_ref, ldo_ref, go_ref, bo_ref) = refs
    first = pl.program_id(0) == 0

    def shifted(n):
        x = cur[n][...]
        if shift_rows:
            prev_row = jnp.where(first, st0[n][pl.ds(7, 1), :], prv[n][pl.ds(7, 1), :])
            rolled = pltpu.roll(x, 1, axis=0)
            rid = lax.broadcasted_iota(jnp.int32, x.shape, 0)
            xp = jnp.where(rid == 0, prev_row, rolled)
        else:
            xp = prv[n][...]
        return x + (xp - x) * mus[n][...]

    r, k, v, gl, wl, al = (shifted(n) for n in range(6))

    y = -(w0_ref[...] + jnp.dot(jnp.tanh(wl).astype(bf16), w2_ref[...], preferred_element_type=f32))
    softplus = jnp.maximum(y, 0.0) + jnp.log1p(jnp.exp(-jnp.abs(y)))
    w_log = -softplus - 0.5
    ld = -jnp.exp(w_log)
    a = jax.nn.sigmoid(a0_ref[...] + jnp.dot(al.astype(bf16), a2_ref[...], preferred_element_type=f32))
    g = jnp.dot(jax.nn.sigmoid(gl).astype(bf16), g2_ref[...], preferred_element_type=f32)
    kk = k * kk_ref[...]
    k2 = k * (1.0 + (a - 1.0) * ka_ref[...])
    rkr = r * k2 * rk_ref[...]
    for h in range(RW_HEADS):
        sl = slice(h * RW_N, (h + 1) * RW_N)
        kkh = kk[:, sl]
        kkh = kkh / jnp.maximum(jnp.sqrt(jnp.sum(kkh * kkh, axis=-1, keepdims=True)), 1e-12)
        ro_ref[h] = r[:, sl]
        ko_ref[h] = k2[:, sl]
        vo_ref[h] = v[:, sl]
        kko_ref[h] = kkh
        kao_ref[h] = kkh * a[:, sl]
        ldo_ref[h] = ld[:, sl]
        go_ref[h] = g[:, sl]
        bo_ref[h] = jnp.sum(rkr[:, sl], axis=-1, keepdims=True) * v[:, sl]


def _rw_prep(z, prev, mu_p, w0, w2p, a0, a2p, g2, k_k, k_a, r_k, shift_rows):
    m = z.shape[0]
    tm = _row_tile(m, 256)
    names = ("r", "k", "v", "gl", "wl", "al")

    def col_spec(name, rows, imap):
        o, _, pw = _OFF[name]
        return pl.BlockSpec((rows, pw), functools.partial(imap, o // pw))

    cur = [col_spec(n, tm, lambda c, i: (i, c)) for n in names]
    if shift_rows:
        blk8 = tm // 8
        prv = [col_spec(n, 8, lambda c, i: (jnp.maximum(i * blk8 - 1, 0), c)) for n in names]
        prv += [col_spec(n, 8, lambda c, i: (0, c)) for n in names]
        prev_args = [z] * 6 + [prev] * 6
    else:
        prv = [col_spec(n, tm, lambda c, i: (i, c)) for n in names]
        prev_args = [prev] * 6
    mus = [col_spec(n, 1, lambda c, i: (0, c)) for n in names]
    vec = pl.BlockSpec((1, RW_DIM), lambda i: (0, 0))
    out_spec = pl.BlockSpec((RW_HEADS, tm, RW_N), lambda i: (0, i, 0))
    outs = pl.pallas_call(
        functools.partial(_rw_prep_kernel, shift_rows),
        out_shape=[jax.ShapeDtypeStruct((RW_HEADS, m, RW_N), f32)] * 8,
        grid=(m // tm,),
        in_specs=cur + prv + mus + [
            vec,
            pl.BlockSpec((128, RW_DIM), lambda i: (0, 0)),
            vec,
            pl.BlockSpec((128, RW_DIM), lambda i: (0, 0)),
            pl.BlockSpec((G_LORA, RW_DIM), lambda i: (0, 0)),
            vec, vec, vec,
        ],
        out_specs=[out_spec] * 8,
        compiler_params=_cparams(("parallel",)),
        name="rw_prep",
    )(*([z] * 6), *prev_args, *([mu_p] * 6),
      w0.reshape(1, RW_DIM), w2p, a0.reshape(1, RW_DIM), a2p, g2,
      k_k.reshape(1, RW_DIM), k_a.reshape(1, RW_DIM), r_k.reshape(1, RW_DIM))
    return outs


_CH = 64
_NN = (((1,), (0,)), ((), ()))
_NT = (((1,), (1,)), ((), ()))
_TN = (((0,), (0,)), ((), ()))


def _split2(x):
    hi = x.astype(bf16)
    lo = (x - hi.astype(f32)).astype(bf16)
    return hi, lo


def _dot3(a, b, dims=_NN):
    ah, al = _split2(a)
    bh, bl = _split2(b)
    d = functools.partial(lax.dot_general, dimension_numbers=dims, preferred_element_type=f32)
    return d(ah, bh) + (d(ah, bl) + d(al, bh))


def _dot_exact_rhs(a01, b, dims=_NN):
    b1 = b.astype(bf16)
    r1 = b - b1.astype(f32)
    b2 = r1.astype(bf16)
    b3 = (r1 - b2.astype(f32)).astype(bf16)
    d = functools.partial(lax.dot_general, dimension_numbers=dims, preferred_element_type=f32)
    return d(a01, b1) + (d(a01, b2) + d(a01, b3))


def _wkv_chunk_kernel(r_ref, k_ref, v_ref, kk_ref, ka_ref, ld_ref, s0_ref, y_ref, sT_ref, st_ref):
    c = pl.program_id(0)

    @pl.when(c == 0)
    def _():
        st_ref[...] = s0_ref[...]

    ti = lax.broadcasted_iota(jnp.int32, (_CH, _CH), 0)
    si = lax.broadcasted_iota(jnp.int32, (_CH, _CH), 1)
    incl = ti >= si
    strict = ti > si
    l_incl = incl.astype(bf16)
    eye = (ti == si).astype(f32)
    ones = jnp.ones((_CH, RW_N), bf16)

    def head(h, carry):
        r = r_ref[h]
        k = k_ref[h]
        v = v_ref[h]
        kk = kk_ref[h]
        ka = ka_ref[h]
        ld = ld_ref[h]
        lp = _dot_exact_rhs(l_incl, ld)
        ld1 = ld.astype(bf16)
        r1 = ld - ld1.astype(f32)
        ld2 = r1.astype(bf16)
        ld3 = (r1 - ld2.astype(f32)).astype(bf16)
        dtn = functools.partial(lax.dot_general, dimension_numbers=_TN, preferred_element_type=f32)
        lp_end = dtn(ld1, ones) + (dtn(ld2, ones) + dtn(ld3, ones))
        e_neg = jnp.exp(-lp)
        at = -kk * jnp.exp(lp - ld)
        bt = ka * e_neg
        kt = k * e_neg
        rt = r * jnp.exp(lp)
        e_end = jnp.exp(lp[_CH - 1:_CH, :] - lp)
        bh = ka * e_end
        kh = k * e_end
        sc = _dot3(jnp.concatenate([at, rt], axis=0), jnp.concatenate([bt, kt], axis=0), _NT)
        a_b = jnp.where(strict, sc[:_CH, :_CH], 0.0)
        a_k = jnp.where(strict, sc[:_CH, _CH:], 0.0)
        g_b = jnp.where(incl, sc[_CH:, :_CH], 0.0)
        g_k = jnp.where(incl, sc[_CH:, _CH:], 0.0)
        tm = eye + a_b
        pw = a_b
        for _ in range(5):
            pw = _dot3(pw, pw)
            tm = tm + _dot3(tm, pw)
        akv = _dot3(a_k, v)
        tx = _dot3(tm, jnp.concatenate([at, akv], axis=1))
        w = tx[:, :RW_N]
        zc = tx[:, RW_N:]
        st = st_ref[h]
        ws = _dot3(jnp.concatenate([w, rt], axis=0), st)
        u = ws[:_CH] + zc
        uv = jnp.concatenate([u, v], axis=0)
        y = ws[_CH:] + _dot3(jnp.concatenate([g_b, g_k], axis=1), uv)
        st_new = jnp.exp(lp_end) * st + _dot3(jnp.concatenate([bh, kh], axis=0), uv, _TN)
        y_ref[h] = y
        st_ref[h] = st_new
        return carry

    lax.fori_loop(0, RW_HEADS, head, 0)

    @pl.when(c == pl.num_programs(0) - 1)
    def _():
        sT_ref[...] = st_ref[...]


def _wkv_chunks(r, k, v, kk, ka, ld, s0t):
    t = r.shape[1]
    spec = pl.BlockSpec((RW_HEADS, _CH, RW_N), lambda c: (0, c, 0))
    sspec = pl.BlockSpec((RW_HEADS, RW_N, RW_N), lambda c: (0, 0, 0))
    return pl.pallas_call(
        _wkv_chunk_kernel,
        out_shape=[jax.ShapeDtypeStruct((RW_HEADS, t, RW_N), f32),
                   jax.ShapeDtypeStruct((RW_HEADS, RW_N, RW_N), f32)],
        grid=(t // _CH,),
        in_specs=[spec] * 6 + [sspec],
        out_specs=[spec, sspec],
        scratch_shapes=[pltpu.VMEM((RW_HEADS, RW_N, RW_N), f32)],
        compiler_params=_cparams(("arbitrary",)),
        name="wkv_chunks",
    )(r, k, v, kk, ka, ld, s0t)


def _rw_post_kernel(y_ref, b_ref, g_ref, lw_ref, lb_ref, o_ref):
    y = y_ref[...]
    mu = jnp.mean(y, axis=-1, keepdims=True)
    var = jnp.mean(jnp.square(y - mu), axis=-1, keepdims=True)
    yn = (y - mu) * lax.rsqrt(var + GN_EPS) * lw_ref[...] + lb_ref[...]
    o_ref[...] = ((yn + b_ref[...]) * g_ref[...]).astype(o_ref.dtype)


def _rw_post(y, bonus, g, ln_w, ln_b):
    m = y.shape[1]
    tm = _row_tile(m, 512)
    spec = pl.BlockSpec((RW_HEADS, tm, RW_N), lambda i: (0, i, 0))
    pspec = pl.BlockSpec((RW_HEADS, 1, RW_N), lambda i: (0, 0, 0))
    return pl.pallas_call(
        _rw_post_kernel,
        out_shape=jax.ShapeDtypeStruct(y.shape, bf16),
        grid=(m // tm,),
        in_specs=[spec, spec, spec, pspec, pspec],
        out_specs=spec,
        compiler_params=_cparams(("parallel",)),
        name="rw_post",
    )(y, bonus, g, ln_w.reshape(RW_HEADS, 1, RW_N), ln_b.reshape(RW_HEADS, 1, RW_N))


def _lora_pad(w):
    return jnp.pad(w, ((0, 128 - w.shape[0]), (0, 0))).astype(bf16)


def _rwkv_prompt(z, shift0_p, s0t, p):
    r, k, v, kk, ka, ld, g, bonus = _rw_prep(
        z, shift0_p, p["mu_p"], p["rw_w0"], p["w2p"], p["rw_a0"], p["a2p"], p["g2"],
        p["rw_k_k"], p["rw_k_a"], p["rw_r_k"], True)
    y, st = _wkv_chunks(r, k, v, kk, ka, ld, s0t)
    return _rw_post(y, bonus, g, p["rw_ln_w"], p["rw_ln_b"]), st


def _wkv_step_kernel(s_ref, r_ref, k_ref, vc_ref, kk_ref, ka_ref, ld_ref, y_ref, so_ref):
    s = s_ref[0]
    kk = kk_ref[0][:, None, :]
    sa = jnp.sum(s * (-kk), axis=-1, keepdims=True)
    s = (s * jnp.exp(ld_ref[0])[:, None, :] + sa * ka_ref[0][:, None, :]
         + vc_ref[0] * k_ref[0][:, None, :])
    so_ref[0] = s
    y_ref[0] = jnp.sum(s * r_ref[0][:, None, :], axis=-1, keepdims=True)


def _wkv_step(s0, r, k, v, kk, ka, ld):
    b = s0.shape[0]
    sspec = pl.BlockSpec((1, RW_HEADS, RW_N, RW_N), lambda i: (i, 0, 0, 0))
    vspec = pl.BlockSpec((1, RW_HEADS, RW_N), lambda i: (i, 0, 0))
    cspec = pl.BlockSpec((1, RW_HEADS, RW_N, 1), lambda i: (i, 0, 0, 0))
    y, s1 = pl.pallas_call(
        _wkv_step_kernel,
        out_shape=[jax.ShapeDtypeStruct((b, RW_HEADS, RW_N, 1), f32),
                   jax.ShapeDtypeStruct(s0.shape, f32)],
        grid=(b,),
        in_specs=[sspec, vspec, vspec, cspec, vspec, vspec, vspec],
        out_specs=[cspec, sspec],
        compiler_params=_cparams(("parallel",)),
        name="wkv_step",
    )(s0, r, k, v[..., None], kk, ka, ld)
    return y[..., 0], s1


_GP = 8


def _gather_kernel(pt_ref, a_ref, b_ref, oa_ref, ob_ref):
    del pt_ref
    oa_ref[...] = a_ref[...]
    ob_ref[...] = b_ref[...]


def _gather_pages(pool_a, pool_b, table):
    _, page, c = pool_a.shape
    b, npg = table.shape
    ispec = pl.BlockSpec((1, page, c), lambda i, j, pt: (pt[i, j], 0, 0))
    ospec = pl.BlockSpec((1, page, c), lambda i, j, pt: (i, j, 0))
    shape = jax.ShapeDtypeStruct((b, npg * page, c), pool_a.dtype)
    return pl.pallas_call(
        _gather_kernel,
        out_shape=[shape, shape],
        grid_spec=pltpu.PrefetchScalarGridSpec(
            num_scalar_prefetch=1, grid=(b, npg), in_specs=[ispec, ispec], out_specs=[ospec, ospec]),
        compiler_params=_cparams(("parallel", "arbitrary")),
        name="gather_pages",
    )(table, pool_a, pool_b)


def _dec_cmp_kernel(n_pick, t_pos, q_ref, ckt_ref, cv_ref, xk_ref, xv_ref, wk_ref, wv_ref, gkc_ref,
                    oc_ref, idx_ref):
    nb = ckt_ref.shape[-1]
    blk = lax.broadcasted_iota(jnp.int32, (1, nb), 1)
    blk_mid = (blk * CMP_BLOCK).astype(f32) + (CMP_BLOCK - 1) / 2
    ck_new = _rms(jnp.dot(xk_ref[0].astype(bf16), wk_ref[...], preferred_element_type=f32), gkc_ref[...])
    cv_new = jnp.dot(xv_ref[0].astype(bf16), wv_ref[...], preferred_element_type=f32)
    new_mid = float(nb * CMP_BLOCK) + (CMP_BLOCK - 1) / 2
    new_ok = nb * CMP_BLOCK + (CMP_BLOCK - 1) <= t_pos
    lane = lax.broadcasted_iota(jnp.int32, (1, 128), 1)
    imps = []
    for kv in range(N_KV):
        q = q_ref[0, kv]
        g1 = lax.broadcasted_iota(jnp.int32, (_GP, 1), 0) + (kv * GROUP + 1)
        slope = jnp.exp2(-0.5 * g1.astype(f32))
        s = jnp.dot(q.astype(bf16), ckt_ref[0, kv], preferred_element_type=f32)
        s = s - slope * (float(t_pos) - blk_mid)
        s = jnp.where(blk * CMP_BLOCK + (CMP_BLOCK - 1) <= t_pos, s, -jnp.inf)
        qn = q.astype(bf16).astype(f32)
        s_new = jnp.sum(qn * ck_new[kv:kv + 1].astype(bf16).astype(f32), axis=-1, keepdims=True)
        s_new = s_new - slope * (float(t_pos) - new_mid)
        s_new = jnp.where(new_ok, s_new, -jnp.inf)
        m = jnp.maximum(jnp.max(s, axis=-1, keepdims=True), s_new)
        m = jnp.where(jnp.isfinite(m), m, 0.0)
        p = jnp.exp(s - m)
        p_new = jnp.exp(s_new - m)
        den = jnp.maximum(jnp.sum(p, axis=-1, keepdims=True) + p_new, 1e-30)
        p = p / den
        p_new = p_new / den
        oc = jnp.dot(p.astype(bf16), cv_ref[0, kv], preferred_element_type=f32)
        oc_ref[0, kv] = oc + p_new * cv_new[kv:kv + 1]
        imps.append(jnp.sum(p[:GROUP], axis=0, keepdims=True))
    imp = jnp.concatenate(imps, axis=0)
    cur = t_pos // CMP_BLOCK
    forced = (blk == 0) | (blk == cur) | (blk == cur - 1)
    v = jnp.where(forced, FORCED_SCORE, jnp.where(blk * CMP_BLOCK <= t_pos, imp, -1.0))
    out = jnp.zeros((N_KV, 128), jnp.int32)
    for it in range(n_pick):
        mx = jnp.max(v, axis=-1, keepdims=True)
        idx = jnp.min(jnp.where(v == mx, blk, nb), axis=-1, keepdims=True)
        out = jnp.where(lane == it, idx, out)
        v = jnp.where(blk == idx, -jnp.inf, v)
    idx_ref[0] = out


def _dec_cmp(q, ck, cv, xk, xv, wk, wv, g_kc, t_pos, n_pick):
    b, nb = ck.shape[:2]
    ckt = ck.astype(bf16).transpose(0, 2, 3, 1)
    cvh = cv.astype(bf16).transpose(0, 2, 1, 3)
    kd = CMP_BLOCK * HEAD_DIM
    return pl.pallas_call(
        functools.partial(_dec_cmp_kernel, n_pick, t_pos),
        out_shape=[jax.ShapeDtypeStruct((b, N_KV, _GP, HEAD_DIM), f32),
                   jax.ShapeDtypeStruct((b, N_KV, 128), jnp.int32)],
        grid=(b,),
        in_specs=[
            pl.BlockSpec((1, N_KV, _GP, HEAD_DIM), lambda i: (i, 0, 0, 0)),
            pl.BlockSpec((1, N_KV, HEAD_DIM, nb), lambda i: (i, 0, 0, 0)),
            pl.BlockSpec((1, N_KV, nb, HEAD_DIM), lambda i: (i, 0, 0, 0)),
            pl.BlockSpec((1, _GP, kd), lambda i: (i, 0, 0)),
            pl.BlockSpec((1, _GP, kd), lambda i: (i, 0, 0)),
            pl.BlockSpec((kd, HEAD_DIM), lambda i: (0, 0)),
            pl.BlockSpec((kd, HEAD_DIM), lambda i: (0, 0)),
            pl.BlockSpec((1, HEAD_DIM), lambda i: (0, 0)),
        ],
        out_specs=[pl.BlockSpec((1, N_KV, _GP, HEAD_DIM), lambda i: (i, 0, 0, 0)),
                   pl.BlockSpec((1, N_KV, 128), lambda i: (i, 0, 0))],
        compiler_params=_cparams(("parallel",)),
        name="dec_cmp",
    )(q, ckt, cvh, xk, xv, wk, wv, g_kc.reshape(1, HEAD_DIM))


def _dec_sel_kernel(t_pos, pt_ref, idx_ref, q_ref, gt_ref, oc_ref, ksn_ref, vsn_ref, kwn_ref, vwn_ref,
                    wk_ref, wv_ref, *rest):
    del pt_ref
    kb = rest[:N_KV]
    vb = rest[N_KV:2 * N_KV]
    o_ref, m_ref, l_ref, acc_ref = rest[2 * N_KV:]
    b = pl.program_id(0)
    s_id = pl.program_id(1)
    n_pick = pl.num_programs(1)
    n_buf = wk_ref.shape[1]

    @pl.when(s_id == 0)
    def _():
        m_ref[...] = jnp.full(m_ref.shape, -jnp.inf, f32)
        l_ref[...] = jnp.zeros(l_ref.shape, f32)
        acc_ref[...] = jnp.zeros(acc_ref.shape, f32)

    def slopes(kv):
        g1 = lax.broadcasted_iota(jnp.int32, (_GP, 1), 0) + (kv * GROUP + 1)
        return jnp.exp2(-0.5 * g1.astype(f32))

    j = lax.broadcasted_iota(jnp.int32, (1, CMP_BLOCK), 1)
    for kv in range(N_KV):
        sl = slice(kv * HEAD_DIM, (kv + 1) * HEAD_DIM)
        q = q_ref[0, kv].astype(bf16)
        pos = idx_ref[b, kv * 128 + s_id] * CMP_BLOCK + j
        dist = t_pos - pos
        s = lax.dot_general(q, kb[kv][0][:, sl].astype(bf16), _NT, preferred_element_type=f32)
        s = s - slopes(kv) * dist.astype(f32)
        s = jnp.where(dist >= 0, s, -jnp.inf)
        m_old = m_ref[kv]
        m_new = jnp.maximum(m_old, jnp.max(s, axis=-1, keepdims=True))
        p = jnp.exp(s - m_new)
        alpha = jnp.exp(m_old - m_new)
        l_ref[kv] = alpha * l_ref[kv] + jnp.sum(p, axis=-1, keepdims=True)
        acc_ref[kv] = alpha * acc_ref[kv] + jnp.dot(p.astype(bf16), vb[kv][0][:, sl].astype(bf16),
                                                    preferred_element_type=f32)
        m_ref[kv] = m_new

    @pl.when(s_id == n_pick - 1)
    def _():
        c = lax.broadcasted_iota(jnp.int32, (1, n_buf), 1)
        kpos = t_pos - n_buf + c
        distw = t_pos - kpos
        okw = (kpos >= 0) & (distw >= 0) & (distw < WINDOW)
        for kv in range(N_KV):
            sl = slice(kv * HEAD_DIM, (kv + 1) * HEAD_DIM)
            q = q_ref[0, kv].astype(bf16)
            qf = q.astype(f32)
            slope = slopes(kv)
            s_new = jnp.sum(qf * ksn_ref[0, kv:kv + 1].astype(bf16).astype(f32), axis=-1, keepdims=True)
            m_old = m_ref[kv]
            m_new = jnp.maximum(m_old, s_new)
            alpha = jnp.exp(m_old - m_new)
            p_new = jnp.exp(s_new - m_new)
            l = alpha * l_ref[kv] + p_new
            acc = alpha * acc_ref[kv] + p_new.astype(bf16).astype(f32) * vsn_ref[0, kv:kv + 1].astype(bf16).astype(f32)
            o_s = acc / jnp.maximum(l, 1e-30)
            sw = lax.dot_general(q, wk_ref[0][:, sl].astype(bf16), _NT, preferred_element_type=f32)
            sw = jnp.where(okw, sw - slope * distw.astype(f32), -jnp.inf)
            sw_new = jnp.sum(qf * kwn_ref[0, kv:kv + 1].astype(bf16).astype(f32), axis=-1, keepdims=True)
            mw = jnp.maximum(jnp.max(sw, axis=-1, keepdims=True), sw_new)
            pw = jnp.exp(sw - mw)
            pw_new = jnp.exp(sw_new - mw)
            den = jnp.maximum(jnp.sum(pw, axis=-1, keepdims=True) + pw_new, 1e-30)
            o_w = jnp.dot((pw / den).astype(bf16), wv_ref[0][:, sl].astype(bf16), preferred_element_type=f32)
            o_w = o_w + (pw_new / den).astype(bf16).astype(f32) * vwn_ref[0, kv:kv + 1].astype(bf16).astype(f32)
            gt = gt_ref[0, kv]
            o_ref[0, kv] = gt[:, 0:1] * oc_ref[0, kv] + gt[:, 1:2] * o_s + gt[:, 2:3] * o_w


def _dec_sel(q, gates, o_c, idx, table, pool_k, pool_v, ks_new, vs_new, kw_new, vw_new, win_k, win_v,
             t_pos, n_pick):
    b = q.shape[0]
    n_buf = win_k.shape[1]
    hspec = pl.BlockSpec((1, N_KV, _GP, HEAD_DIM), lambda i, s, pt, ix: (i, 0, 0, 0))
    nspec = pl.BlockSpec((1, N_KV, HEAD_DIM), lambda i, s, pt, ix: (i, 0, 0))
    wspec = pl.BlockSpec((1, n_buf, KV_DIM), lambda i, s, pt, ix: (i, 0, 0))

    def blk_spec(kv):
        def imap(i, s, pt, ix):
            blk = ix[i, kv * 128 + s]
            return (pt[i, blk // 2] * 2 + blk % 2, 0, 0)
        return pl.BlockSpec((1, CMP_BLOCK, KV_DIM), imap)

    kspecs = [blk_spec(kv) for kv in range(N_KV)]
    return pl.pallas_call(
        functools.partial(_dec_sel_kernel, t_pos),
        out_shape=jax.ShapeDtypeStruct((b, N_KV, _GP, HEAD_DIM), f32),
        grid_spec=pltpu.PrefetchScalarGridSpec(
            num_scalar_prefetch=2,
            grid=(b, n_pick),
            in_specs=[hspec, pl.BlockSpec((1, N_KV, _GP, 3), lambda i, s, pt, ix: (i, 0, 0, 0)), hspec,
                      nspec, nspec, nspec, nspec, wspec, wspec] + kspecs + kspecs,
            out_specs=hspec,
            scratch_shapes=[pltpu.VMEM((N_KV, _GP, 1), f32), pltpu.VMEM((N_KV, _GP, 1), f32),
                            pltpu.VMEM((N_KV, _GP, HEAD_DIM), f32)]),
        compiler_params=_cparams(("arbitrary", "arbitrary")),
        name="dec_sel",
    )(table, idx, q, gates, o_c, ks_new, vs_new, kw_new, vw_new, win_k, win_v,
      *([pool_k] * N_KV), *([pool_v] * N_KV))


def _head_rms(x, g):
    ms = jnp.mean(x * x, axis=-1, keepdims=True)
    return x * lax.rsqrt(ms + NORM_EPS) * g


def _prepare(p):
    q = dict(p)
    for n in ("ffn1_gate", "ffn1_up", "ffn1_down", "ffn2_gate", "ffn2_up", "ffn2_down", "w_pa", "w_pb", "w_out"):
        q[n] = p[n].astype(bf16)
    q["w_in_p"] = _pad_cols(p["w_in"]).astype(bf16)
    q["mu_p"] = _pad_rw_cols(p["rw_mu"][None])
    q["w2p"] = _lora_pad(p["rw_w2"])
    q["a2p"] = _lora_pad(p["rw_a2"])
    q["g2"] = p["rw_g2"].astype(bf16)
    return q


def _nsa_proj(z, p):
    m = z.shape[0]
    kvs = (m, N_KV, HEAD_DIM)
    q = _head_rms(_seg(z, "q").reshape(m, N_HEADS, HEAD_DIM), p["g_q"])
    kc = _seg(z, "kc").reshape(kvs)
    vc = _seg(z, "vc").reshape(kvs)
    ks = _head_rms(_seg(z, "ks").reshape(kvs), p["g_ks"])
    vs = _seg(z, "vs").reshape(kvs)
    kw = _head_rms(_seg(z, "kw").reshape(kvs), p["g_kw"])
    vw = _seg(z, "vw").reshape(kvs)
    gates = jax.nn.sigmoid(_seg(z, "ga").reshape(m, N_HEADS, 3))
    return q, gates, kc, vc, ks, vs, kw, vw


def _layer_prompt(x, p):
    t = x.shape[0]
    x = _ffn(x, p["n_ffn1"], p["ffn1_gate"], p["ffn1_up"], p["ffn1_down"])
    z = _inproj(x, p["n_mix"], p["w_in_p"])
    q, gates, kc, vc, ks, vs, kw, vw = _nsa_proj(z, p)
    ck = _head_rms(_compress(kc.reshape(1, t, KV_DIM), p["pe_cmp_k"], p["w_cmp_k"])[0]
                   .reshape(-1, N_KV, HEAD_DIM), p["g_kc"])
    cv = _compress(vc.reshape(1, t, KV_DIM), p["pe_cmp_v"], p["w_cmp_v"])[0].reshape(-1, N_KV, HEAD_DIM)
    o_a = _nsa_prompt(q, gates, ck, cv, ks, vs, kw, vw).astype(bf16)
    shift0 = jnp.zeros((8, _Z_COLS), f32)
    s0t = jnp.zeros((RW_HEADS, RW_N, RW_N), f32)
    o_b, st = _rwkv_prompt(z, shift0, s0t, p)
    o_b = o_b.transpose(1, 0, 2).reshape(t, RW_DIM)
    x = _merge(o_a, o_b, z, x, p["w_pa"], p["w_pb"], p["w_out"])
    x = _ffn(x, p["n_ffn2"], p["ffn2_gate"], p["ffn2_up"], p["ffn2_down"])
    n_buf = min(WINDOW, t)
    states = (kc, vc, ks, vs, kw[-n_buf:], vw[-n_buf:], _rw_cols(z[-1:]), st.transpose(0, 2, 1)[None])
    return x, states


def _layer_sample(x, p, past):
    b = x.shape[0]
    table = past["page_table"]
    page = past["cmp_k"].shape[1]
    t_pos = table.shape[1] * page
    x = _ffn(x, p["n_ffn1"], p["ffn1_gate"], p["ffn1_up"], p["ffn1_down"])
    z = _inproj(x, p["n_mix"], p["w_in_p"])
    q, gates, kc, vc, ks, vs, kw, vw = _nsa_proj(z, p)

    kc_all, vc_all = _gather_pages(past["cmp_k"].reshape(-1, page, KV_DIM),
                                   past["cmp_v"].reshape(-1, page, KV_DIM), table)
    nbp = t_pos // CMP_BLOCK
    ck = _head_rms(_compress(kc_all, p["pe_cmp_k"], p["w_cmp_k"]).reshape(b, nbp, N_KV, HEAD_DIM), p["g_kc"])
    cv = _compress(vc_all, p["pe_cmp_v"], p["w_cmp_v"]).reshape(b, nbp, N_KV, HEAD_DIM)

    def new_block_rows(k_new, pe):
        first = k_new + pe[0]
        rest = jnp.broadcast_to(pe[1:].reshape(1, 1, -1), (b, N_KV, (CMP_BLOCK - 1) * HEAD_DIM))
        rows = jnp.concatenate([first, rest], axis=-1)
        return jnp.pad(rows, ((0, 0), (0, _GP - N_KV), (0, 0)))

    n_sel = min(N_SEL, nbp + 1)
    n_pick = n_sel - 1
    scale = HEAD_DIM ** -0.5
    pad_g = lambda a: jnp.pad(a, ((0, 0), (0, 0), (0, _GP - GROUP), (0, 0)))
    qh = pad_g((q * scale).reshape(b, N_KV, GROUP, HEAD_DIM))
    gth = pad_g(gates.reshape(b, N_KV, GROUP, 3))
    kd = CMP_BLOCK * HEAD_DIM
    o_c, idx = _dec_cmp(qh, ck, cv, new_block_rows(kc, p["pe_cmp_k"]), new_block_rows(vc, p["pe_cmp_v"]),
                        p["w_cmp_k"].reshape(kd, HEAD_DIM).astype(bf16),
                        p["w_cmp_v"].reshape(kd, HEAD_DIM).astype(bf16), p["g_kc"], t_pos, n_pick)
    o_a = _dec_sel(qh, gth, o_c, idx.reshape(b, N_KV * 128), table,
                   past["slc_k"].reshape(-1, CMP_BLOCK, KV_DIM), past["slc_v"].reshape(-1, CMP_BLOCK, KV_DIM),
                   ks, vs, kw, vw, past["win_k"].reshape(b, -1, KV_DIM), past["win_v"].reshape(b, -1, KV_DIM),
                   t_pos, n_pick)
    o_a = o_a[:, :, :GROUP].reshape(b, NSA_DIM).astype(bf16)

    prev = _pad_rw_cols(past["shift"])
    r, k, v, kk, ka, ld, g, bonus = _rw_prep(
        z, prev, p["mu_p"], p["rw_w0"], p["w2p"], p["rw_a0"], p["a2p"], p["g2"],
        p["rw_k_k"], p["rw_k_a"], p["rw_r_k"], False)
    tb = lambda a: a.transpose(1, 0, 2)
    y, wkv = _wkv_step(past["wkv"], tb(r), tb(k), tb(v), tb(kk), tb(ka), tb(ld))
    o_b = _rw_post(tb(y), bonus, g, p["rw_ln_w"], p["rw_ln_b"])
    o_b = o_b.transpose(1, 0, 2).reshape(b, RW_DIM)

    x = _merge(o_a, o_b, z, x, p["w_pa"], p["w_pb"], p["w_out"])
    x = _ffn(x, p["n_ffn2"], p["ffn2_gate"], p["ffn2_up"], p["ffn2_down"])
    kvs = lambda a: a.reshape(b, 1, N_KV, HEAD_DIM)
    win_k = jnp.concatenate([past["win_k"][:, 1:], kvs(kw)], axis=1)
    win_v = jnp.concatenate([past["win_v"][:, 1:], kvs(vw)], axis=1)
    states = (kvs(kc), kvs(vc), kvs(ks), kvs(vs), win_k, win_v, _rw_cols(z), wkv)
    return x, states


def kernel(x_prompt, x_sample, cache_cmp_k, cache_cmp_v, cache_slc_k, cache_slc_v, cache_win_k, cache_win_v,
           state_shift, state_wkv, page_table,
           n_ffn1, ffn1_gate, ffn1_up, ffn1_down, n_mix, w_in, g_q, g_kc, g_ks, g_kw,
           w_cmp_k, pe_cmp_k, w_cmp_v, pe_cmp_v,
           rw_mu, rw_w0, rw_w2, rw_a0, rw_a2, rw_g2, rw_k_k, rw_k_a, rw_r_k, rw_ln_w, rw_ln_b,
           w_pa, w_pb, w_out, n_ffn2, ffn2_gate, ffn2_up, ffn2_down):
    assert x_prompt.shape[0] == 1 and x_sample.shape[1] == 1 and n_ffn1.shape[0] == 1
    l = 0
    p = _prepare(dict(
        n_ffn1=n_ffn1[l], ffn1_gate=ffn1_gate[l], ffn1_up=ffn1_up[l], ffn1_down=ffn1_down[l],
        n_mix=n_mix[l], w_in=w_in[l], g_q=g_q[l], g_kc=g_kc[l], g_ks=g_ks[l], g_kw=g_kw[l],
        w_cmp_k=w_cmp_k[l], pe_cmp_k=pe_cmp_k[l], w_cmp_v=w_cmp_v[l], pe_cmp_v=pe_cmp_v[l],
        rw_mu=rw_mu[l], rw_w0=rw_w0[l], rw_w2=rw_w2[l], rw_a0=rw_a0[l], rw_a2=rw_a2[l], rw_g2=rw_g2[l],
        rw_k_k=rw_k_k[l], rw_k_a=rw_k_a[l], rw_r_k=rw_r_k[l], rw_ln_w=rw_ln_w[l], rw_ln_b=rw_ln_b[l],
        w_pa=w_pa[l], w_pb=w_pb[l], w_out=w_out[l],
        n_ffn2=n_ffn2[l], ffn2_gate=ffn2_gate[l], ffn2_up=ffn2_up[l], ffn2_down=ffn2_down[l]))
    t = x_prompt.shape[1]
    y_p, sp = _layer_prompt(x_prompt[0], p)
    past = dict(page_table=page_table, cmp_k=cache_cmp_k[l], cmp_v=cache_cmp_v[l], slc_k=cache_slc_k[l],
                slc_v=cache_slc_v[l], win_k=cache_win_k[l], win_v=cache_win_v[l], shift=state_shift[l],
                wkv=state_wkv[l])
    y_s, ss = _layer_sample(x_sample[:, 0], p, past)
    kvp = lambda a: a.reshape(1, 1, -1, N_KV, HEAD_DIM)
    outs_p = (kvp(sp[0]), kvp(sp[1]), kvp(sp[2]), kvp(sp[3]), kvp(sp[4]), kvp(sp[5]), sp[6][None], sp[7][None])
    outs_s = tuple(a[None] for a in ss)
    return (y_p.reshape(1, t, D_MODEL), y_s[:, None, :]) + outs_p + outs_s
```

```python
import functools

import jax
import jax.numpy as jnp
from jax import lax
from jax.experimental import pallas as pl
from jax.experimental.pallas import tpu as pltpu

f32 = jnp.float32
bf16 = jnp.bfloat16

D_MODEL = 2048
N_HEADS = 16
N_KV = 4
GROUP = 4
HEAD_DIM = 64
NSA_DIM = N_HEADS * HEAD_DIM
KV_DIM = N_KV * HEAD_DIM
CMP_BLOCK = 64
N_SEL = 16
WINDOW = 512
Q_BLOCK = 128
FORCED_SCORE = 1e3
RW_HEADS = 16
RW_N = 64
RW_DIM = RW_HEADS * RW_N
W_LORA = 96
A_LORA = 96
G_LORA = 256
D_FF = 5632
NORM_EPS = 1e-6
GN_EPS = 64e-5
NEG_BIG = -1e30

VMEM_LIMIT = 56 * 1024 * 1024

_SEGS = (
    ("r", 2608, 1024, 1024),
    ("k", 3728, 1024, 1024),
    ("v", 4752, 1024, 1024),
    ("q", 0, 1024, 1024),
    ("g_a", 6128, 2048, 2048),
    ("g_b", 8176, 2048, 2048),
    ("kc", 1024, 256, 256),
    ("vc", 1280, 256, 256),
    ("ks", 1536, 256, 256),
    ("vs", 1792, 256, 256),
    ("kw", 2048, 256, 256),
    ("vw", 2304, 256, 256),
    ("gl", 5872, 256, 256),
    ("ga", 2560, 48, 128),
    ("wl", 3632, 96, 128),
    ("al", 5776, 96, 128),
)
_Z_COLS = 10752


def _seg_offsets():
    offs, o = {}, 0
    for name, _, w, pw in _SEGS:
        assert o % pw == 0
        offs[name] = (o, w, pw)
        o += pw
    return offs, o


_OFF, _USED = _seg_offsets()


_IN_COLS = 10224
_RW_START = 2608
_RW_COLS = 3520


def _pad_cols(x):
    parts = []
    for _, s, w, pw in _SEGS:
        seg = x[..., s:s + w]
        if pw != w:
            seg = jnp.pad(seg, [(0, 0)] * (x.ndim - 1) + [(0, pw - w)])
        parts.append(seg)
    parts.append(jnp.zeros(x.shape[:-1] + (_Z_COLS - _USED,), x.dtype))
    return jnp.concatenate(parts, axis=-1)


def _pad_rw_cols(x):
    pad = [(0, 0)] * (x.ndim - 1) + [(_RW_START, _IN_COLS - _RW_START - _RW_COLS)]
    return _pad_cols(jnp.pad(x, pad))


def _seg(z, name):
    o, w, _ = _OFF[name]
    return z[..., o:o + w]


def _rw_cols(z):
    return jnp.concatenate([_seg(z, n) for n in ("r", "wl", "k", "v", "al", "gl")], axis=-1)


def _cparams(sem, vmem=VMEM_LIMIT):
    return pltpu.CompilerParams(dimension_semantics=sem, vmem_limit_bytes=vmem)


def _row_tile(m, pref):
    return pref if m % pref == 0 else m


def _rms(x, g):
    ms = jnp.mean(x * x, axis=-1, keepdims=True)
    return x * lax.rsqrt(ms + NORM_EPS) * g


def _ffn_kernel(x_ref, g_ref, wg_ref, wu_ref, wd_ref, o_ref, h_ref, acc_ref):
    j = pl.program_id(1)

    @pl.when(j == 0)
    def _():
        h_ref[...] = _rms(x_ref[...], g_ref[...]).astype(bf16)
        acc_ref[...] = jnp.zeros_like(acc_ref)

    h = h_ref[...]
    g = jnp.dot(h, wg_ref[...], preferred_element_type=f32)
    u = jnp.dot(h, wu_ref[...], preferred_element_type=f32)
    a = (g * jax.nn.sigmoid(g) * u).astype(bf16)
    acc_ref[...] += jnp.dot(a, wd_ref[...], preferred_element_type=f32)

    @pl.when(j == pl.num_programs(1) - 1)
    def _():
        o_ref[...] = x_ref[...] + 0.5 * acc_ref[...]


def _ffn(x, gain, wg, wu, wd):
    m, d = x.shape
    ff = wg.shape[1]
    bm = _row_tile(m, 512)
    bf = 512
    return pl.pallas_call(
        _ffn_kernel,
        out_shape=jax.ShapeDtypeStruct((m, d), f32),
        grid=(m // bm, ff // bf),
        in_specs=[
            pl.BlockSpec((bm, d), lambda i, j: (i, 0)),
            pl.BlockSpec((1, d), lambda i, j: (0, 0)),
            pl.BlockSpec((d, bf), lambda i, j: (0, j)),
            pl.BlockSpec((d, bf), lambda i, j: (0, j)),
            pl.BlockSpec((bf, d), lambda i, j: (j, 0)),
        ],
        out_specs=pl.BlockSpec((bm, d), lambda i, j: (i, 0)),
        scratch_shapes=[pltpu.VMEM((bm, d), bf16), pltpu.VMEM((bm, d), f32)],
        compiler_params=_cparams(("parallel", "arbitrary")),
        name="ffn",
    )(x, gain.reshape(1, d), wg, wu, wd)


def _inproj_kernel(x_ref, g_ref, w_ref, o_ref, h_ref):
    @pl.when(pl.program_id(1) == 0)
    def _():
        h_ref[...] = _rms(x_ref[...], g_ref[...]).astype(bf16)

    o_ref[...] = jnp.dot(h_ref[...], w_ref[...], preferred_element_type=f32)


def _inproj(x, gain, w):
    m, d = x.shape
    n = w.shape[1]
    bm = _row_tile(m, 512)
    bn = 512
    return pl.pallas_call(
        _inproj_kernel,
        out_shape=jax.ShapeDtypeStruct((m, n), f32),
        grid=(m // bm, n // bn),
        in_specs=[
            pl.BlockSpec((bm, d), lambda i, j: (i, 0)),
            pl.BlockSpec((1, d), lambda i, j: (0, 0)),
            pl.BlockSpec((d, bn), lambda i, j: (0, j)),
        ],
        out_specs=pl.BlockSpec((bm, bn), lambda i, j: (i, j)),
        scratch_shapes=[pltpu.VMEM((bm, d), bf16)],
        compiler_params=_cparams(("parallel", "arbitrary")),
        name="inproj",
    )(x, gain.reshape(1, d), w)


def _merge_kernel(oa_ref, ob_ref, ga_ref, gb_ref, x_ref, wpa_ref, wpb_ref, wo_ref, o_ref):
    pa = jnp.dot(oa_ref[...], wpa_ref[...], preferred_element_type=f32)
    pb = jnp.dot(ob_ref[...], wpb_ref[...], preferred_element_type=f32)
    mix = jax.nn.sigmoid(ga_ref[...]) * pa + jax.nn.sigmoid(gb_ref[...]) * pb
    o_ref[...] = x_ref[...] + jnp.dot(mix.astype(bf16), wo_ref[...], preferred_element_type=f32)


def _merge(oa, ob, z, x, wpa, wpb, wo):
    m, d = x.shape
    bm = _row_tile(m, 256)
    ca = _OFF["g_a"][0] // d
    cb = _OFF["g_b"][0] // d
    return pl.pallas_call(
        _merge_kernel,
        out_shape=jax.ShapeDtypeStruct((m, d), f32),
        grid=(m // bm,),
        in_specs=[
            pl.BlockSpec((bm, NSA_DIM), lambda i: (i, 0)),
            pl.BlockSpec((bm, RW_DIM), lambda i: (i, 0)),
            pl.BlockSpec((bm, d), lambda i: (i, ca)),
            pl.BlockSpec((bm, d), lambda i: (i, cb)),
            pl.BlockSpec((bm, d), lambda i: (i, 0)),
            pl.BlockSpec((NSA_DIM, d), lambda i: (0, 0)),
            pl.BlockSpec((RW_DIM, d), lambda i: (0, 0)),
            pl.BlockSpec((d, d), lambda i: (0, 0)),
        ],
        out_specs=pl.BlockSpec((bm, d), lambda i: (i, 0)),
        compiler_params=_cparams(("parallel",)),
        name="merge",
    )(oa, ob, z, z, x, wpa, wpb, wo)


def _compress_kernel(x_ref, pe_ref, w_ref, o_ref):
    nb = o_ref.shape[1]
    acc = [jnp.zeros((nb, 128), f32), jnp.zeros((nb, 128), f32)]
    for j in range(CMP_BLOCK):
        pe_j = pe_ref[pl.ds(j, 1), :]
        w_j = w_ref[j]
        for h in range(2):
            xj = x_ref[0, pl.ds(2 * j + h, nb, stride=2 * CMP_BLOCK), :] + pe_j
            acc[h] = acc[h] + jnp.dot(xj.astype(bf16), w_j, preferred_element_type=f32)
    o_ref[0] = jnp.concatenate(acc, axis=-1)


def _compress(x, pe, w):
    b, l, _ = x.shape
    nb = l // CMP_BLOCK
    pe_t = jnp.tile(pe, (1, 2))
    eye = jnp.eye(2, dtype=f32)
    wbd = jnp.einsum("ab,jde->jadbe", eye, w).reshape(CMP_BLOCK, 128, 128).astype(bf16)
    return pl.pallas_call(
        _compress_kernel,
        out_shape=jax.ShapeDtypeStruct((b, nb, KV_DIM), f32),
        grid=(b,),
        in_specs=[
            pl.BlockSpec((1, 2 * l, 128), lambda i: (i, 0, 0)),
            pl.BlockSpec((CMP_BLOCK, 128), lambda i: (0, 0)),
            pl.BlockSpec((CMP_BLOCK, 128, 128), lambda i: (0, 0, 0)),
        ],
        out_specs=pl.BlockSpec((1, nb, KV_DIM), lambda i: (i, 0, 0)),
        compiler_params=_cparams(("parallel",)),
        name="compress",
    )(x.reshape(b, 2 * l, 128), pe_t, wbd)


_TK = 512
_ROWS = GROUP * Q_BLOCK


def _softmax_cols(s):
    m = jnp.max(s, axis=0, keepdims=True)
    m = jnp.where(jnp.isfinite(m), m, 0.0)
    p = jnp.exp(s - m)
    return p / jnp.maximum(jnp.sum(p, axis=0, keepdims=True), 1e-30)


def _topk_mask_t(vt, n_sel):
    nb = vt.shape[0]
    bi = lax.broadcasted_iota(jnp.int32, vt.shape, 0)
    sel = jnp.zeros(vt.shape, f32)
    for _ in range(n_sel):
        mx = jnp.max(vt, axis=0, keepdims=True)
        idx = jnp.min(jnp.where(vt == mx, bi, nb), axis=0, keepdims=True)
        hit = bi == idx
        sel = jnp.where(hit, 1.0, sel)
        vt = jnp.where(hit, -jnp.inf, vt)
    return sel


def _nsa_kernel(qt_ref, gt_ref, ck_ref, cvt_ref, ke_ref, vst_ref, kw_ref, vwt_ref,
                o_ref, a0_ref, a0w_ref):
    kv = pl.program_id(0)
    i = pl.program_id(1)
    nb = ck_ref.shape[1]
    n_sel = min(N_SEL, nb)
    wk = WINDOW + Q_BLOCK

    lane = lax.broadcasted_iota(jnp.int32, (1, _ROWS), 1)
    grp = lane // Q_BLOCK
    tl = lane % Q_BLOCK
    slope = jnp.exp2(-0.5 * (kv * GROUP + grp + 1).astype(f32))
    tlf = tl.astype(f32)

    @pl.when(i == 0)
    def _():
        krow = lax.broadcasted_iota(jnp.int32, (_TK, 1), 0).astype(f32)
        a0_ref[...] = slope * (tlf - krow)
        kroww = lax.broadcasted_iota(jnp.int32, (wk, 1), 0).astype(f32)
        a0w_ref[...] = slope * (tlf + float(WINDOW) - kroww)

    qt = qt_ref[0, 0]
    t0 = i * Q_BLOCK
    tok = t0 + tl

    blk = lax.broadcasted_iota(jnp.int32, (nb, 1), 0)
    blk_mid = (blk * CMP_BLOCK).astype(f32) + (CMP_BLOCK - 1) / 2
    s = jnp.dot(ck_ref[0], qt, preferred_element_type=f32)
    s = s - slope * (tok.astype(f32) - blk_mid)
    s = jnp.where(blk * CMP_BLOCK + (CMP_BLOCK - 1) <= tok, s, -jnp.inf)
    p_c = _softmax_cols(s)
    o_c = jnp.dot(cvt_ref[0], p_c.astype(bf16), preferred_element_type=f32)

    w0 = pl.multiple_of(t0, Q_BLOCK)
    s = jnp.dot(kw_ref[0, pl.ds(w0, wk), :], qt, preferred_element_type=f32) - a0w_ref[...]
    kroww = lax.broadcasted_iota(jnp.int32, (wk, 1), 0)
    dist = tl + WINDOW - kroww
    s = jnp.where((t0 - WINDOW + kroww >= 0) & (dist >= 0) & (dist < WINDOW), s, -jnp.inf)
    p_w = _softmax_cols(s)
    o_w = jnp.dot(vwt_ref[0, :, pl.ds(w0, wk)], p_w.astype(bf16), preferred_element_type=f32)

    imp = p_c[:, 0:Q_BLOCK]
    for g in range(1, GROUP):
        imp = imp + p_c[:, g * Q_BLOCK:(g + 1) * Q_BLOCK]
    tq = t0 + lax.broadcasted_iota(jnp.int32, (1, Q_BLOCK), 1)
    cur = tq // CMP_BLOCK
    forced = (blk == 0) | (blk == cur) | (blk == cur - 1)
    imp = jnp.where(forced, FORCED_SCORE, jnp.where(blk * CMP_BLOCK <= tq, imp, -1.0))
    sel_t = _topk_mask_t(imp, n_sel)
    bias_t = jnp.where(sel_t > 0.0, 0.0, NEG_BIG).astype(bf16)
    rhs = jnp.concatenate([qt] + [jnp.concatenate([bias_t] * GROUP, axis=1)], axis=0)

    def scores(j):
        c0 = pl.multiple_of(j * _TK, _TK)
        return jnp.dot(ke_ref[0, pl.ds(c0, _TK), :], rhs, preferred_element_type=f32) - a0_ref[...]

    def update(j, s, carry, causal):
        m, l, acc = carry
        c0 = pl.multiple_of(j * _TK, _TK)
        if causal:
            krow = lax.broadcasted_iota(jnp.int32, (_TK, 1), 0)
            s = jnp.where(tok - (c0 + krow) >= 0, s, -jnp.inf)
        off = slope * (t0 - c0).astype(f32)
        m_new = jnp.maximum(m, jnp.max(s, axis=0, keepdims=True) - off)
        p = jnp.exp(s - (m_new + off))
        alpha = jnp.exp(m - m_new)
        l = alpha * l + jnp.sum(p, axis=0, keepdims=True)
        acc = alpha * acc + jnp.dot(vst_ref[0, :, pl.ds(c0, _TK)], p.astype(bf16),
                                    preferred_element_type=f32)
        return m_new, l, acc

    jd = t0 // _TK
    init = (jnp.full((1, _ROWS), -jnp.inf, f32), jnp.zeros((1, _ROWS), f32),
            jnp.zeros((HEAD_DIM, _ROWS), f32))
    def body(j, carry):
        s_next = scores(j + 1)
        return update(j, carry[3], carry[:3], False) + (s_next,)

    carry = lax.fori_loop(0, jd, body, init + (scores(0),))
    _, l, acc = update(jd, carry[3], carry[:3], True)
    o_s = acc / jnp.maximum(l, 1e-30)

    gt = gt_ref[0, 0]
    o_ref[0, 0] = gt[0:1] * o_c + gt[1:2] * o_s + gt[2:3] * o_w


def _nsa_prompt(q, gates, ck, cv, ks, vs, kw, vw):
    t = q.shape[0]
    nqb = t // Q_BLOCK
    nb = ck.shape[0]
    scale = HEAD_DIM ** -0.5
    qt = (q * scale).astype(bf16).reshape(nqb, Q_BLOCK, N_KV, GROUP, HEAD_DIM)
    qt = qt.transpose(2, 0, 4, 3, 1).reshape(N_KV, nqb, HEAD_DIM, _ROWS)
    gt = gates.reshape(nqb, Q_BLOCK, N_KV, GROUP, 3).transpose(2, 0, 4, 3, 1).reshape(N_KV, nqb, 3, _ROWS)
    ckh = ck.astype(bf16).transpose(1, 0, 2)
    cvt = cv.astype(bf16).transpose(1, 2, 0)
    onehot = (jnp.arange(t)[:, None] // CMP_BLOCK == jnp.arange(nb)[None, :]).astype(bf16)
    ke = jnp.concatenate([ks.astype(bf16).transpose(1, 0, 2),
                          jnp.broadcast_to(onehot[None], (N_KV, t, nb))], axis=-1)
    vst = vs.astype(bf16).transpose(1, 2, 0)
    kwh = jnp.pad(kw.astype(bf16).transpose(1, 0, 2), ((0, 0), (WINDOW, 0), (0, 0)))
    vwt = jnp.pad(vw.astype(bf16).transpose(1, 2, 0), ((0, 0), (0, 0), (WINDOW, 0)))
    kd = HEAD_DIM + nb
    out = pl.pallas_call(
        _nsa_kernel,
        out_shape=jax.ShapeDtypeStruct((N_KV, nqb, HEAD_DIM, _ROWS), f32),
        grid=(N_KV, nqb),
        in_specs=[
            pl.BlockSpec((1, 1, HEAD_DIM, _ROWS), lambda k, i: (k, i, 0, 0)),
            pl.BlockSpec((1, 1, 3, _ROWS), lambda k, i: (k, i, 0, 0)),
            pl.BlockSpec((1, nb, HEAD_DIM), lambda k, i: (k, 0, 0)),
            pl.BlockSpec((1, HEAD_DIM, nb), lambda k, i: (k, 0, 0)),
            pl.BlockSpec((1, t, kd), lambda k, i: (k, 0, 0)),
            pl.BlockSpec((1, HEAD_DIM, t), lambda k, i: (k, 0, 0)),
            pl.BlockSpec((1, t + WINDOW, HEAD_DIM), lambda k, i: (k, 0, 0)),
            pl.BlockSpec((1, HEAD_DIM, t + WINDOW), lambda k, i: (k, 0, 0)),
        ],
        out_specs=pl.BlockSpec((1, 1, HEAD_DIM, _ROWS), lambda k, i: (k, i, 0, 0)),
        scratch_shapes=[pltpu.VMEM((_TK, _ROWS), f32), pltpu.VMEM((WINDOW + Q_BLOCK, _ROWS), f32)],
        compiler_params=_cparams(("arbitrary", "arbitrary")),
        name="nsa_prompt",
    )(qt, gt, ckh, cvt, ke, vst, kwh, vwt)
    out = out.reshape(N_KV, nqb, HEAD_DIM, GROUP, Q_BLOCK).transpose(1, 4, 0, 3, 2)
    return out.reshape(t, NSA_DIM)


def _pair_sum(x, low):
    s_lo = jnp.sum(jnp.where(low, x, 0.0), axis=-1, keepdims=True)
    s_hi = jnp.sum(jnp.where(low, 0.0, x), axis=-1, keepdims=True)
    return jnp.where(low, s_lo, s_hi)


def _rw_prep_kernel(shift_rows, *refs):
    cur, refs = refs[:6], refs[6:]
    prv, refs = refs[:6], refs[6:]
    if shift_rows:
        st0, refs = refs[:6], refs[6:]
    mus, refs = refs[:6], refs[6:]
    (w0_ref, w2_ref, a0_ref, a2_ref, g2_ref, kk_ref, ka_ref, rk_ref,
     ro_ref, ko_ref, vo_ref, kko_ref, kao_ref, ldo_ref, go_ref, bo_ref) = refs
    first = pl.program_id(0) == 0

    def shifted(n):
        x = cur[n][...]
        if shift_rows:
            prev_row = jnp.where(first, st0[n][pl.ds(7, 1), :], prv[n][pl.ds(7, 1), :])
            rolled = pltpu.roll(x, 1, axis=0)
            rid = lax.broadcasted_iota(jnp.int32, x.shape, 0)
            xp = jnp.where(rid == 0, prev_row, rolled)
        else:
            xp = prv[n][...]
        return x + (xp - x) * mus[n][...]

    r, k, v, gl, wl, al = (shifted(n) for n in range(6))

    y = -(w0_ref[...] + jnp.dot(jnp.tanh(wl).astype(bf16), w2_ref[...], preferred_element_type=f32))
    softplus = jnp.maximum(y, 0.0) + jnp.log1p(jnp.exp(-jnp.abs(y)))
    w_log = -softplus - 0.5
    ld = -jnp.exp(w_log)
    a = jax.nn.sigmoid(a0_ref[...] + jnp.dot(al.astype(bf16), a2_ref[...], preferred_element_type=f32))
    g = jnp.dot(jax.nn.sigmoid(gl).astype(bf16), g2_ref[...], preferred_element_type=f32)
    kk = k * kk_ref[...]
    k2 = k * (1.0 + (a - 1.0) * ka_ref[...])
    rkr = r * k2 * rk_ref[...]
    low = lax.broadcasted_iota(jnp.int32, (1, 2 * RW_N), 1) < RW_N
    for hp in range(RW_HEADS // 2):
        sl = slice(hp * 2 * RW_N, (hp + 1) * 2 * RW_N)
        kkp = kk[:, sl]
        kkp = kkp / jnp.maximum(jnp.sqrt(_pair_sum(kkp * kkp, low)), 1e-12)
        ro_ref[hp] = r[:, sl]
        ko_ref[hp] = k2[:, sl]
        vo_ref[hp] = v[:, sl]
        kko_ref[hp] = kkp
        kao_ref[hp] = kkp * a[:, sl]
        ldo_ref[hp] = ld[:, sl]
        go_ref[hp] = g[:, sl]
        bo_ref[hp] = _pair_sum(rkr[:, sl], low) * v[:, sl]


def _rw_prep(z, prev, mu_p, w0, w2p, a0, a2p, g2, k_k, k_a, r_k, shift_rows):
    m = z.shape[0]
    tm = _row_tile(m, 256)
    names = ("r", "k", "v", "gl", "wl", "al")

    def col_spec(name, rows, imap):
        o, _, pw = _OFF[name]
        return pl.BlockSpec((rows, pw), functools.partial(imap, o // pw))

    cur = [col_spec(n, tm, lambda c, i: (i, c)) for n in names]
    if shift_rows:
        blk8 = tm // 8
        prv = [col_spec(n, 8, lambda c, i: (jnp.maximum(i * blk8 - 1, 0), c)) for n in names]
        prv += [col_spec(n, 8, lambda c, i: (0, c)) for n in names]
        prev_args = [z] * 6 + [prev] * 6
    else:
        prv = [col_spec(n, tm, lambda c, i: (i, c)) for n in names]
        prev_args = [prev] * 6
    mus = [col_spec(n, 1, lambda c, i: (0, c)) for n in names]
    vec = pl.BlockSpec((1, RW_DIM), lambda i: (0, 0))
    out_spec = pl.BlockSpec((RW_HEADS // 2, tm, 2 * RW_N), lambda i: (0, i, 0))
    outs = pl.pallas_call(
        functools.partial(_rw_prep_kernel, shift_rows),
        out_shape=[jax.ShapeDtypeStruct((RW_HEADS // 2, m, 2 * RW_N), f32)] * 8,
        grid=(m // tm,),
        in_specs=cur + prv + mus + [
            vec,
            pl.BlockSpec((128, RW_DIM), lambda i: (0, 0)),
            vec,
            pl.BlockSpec((128, RW_DIM), lambda i: (0, 0)),
            pl.BlockSpec((G_LORA, RW_DIM), lambda i: (0, 0)),
            vec, vec, vec,
        ],
        out_specs=[out_spec] * 8,
        compiler_params=_cparams(("parallel",)),
        name="rw_prep",
    )(*([z] * 6), *prev_args, *([mu_p] * 6),
      w0.reshape(1, RW_DIM), w2p, a0.reshape(1, RW_DIM), a2p, g2,
      k_k.reshape(1, RW_DIM), k_a.reshape(1, RW_DIM), r_k.reshape(1, RW_DIM))
    return outs


_CH = 64
_HEADS_PER_STEP = 8
_NN = (((1,), (0,)), ((), ()))
_NT = (((1,), (1,)), ((), ()))
_TN = (((0,), (0,)), ((), ()))


def _split2(x):
    hi = x.astype(bf16)
    lo = (x - hi.astype(f32)).astype(bf16)
    return hi, lo


def _dot3(a, b, dims=_NN):
    ah, al = _split2(a)
    bh, bl = _split2(b)
    d = functools.partial(lax.dot_general, dimension_numbers=dims, preferred_element_type=f32)
    return d(ah, bh) + (d(ah, bl) + d(al, bh))


def _dot_exact_rhs(a, b, dims=_NN, split_lhs=False):
    x = a if split_lhs else b
    x1 = x.astype(bf16)
    r1 = x - x1.astype(f32)
    x2 = r1.astype(bf16)
    x3 = (r1 - x2.astype(f32)).astype(bf16)
    d = functools.partial(lax.dot_general, dimension_numbers=dims, preferred_element_type=f32)
    if split_lhs:
        return d(x1, b) + (d(x2, b) + d(x3, b))
    return d(a, x1) + (d(a, x2) + d(a, x3))


def _wkv_chunk_kernel(r_ref, k_ref, v_ref, kk_ref, ka_ref, ld_ref, s0_ref, y_ref, sT_ref, st_ref):
    c = pl.program_id(0)

    @pl.when(c == 0)
    def _():
        st_ref[...] = s0_ref[...]

    ti = lax.broadcasted_iota(jnp.int32, (_CH, _CH), 0)
    si = lax.broadcasted_iota(jnp.int32, (_CH, _CH), 1)
    incl = ti >= si
    strict = ti > si
    l_incl = incl.astype(bf16)
    eye = (ti == si).astype(f32)
    ones = jnp.ones((_CH, RW_N), bf16)

    def each(f, *lists):
        return [f(*a) for a in zip(*lists)]

    def cat0(a, b):
        return jnp.concatenate([a, b], axis=0)

    def cat1(a, b):
        return jnp.concatenate([a, b], axis=1)

    def heads_step(heads):
        def load(ref):
            return [ref[hp, :, sub * RW_N:(sub + 1) * RW_N] for hp, sub in heads]

        r, k, v, kk, ka, ld = (load(ref) for ref in (r_ref, k_ref, v_ref, kk_ref, ka_ref, ld_ref))
        lp = each(lambda x: _dot_exact_rhs(l_incl, x), ld)
        lp_end = each(lambda x: _dot_exact_rhs(x, ones, _TN, split_lhs=True), ld)
        e_neg = each(lambda x: jnp.exp(-x), lp)
        at = each(lambda a, x, y: -a * jnp.exp(x - y), kk, lp, ld)
        bt = each(jnp.multiply, ka, e_neg)
        kt = each(jnp.multiply, k, e_neg)
        rt = each(lambda a, x: a * jnp.exp(x), r, lp)
        e_end = each(lambda x: jnp.exp(x[_CH - 1:_CH, :] - x), lp)
        bh = each(jnp.multiply, ka, e_end)
        kh = each(jnp.multiply, k, e_end)
        sc = each(lambda a, b: _dot3(a, b, _NT), each(cat0, at, rt), each(cat0, bt, kt))
        a_b = each(lambda x: jnp.where(strict, x[:_CH, :_CH], 0.0), sc)
        a_k = each(lambda x: jnp.where(strict, x[:_CH, _CH:], 0.0), sc)
        g_b = each(lambda x: jnp.where(incl, x[_CH:, :_CH], 0.0), sc)
        g_k = each(lambda x: jnp.where(incl, x[_CH:, _CH:], 0.0), sc)
        tm = each(lambda x: eye + x, a_b)
        pw = a_b
        for _ in range(5):
            pw = each(_dot3, pw, pw)
            tm = each(lambda t_, p_: t_ + _dot3(t_, p_), tm, pw)
        akv = each(_dot3, a_k, v)
        tx = each(_dot3, tm, each(cat1, at, akv))
        st = [st_ref[2 * hp + sub] for hp, sub in heads]
        ws = each(_dot3, each(lambda x, y: cat0(x[:, :RW_N], y), tx, rt), st)
        u = each(lambda a, x: a[:_CH] + x[:, RW_N:], ws, tx)
        uv = each(cat0, u, v)
        y = each(lambda a, gb, gk, x: a[_CH:] + _dot3(cat1(gb, gk), x), ws, g_b, g_k, uv)
        st_new = each(lambda e, s_, b_, k_, x: jnp.exp(e) * s_ + _dot3(cat0(b_, k_), x, _TN),
                      lp_end, st, bh, kh, uv)
        for (hp, sub), s_ in zip(heads, st_new):
            st_ref[2 * hp + sub] = s_
        return y

    def pairs(i, carry):
        heads = [(_HEADS_PER_STEP // 2 * i + a, sub) for a in range(_HEADS_PER_STEP // 2) for sub in range(2)]
        y = heads_step(heads)
        for a in range(_HEADS_PER_STEP // 2):
            y_ref[_HEADS_PER_STEP // 2 * i + a] = cat1(y[2 * a], y[2 * a + 1])
        return carry

    lax.fori_loop(0, RW_HEADS // _HEADS_PER_STEP, pairs, 0)

    @pl.when(c == pl.num_programs(0) - 1)
    def _():
        sT_ref[...] = st_ref[...]


def _wkv_chunks(r, k, v, kk, ka, ld, s0t):
    t = r.shape[1]
    spec = pl.BlockSpec((RW_HEADS // 2, _CH, 2 * RW_N), lambda c: (0, c, 0))
    sspec = pl.BlockSpec((RW_HEADS, RW_N, RW_N), lambda c: (0, 0, 0))
    return pl.pallas_call(
        _wkv_chunk_kernel,
        out_shape=[jax.ShapeDtypeStruct((RW_HEADS // 2, t, 2 * RW_N), f32),
                   jax.ShapeDtypeStruct((RW_HEADS, RW_N, RW_N), f32)],
        grid=(t // _CH,),
        in_specs=[spec] * 6 + [sspec],
        out_specs=[spec, sspec],
        scratch_shapes=[pltpu.VMEM((RW_HEADS, RW_N, RW_N), f32)],
        compiler_params=_cparams(("arbitrary",)),
        name="wkv_chunks",
    )(r, k, v, kk, ka, ld, s0t)


def _rw_post_kernel(y_ref, b_ref, g_ref, lw_ref, lb_ref, o_ref):
    low = lax.broadcasted_iota(jnp.int32, (1, 2 * RW_N), 1) < RW_N
    for hp in range(RW_HEADS // 2):
        y = y_ref[hp]
        mu = _pair_sum(y, low) / RW_N
        var = _pair_sum(jnp.square(y - mu), low) / RW_N
        yn = (y - mu) * lax.rsqrt(var + GN_EPS) * lw_ref[hp] + lb_ref[hp]
        o_ref[:, hp * 2 * RW_N:(hp + 1) * 2 * RW_N] = ((yn + b_ref[hp]) * g_ref[hp]).astype(o_ref.dtype)


def _rw_post(y, bonus, g, ln_w, ln_b):
    m = y.shape[1]
    tm = _row_tile(m, 512)
    spec = pl.BlockSpec((RW_HEADS // 2, tm, 2 * RW_N), lambda i: (0, i, 0))
    pspec = pl.BlockSpec((RW_HEADS // 2, 1, 2 * RW_N), lambda i: (0, 0, 0))
    return pl.pallas_call(
        _rw_post_kernel,
        out_shape=jax.ShapeDtypeStruct((m, RW_DIM), bf16),
        grid=(m // tm,),
        in_specs=[spec, spec, spec, pspec, pspec],
        out_specs=pl.BlockSpec((tm, RW_DIM), lambda i: (i, 0)),
        compiler_params=_cparams(("parallel",)),
        name="rw_post",
    )(y, bonus, g, ln_w.reshape(RW_HEADS // 2, 1, 2 * RW_N), ln_b.reshape(RW_HEADS // 2, 1, 2 * RW_N))


def _lora_pad(w):
    return jnp.pad(w, ((0, 128 - w.shape[0]), (0, 0))).astype(bf16)


def _rwkv_prompt(z, shift0_p, s0t, p):
    r, k, v, kk, ka, ld, g, bonus = _rw_prep(
        z, shift0_p, p["mu_p"], p["rw_w0"], p["w2p"], p["rw_a0"], p["a2p"], p["g2"],
        p["rw_k_k"], p["rw_k_a"], p["rw_r_k"], True)
    y, st = _wkv_chunks(r, k, v, kk, ka, ld, s0t)
    return _rw_post(y, bonus, g, p["rw_ln_w"], p["rw_ln_b"]), st


def _wkv_step_kernel(s_ref, r_ref, k_ref, vc_ref, kk_ref, ka_ref, ld_ref, y_ref, so_ref):
    s = s_ref[0]
    kk = kk_ref[0][:, None, :]
    sa = jnp.sum(s * (-kk), axis=-1, keepdims=True)
    s = (s * jnp.exp(ld_ref[0])[:, None, :] + sa * ka_ref[0][:, None, :]
         + vc_ref[0] * k_ref[0][:, None, :])
    so_ref[0] = s
    y_ref[0] = jnp.sum(s * r_ref[0][:, None, :], axis=-1, keepdims=True)


def _wkv_step(s0, r, k, v, kk, ka, ld):
    b = s0.shape[0]
    sspec = pl.BlockSpec((1, RW_HEADS, RW_N, RW_N), lambda i: (i, 0, 0, 0))
    vspec = pl.BlockSpec((1, RW_HEADS, RW_N), lambda i: (i, 0, 0))
    cspec = pl.BlockSpec((1, RW_HEADS, RW_N, 1), lambda i: (i, 0, 0, 0))
    y, s1 = pl.pallas_call(
        _wkv_step_kernel,
        out_shape=[jax.ShapeDtypeStruct((b, RW_HEADS, RW_N, 1), f32),
                   jax.ShapeDtypeStruct(s0.shape, f32)],
        grid=(b,),
        in_specs=[sspec, vspec, vspec, cspec, vspec, vspec, vspec],
        out_specs=[cspec, sspec],
        compiler_params=_cparams(("parallel",)),
        name="wkv_step",
    )(s0, r, k, v[..., None], kk, ka, ld)
    return y[..., 0], s1


_GP = 8


_PAGE_ROWS = N_KV * HEAD_DIM


def _compress_pages_kernel(npg, pt_ref, pool_ref, pe_ref, w_ref, o_ref, buf0, buf1, sem):
    g = pl.program_id(0)
    n_steps = pl.num_programs(0)

    def page_copies(seq, buf, slot):
        return [pltpu.make_async_copy(pool_ref.at[pt_ref[seq, pg]],
                                      buf.at[pl.ds(pg * _PAGE_ROWS, _PAGE_ROWS)], sem.at[slot])
                for pg in range(npg)]

    def start(seq, buf, slot):
        for cp in page_copies(seq, buf, slot):
            cp.start()

    def wait(seq, buf, slot):
        for cp in page_copies(seq, buf, slot):
            cp.wait()

    def compress(buf, out_slot):
        acc = jnp.zeros((npg * N_KV, 2 * CMP_BLOCK), f32)
        for d in range(HEAD_DIM):
            x = buf[pl.ds(d, npg * N_KV, stride=HEAD_DIM), :] + pe_ref[pl.ds(d, 1), :]
            acc = acc + jnp.dot(x.astype(bf16), w_ref[d], preferred_element_type=f32)
        o_ref[out_slot] = acc

    @pl.when(g == 0)
    def _():
        start(0, buf0, 0)

    start(2 * g + 1, buf1, 1)
    wait(2 * g, buf0, 0)
    compress(buf0, 0)

    @pl.when(g + 1 < n_steps)
    def _():
        start(2 * g + 2, buf0, 0)

    wait(2 * g + 1, buf1, 1)
    compress(buf1, 1)


def _compress_pages(pool_t, table, pe, w):
    b, npg = table.shape
    page = pool_t.shape[-1]
    assert b % 2 == 0 and page == 2 * CMP_BLOCK
    pe_t = jnp.tile(pe.T, (1, 2))
    eye = jnp.eye(2, dtype=f32)
    wd = jnp.einsum("ab,jde->dajbe", eye, w).reshape(HEAD_DIM, page, page).astype(bf16)
    out = pl.pallas_call(
        functools.partial(_compress_pages_kernel, npg),
        out_shape=jax.ShapeDtypeStruct((b, npg * N_KV, page), f32),
        grid_spec=pltpu.PrefetchScalarGridSpec(
            num_scalar_prefetch=1,
            grid=(b // 2,),
            in_specs=[
                pl.BlockSpec(memory_space=pl.ANY),
                pl.BlockSpec((HEAD_DIM, page), lambda i, pt: (0, 0)),
                pl.BlockSpec((HEAD_DIM, page, page), lambda i, pt: (0, 0, 0)),
            ],
            out_specs=pl.BlockSpec((2, npg * N_KV, page), lambda i, pt: (i, 0, 0)),
            scratch_shapes=[pltpu.VMEM((npg * _PAGE_ROWS, page), f32),
                            pltpu.VMEM((npg * _PAGE_ROWS, page), f32),
                            pltpu.SemaphoreType.DMA((2,))]),
        compiler_params=_cparams(("arbitrary",)),
        name="compress_pages",
    )(table, pool_t, pe_t, wd)
    out = out.reshape(b, npg, N_KV, 2, HEAD_DIM).transpose(0, 1, 3, 2, 4)
    return out.reshape(b, 2 * npg, N_KV, HEAD_DIM)


def _dec_cmp_kernel(n_pick, t_pos, q_ref, ckt_ref, cv_ref, xk_ref, xv_ref, wk_ref, wv_ref, gkc_ref,
                    oc_ref, idx_ref):
    nb = ckt_ref.shape[-1]
    blk = lax.broadcasted_iota(jnp.int32, (1, nb), 1)
    blk_mid = (blk * CMP_BLOCK).astype(f32) + (CMP_BLOCK - 1) / 2
    ck_new = _rms(jnp.dot(xk_ref[0].astype(bf16), wk_ref[...], preferred_element_type=f32), gkc_ref[...])
    cv_new = jnp.dot(xv_ref[0].astype(bf16), wv_ref[...], preferred_element_type=f32)
    new_mid = float(nb * CMP_BLOCK) + (CMP_BLOCK - 1) / 2
    new_ok = nb * CMP_BLOCK + (CMP_BLOCK - 1) <= t_pos
    lane = lax.broadcasted_iota(jnp.int32, (1, 128), 1)
    imps = []
    for kv in range(N_KV):
        q = q_ref[0, kv]
        g1 = lax.broadcasted_iota(jnp.int32, (_GP, 1), 0) + (kv * GROUP + 1)
        slope = jnp.exp2(-0.5 * g1.astype(f32))
        s = jnp.dot(q.astype(bf16), ckt_ref[0, kv], preferred_element_type=f32)
        s = s - slope * (float(t_pos) - blk_mid)
        s = jnp.where(blk * CMP_BLOCK + (CMP_BLOCK - 1) <= t_pos, s, -jnp.inf)
        qn = q.astype(bf16).astype(f32)
        s_new = jnp.sum(qn * ck_new[kv:kv + 1].astype(bf16).astype(f32), axis=-1, keepdims=True)
        s_new = s_new - slope * (float(t_pos) - new_mid)
        s_new = jnp.where(new_ok, s_new, -jnp.inf)
        m = jnp.maximum(jnp.max(s, axis=-1, keepdims=True), s_new)
        m = jnp.where(jnp.isfinite(m), m, 0.0)
        p = jnp.exp(s - m)
        p_new = jnp.exp(s_new - m)
        den = jnp.maximum(jnp.sum(p, axis=-1, keepdims=True) + p_new, 1e-30)
        p = p / den
        p_new = p_new / den
        oc = jnp.dot(p.astype(bf16), cv_ref[0, kv], preferred_element_type=f32)
        oc_ref[0, kv] = oc + p_new * cv_new[kv:kv + 1]
        imps.append(jnp.sum(p[:GROUP], axis=0, keepdims=True))
    imp = jnp.concatenate(imps, axis=0)
    cur = t_pos // CMP_BLOCK
    forced = (blk == 0) | (blk == cur) | (blk == cur - 1)
    v = jnp.where(forced, FORCED_SCORE, jnp.where(blk * CMP_BLOCK <= t_pos, imp, -1.0))
    out = jnp.zeros((N_KV, 128), jnp.int32)
    for it in range(n_pick):
        mx = jnp.max(v, axis=-1, keepdims=True)
        idx = jnp.min(jnp.where(v == mx, blk, nb), axis=-1, keepdims=True)
        out = jnp.where(lane == it, idx, out)
        v = jnp.where(blk == idx, -jnp.inf, v)
    idx_ref[0] = out


def _dec_cmp(q, ck, cv, xk, xv, wk, wv, g_kc, t_pos, n_pick):
    b, nb = ck.shape[:2]
    ckt = ck.astype(bf16).transpose(0, 2, 3, 1)
    cvh = cv.astype(bf16).transpose(0, 2, 1, 3)
    kd = CMP_BLOCK * HEAD_DIM
    return pl.pallas_call(
        functools.partial(_dec_cmp_kernel, n_pick, t_pos),
        out_shape=[jax.ShapeDtypeStruct((b, N_KV, _GP, HEAD_DIM), f32),
                   jax.ShapeDtypeStruct((b, N_KV, 128), jnp.int32)],
        grid=(b,),
        in_specs=[
            pl.BlockSpec((1, N_KV, _GP, HEAD_DIM), lambda i: (i, 0, 0, 0)),
            pl.BlockSpec((1, N_KV, HEAD_DIM, nb), lambda i: (i, 0, 0, 0)),
            pl.BlockSpec((1, N_KV, nb, HEAD_DIM), lambda i: (i, 0, 0, 0)),
            pl.BlockSpec((1, _GP, kd), lambda i: (i, 0, 0)),
            pl.BlockSpec((1, _GP, kd), lambda i: (i, 0, 0)),
            pl.BlockSpec((kd, HEAD_DIM), lambda i: (0, 0)),
            pl.BlockSpec((kd, HEAD_DIM), lambda i: (0, 0)),
            pl.BlockSpec((1, HEAD_DIM), lambda i: (0, 0)),
        ],
        out_specs=[pl.BlockSpec((1, N_KV, _GP, HEAD_DIM), lambda i: (i, 0, 0, 0)),
                   pl.BlockSpec((1, N_KV, 128), lambda i: (i, 0, 0))],
        compiler_params=_cparams(("parallel",)),
        name="dec_cmp",
    )(q, ckt, cvh, xk, xv, wk, wv, g_kc.reshape(1, HEAD_DIM))


def _dec_sel_kernel(t_pos, pt_ref, idx_ref, q_ref, gt_ref, oc_ref, ksn_ref, vsn_ref, kwn_ref, vwn_ref,
                    wk_ref, wv_ref, *rest):
    del pt_ref
    kb = rest[:N_KV]
    vb = rest[N_KV:2 * N_KV]
    o_ref, m_ref, l_ref, acc_ref = rest[2 * N_KV:]
    b = pl.program_id(0)
    s_id = pl.program_id(1)
    n_pick = pl.num_programs(1)
    n_buf = wk_ref.shape[-1]
    page = 2 * CMP_BLOCK

    @pl.when(s_id == 0)
    def _():
        m_ref[...] = jnp.full(m_ref.shape, -jnp.inf, f32)
        l_ref[...] = jnp.zeros(l_ref.shape, f32)
        acc_ref[...] = jnp.zeros(acc_ref.shape, f32)

    def slopes(kv):
        g1 = lax.broadcasted_iota(jnp.int32, (_GP, 1), 0) + (kv * GROUP + 1)
        return jnp.exp2(-0.5 * g1.astype(f32))

    lane = lax.broadcasted_iota(jnp.int32, (1, page), 1)
    for kv in range(N_KV):
        q = q_ref[0, kv].astype(bf16)
        blk = idx_ref[b, kv * 128 + s_id]
        dist = t_pos - ((blk // 2) * page + lane)
        s = jnp.dot(q, kb[kv][0, 0].astype(bf16), preferred_element_type=f32)
        s = s - slopes(kv) * dist.astype(f32)
        s = jnp.where((lane // CMP_BLOCK == blk % 2) & (dist >= 0), s, -jnp.inf)
        m_old = m_ref[kv]
        m_new = jnp.maximum(m_old, jnp.max(s, axis=-1, keepdims=True))
        p = jnp.exp(s - m_new)
        alpha = jnp.exp(m_old - m_new)
        l_ref[kv] = alpha * l_ref[kv] + jnp.sum(p, axis=-1, keepdims=True)
        acc_ref[kv] = alpha * acc_ref[kv] + lax.dot_general(
            p.astype(bf16), vb[kv][0, 0].astype(bf16), _NT, preferred_element_type=f32)
        m_ref[kv] = m_new

    @pl.when(s_id == n_pick - 1)
    def _():
        c = lax.broadcasted_iota(jnp.int32, (1, n_buf), 1)
        kpos = t_pos - n_buf + c
        distw = t_pos - kpos
        okw = (kpos >= 0) & (distw >= 0) & (distw < WINDOW)
        for kv in range(N_KV):
            q = q_ref[0, kv].astype(bf16)
            qf = q.astype(f32)
            slope = slopes(kv)
            s_new = jnp.sum(qf * ksn_ref[0, kv:kv + 1].astype(bf16).astype(f32), axis=-1, keepdims=True)
            m_old = m_ref[kv]
            m_new = jnp.maximum(m_old, s_new)
            alpha = jnp.exp(m_old - m_new)
            p_new = jnp.exp(s_new - m_new)
            l = alpha * l_ref[kv] + p_new
            acc = alpha * acc_ref[kv] + p_new.astype(bf16).astype(f32) * vsn_ref[0, kv:kv + 1].astype(bf16).astype(f32)
            o_s = acc / jnp.maximum(l, 1e-30)
            sw = jnp.dot(q, wk_ref[0, kv].astype(bf16), preferred_element_type=f32)
            sw = jnp.where(okw, sw - slope * distw.astype(f32), -jnp.inf)
            sw_new = jnp.sum(qf * kwn_ref[0, kv:kv + 1].astype(bf16).astype(f32), axis=-1, keepdims=True)
            mw = jnp.maximum(jnp.max(sw, axis=-1, keepdims=True), sw_new)
            pw = jnp.exp(sw - mw)
            pw_new = jnp.exp(sw_new - mw)
            den = jnp.maximum(jnp.sum(pw, axis=-1, keepdims=True) + pw_new, 1e-30)
            o_w = lax.dot_general((pw / den).astype(bf16), wv_ref[0, kv].astype(bf16), _NT,
                                  preferred_element_type=f32)
            o_w = o_w + (pw_new / den).astype(bf16).astype(f32) * vwn_ref[0, kv:kv + 1].astype(bf16).astype(f32)
            gt = gt_ref[0, kv]
            o_ref[0, kv] = gt[:, 0:1] * oc_ref[0, kv] + gt[:, 1:2] * o_s + gt[:, 2:3] * o_w


def _dec_sel(q, gates, o_c, idx, table, pool_k, pool_v, ks_new, vs_new, kw_new, vw_new, win_k, win_v,
             t_pos, n_pick):
    b = q.shape[0]
    n_buf = win_k.shape[-1]
    page = pool_k.shape[-1]
    hspec = pl.BlockSpec((1, N_KV, _GP, HEAD_DIM), lambda i, s, pt, ix: (i, 0, 0, 0))
    nspec = pl.BlockSpec((1, N_KV, HEAD_DIM), lambda i, s, pt, ix: (i, 0, 0))
    wspec = pl.BlockSpec((1, N_KV, HEAD_DIM, n_buf), lambda i, s, pt, ix: (i, 0, 0, 0))

    def blk_spec(kv):
        def imap(i, s, pt, ix):
            return (pt[i, ix[i, kv * 128 + s] // 2], kv, 0, 0)
        return pl.BlockSpec((1, 1, HEAD_DIM, page), imap)

    kspecs = [blk_spec(kv) for kv in range(N_KV)]
    return pl.pallas_call(
        functools.partial(_dec_sel_kernel, t_pos),
        out_shape=jax.ShapeDtypeStruct((b, N_KV, _GP, HEAD_DIM), f32),
        grid_spec=pltpu.PrefetchScalarGridSpec(
            num_scalar_prefetch=2,
            grid=(b, n_pick),
            in_specs=[hspec, pl.BlockSpec((1, N_KV, _GP, 3), lambda i, s, pt, ix: (i, 0, 0, 0)), hspec,
                      nspec, nspec, nspec, nspec, wspec, wspec] + kspecs + kspecs,
            out_specs=hspec,
            scratch_shapes=[pltpu.VMEM((N_KV, _GP, 1), f32), pltpu.VMEM((N_KV, _GP, 1), f32),
                            pltpu.VMEM((N_KV, _GP, HEAD_DIM), f32)]),
        compiler_params=_cparams(("arbitrary", "arbitrary")),
        name="dec_sel",
    )(table, idx, q, gates, o_c, ks_new, vs_new, kw_new, vw_new, win_k, win_v,
      *([pool_k] * N_KV), *([pool_v] * N_KV))


def _head_rms(x, g):
    ms = jnp.mean(x * x, axis=-1, keepdims=True)
    return x * lax.rsqrt(ms + NORM_EPS) * g


def _prepare(p):
    q = dict(p)
    for n in ("ffn1_gate", "ffn1_up", "ffn1_down", "ffn2_gate", "ffn2_up", "ffn2_down", "w_pa", "w_pb", "w_out"):
        q[n] = p[n].astype(bf16)
    q["w_in_p"] = _pad_cols(p["w_in"]).astype(bf16)
    q["mu_p"] = _pad_rw_cols(p["rw_mu"][None])
    q["w2p"] = _lora_pad(p["rw_w2"])
    q["a2p"] = _lora_pad(p["rw_a2"])
    q["g2"] = p["rw_g2"].astype(bf16)
    return q


def _nsa_proj(z, p):
    m = z.shape[0]
    kvs = (m, N_KV, HEAD_DIM)
    q = _head_rms(_seg(z, "q").reshape(m, N_HEADS, HEAD_DIM), p["g_q"])
    kc = _seg(z, "kc").reshape(kvs)
    vc = _seg(z, "vc").reshape(kvs)
    ks = _head_rms(_seg(z, "ks").reshape(kvs), p["g_ks"])
    vs = _seg(z, "vs").reshape(kvs)
    kw = _head_rms(_seg(z, "kw").reshape(kvs), p["g_kw"])
    vw = _seg(z, "vw").reshape(kvs)
    gates = jax.nn.sigmoid(_seg(z, "ga").reshape(m, N_HEADS, 3))
    return q, gates, kc, vc, ks, vs, kw, vw


def _layer_prompt(x, p):
    t = x.shape[0]
    x = _ffn(x, p["n_ffn1"], p["ffn1_gate"], p["ffn1_up"], p["ffn1_down"])
    z = _inproj(x, p["n_mix"], p["w_in_p"])
    q, gates, kc, vc, ks, vs, kw, vw = _nsa_proj(z, p)
    ck = _head_rms(_compress(kc.reshape(1, t, KV_DIM), p["pe_cmp_k"], p["w_cmp_k"])[0]
                   .reshape(-1, N_KV, HEAD_DIM), p["g_kc"])
    cv = _compress(vc.reshape(1, t, KV_DIM), p["pe_cmp_v"], p["w_cmp_v"])[0].reshape(-1, N_KV, HEAD_DIM)
    o_a = _nsa_prompt(q, gates, ck, cv, ks, vs, kw, vw).astype(bf16)
    shift0 = jnp.zeros((8, _Z_COLS), f32)
    s0t = jnp.zeros((RW_HEADS, RW_N, RW_N), f32)
    o_b, st = _rwkv_prompt(z, shift0, s0t, p)
    x = _merge(o_a, o_b, z, x, p["w_pa"], p["w_pb"], p["w_out"])
    x = _ffn(x, p["n_ffn2"], p["ffn2_gate"], p["ffn2_up"], p["ffn2_down"])
    n_buf = min(WINDOW, t)
    states = (kc, vc, ks, vs, kw[-n_buf:], vw[-n_buf:], _rw_cols(z[-1:]), st.transpose(0, 2, 1)[None])
    return x, states


def _layer_sample(x, p, past):
    b = x.shape[0]
    table = past["page_table"]
    page = past["cmp_k"].shape[1]
    t_pos = table.shape[1] * page
    x = _ffn(x, p["n_ffn1"], p["ffn1_gate"], p["ffn1_up"], p["ffn1_down"])
    z = _inproj(x, p["n_mix"], p["w_in_p"])
    q, gates, kc, vc, ks, vs, kw, vw = _nsa_proj(z, p)

    fm = lambda a: a.transpose(0, 2, 3, 1)
    pool_rows = lambda a: fm(a).reshape(-1, _PAGE_ROWS, page)

    nbp = t_pos // CMP_BLOCK
    ck = _head_rms(_compress_pages(pool_rows(past["cmp_k"]), table, p["pe_cmp_k"], p["w_cmp_k"]), p["g_kc"])
    cv = _compress_pages(pool_rows(past["cmp_v"]), table, p["pe_cmp_v"], p["w_cmp_v"])

    def new_block_rows(k_new, pe):
        first = k_new + pe[0]
        rest = jnp.broadcast_to(pe[1:].reshape(1, 1, -1), (b, N_KV, (CMP_BLOCK - 1) * HEAD_DIM))
        rows = jnp.concatenate([first, rest], axis=-1)
        return jnp.pad(rows, ((0, 0), (0, _GP - N_KV), (0, 0)))

    n_sel = min(N_SEL, nbp + 1)
    n_pick = n_sel - 1
    scale = HEAD_DIM ** -0.5
    pad_g = lambda a: jnp.pad(a, ((0, 0), (0, 0), (0, _GP - GROUP), (0, 0)))
    qh = pad_g((q * scale).reshape(b, N_KV, GROUP, HEAD_DIM))
    gth = pad_g(gates.reshape(b, N_KV, GROUP, 3))
    kd = CMP_BLOCK * HEAD_DIM
    o_c, idx = _dec_cmp(qh, ck, cv, new_block_rows(kc, p["pe_cmp_k"]), new_block_rows(vc, p["pe_cmp_v"]),
                        p["w_cmp_k"].reshape(kd, HEAD_DIM).astype(bf16),
                        p["w_cmp_v"].reshape(kd, HEAD_DIM).astype(bf16), p["g_kc"], t_pos, n_pick)
    o_a = _dec_sel(qh, gth, o_c, idx.reshape(b, N_KV * 128), table, fm(past["slc_k"]), fm(past["slc_v"]),
                   ks, vs, kw, vw, fm(past["win_k"]), fm(past["win_v"]), t_pos, n_pick)
    o_a = o_a[:, :, :GROUP].reshape(b, NSA_DIM).astype(bf16)

    prev = _pad_rw_cols(past["shift"])
    r, k, v, kk, ka, ld, g, bonus = _rw_prep(
        z, prev, p["mu_p"], p["rw_w0"], p["w2p"], p["rw_a0"], p["a2p"], p["g2"],
        p["rw_k_k"], p["rw_k_a"], p["rw_r_k"], False)
    tb = lambda a: a.transpose(1, 0, 2).reshape(b, RW_HEADS, RW_N)
    y, wkv = _wkv_step(past["wkv"], tb(r), tb(k), tb(v), tb(kk), tb(ka), tb(ld))
    y = y.reshape(b, RW_HEADS // 2, 2 * RW_N).transpose(1, 0, 2)
    o_b = _rw_post(y, bonus, g, p["rw_ln_w"], p["rw_ln_b"])

    x = _merge(o_a, o_b, z, x, p["w_pa"], p["w_pb"], p["w_out"])
    x = _ffn(x, p["n_ffn2"], p["ffn2_gate"], p["ffn2_up"], p["ffn2_down"])
    kvs = lambda a: a.reshape(b, 1, N_KV, HEAD_DIM)
    win_k = jnp.concatenate([past["win_k"][:, 1:], kvs(kw)], axis=1)
    win_v = jnp.concatenate([past["win_v"][:, 1:], kvs(vw)], axis=1)
    states = (kvs(kc), kvs(vc), kvs(ks), kvs(vs), win_k, win_v, _rw_cols(z), wkv)
    return x, states


def kernel(x_prompt, x_sample, cache_cmp_k, cache_cmp_v, cache_slc_k, cache_slc_v, cache_win_k, cache_win_v,
           state_shift, state_wkv, page_table,
           n_ffn1, ffn1_gate, ffn1_up, ffn1_down, n_mix, w_in, g_q, g_kc, g_ks, g_kw,
           w_cmp_k, pe_cmp_k, w_cmp_v, pe_cmp_v,
           rw_mu, rw_w0, rw_w2, rw_a0, rw_a2, rw_g2, rw_k_k, rw_k_a, rw_r_k, rw_ln_w, rw_ln_b,
           w_pa, w_pb, w_out, n_ffn2, ffn2_gate, ffn2_up, ffn2_down):
    assert x_prompt.shape[0] == 1 and x_sample.shape[1] == 1 and n_ffn1.shape[0] == 1
    l = 0
    p = _prepare(dict(
        n_ffn1=n_ffn1[l], ffn1_gate=ffn1_gate[l], ffn1_up=ffn1_up[l], ffn1_down=ffn1_down[l],
        n_mix=n_mix[l], w_in=w_in[l], g_q=g_q[l], g_kc=g_kc[l], g_ks=g_ks[l], g_kw=g_kw[l],
        w_cmp_k=w_cmp_k[l], pe_cmp_k=pe_cmp_k[l], w_cmp_v=w_cmp_v[l], pe_cmp_v=pe_cmp_v[l],
        rw_mu=rw_mu[l], rw_w0=rw_w0[l], rw_w2=rw_w2[l], rw_a0=rw_a0[l], rw_a2=rw_a2[l], rw_g2=rw_g2[l],
        rw_k_k=rw_k_k[l], rw_k_a=rw_k_a[l], rw_r_k=rw_r_k[l], rw_ln_w=rw_ln_w[l], rw_ln_b=rw_ln_b[l],
        w_pa=w_pa[l], w_pb=w_pb[l], w_out=w_out[l],
        n_ffn2=n_ffn2[l], ffn2_gate=ffn2_gate[l], ffn2_up=ffn2_up[l], ffn2_down=ffn2_down[l]))
    t = x_prompt.shape[1]
    y_p, sp = _layer_prompt(x_prompt[0], p)
    past = dict(page_table=page_table, cmp_k=cache_cmp_k[l], cmp_v=cache_cmp_v[l], slc_k=cache_slc_k[l],
                slc_v=cache_slc_v[l], win_k=cache_win_k[l], win_v=cache_win_v[l], shift=state_shift[l],
                wkv=state_wkv[l])
    y_s, ss = _layer_sample(x_sample[:, 0], p, past)
    kvp = lambda a: a.reshape(1, 1, -1, N_KV, HEAD_DIM)
    outs_p = (kvp(sp[0]), kvp(sp[1]), kvp(sp[2]), kvp(sp[3]), kvp(sp[4]), kvp(sp[5]), sp[6][None], sp[7][None])
    outs_s = tuple(a[None] for a in ss)
    return (y_p.reshape(1, t, D_MODEL), y_s[:, None, :]) + outs_p + outs_s
```

```python
import functools

import jax
import jax.numpy as jnp
from jax import lax
from jax.experimental import pallas as pl
from jax.experimental.pallas import tpu as pltpu

f32 = jnp.float32
bf16 = jnp.bfloat16

D_MODEL = 2048
N_HEADS = 16
N_KV = 4
GROUP = 4
HEAD_DIM = 64
NSA_DIM = N_HEADS * HEAD_DIM
KV_DIM = N_KV * HEAD_DIM
CMP_BLOCK = 64
N_SEL = 16
WINDOW = 512
Q_BLOCK = 128
FORCED_SCORE = 1e3
RW_HEADS = 16
RW_N = 64
RW_DIM = RW_HEADS * RW_N
W_LORA = 96
A_LORA = 96
G_LORA = 256
D_FF = 5632
NORM_EPS = 1e-6
GN_EPS = 64e-5
NEG_BIG = -1e30

VMEM_LIMIT = 56 * 1024 * 1024

_SEGS = (
    ("r", 2608, 1024, 1024),
    ("k", 3728, 1024, 1024),
    ("v", 4752, 1024, 1024),
    ("q", 0, 1024, 1024),
    ("g_a", 6128, 2048, 2048),
    ("g_b", 8176, 2048, 2048),
    ("kc", 1024, 256, 256),
    ("vc", 1280, 256, 256),
    ("ks", 1536, 256, 256),
    ("vs", 1792, 256, 256),
    ("kw", 2048, 256, 256),
    ("vw", 2304, 256, 256),
    ("gl", 5872, 256, 256),
    ("ga", 2560, 48, 128),
    ("wl", 3632, 96, 128),
    ("al", 5776, 96, 128),
)
_Z_COLS = 10752


def _seg_offsets():
    offs, o = {}, 0
    for name, _, w, pw in _SEGS:
        assert o % pw == 0
        offs[name] = (o, w, pw)
        o += pw
    return offs, o


_OFF, _USED = _seg_offsets()


_IN_COLS = 10224
_RW_START = 2608
_RW_COLS = 3520


def _pad_cols(x):
    parts = []
    for _, s, w, pw in _SEGS:
        seg = x[..., s:s + w]
        if pw != w:
            seg = jnp.pad(seg, [(0, 0)] * (x.ndim - 1) + [(0, pw - w)])
        parts.append(seg)
    parts.append(jnp.zeros(x.shape[:-1] + (_Z_COLS - _USED,), x.dtype))
    return jnp.concatenate(parts, axis=-1)


def _pad_rw_cols(x):
    pad = [(0, 0)] * (x.ndim - 1) + [(_RW_START, _IN_COLS - _RW_START - _RW_COLS)]
    return _pad_cols(jnp.pad(x, pad))


def _seg(z, name):
    o, w, _ = _OFF[name]
    return z[..., o:o + w]


def _rw_cols(z):
    return jnp.concatenate([_seg(z, n) for n in ("r", "wl", "k", "v", "al", "gl")], axis=-1)


def _cparams(sem, vmem=VMEM_LIMIT):
    return pltpu.CompilerParams(dimension_semantics=sem, vmem_limit_bytes=vmem)


def _row_tile(m, pref):
    return pref if m % pref == 0 else m


def _rms(x, g):
    ms = jnp.mean(x * x, axis=-1, keepdims=True)
    return x * lax.rsqrt(ms + NORM_EPS) * g


def _ffn_kernel(x_ref, g_ref, wg_ref, wu_ref, wd_ref, o_ref, h_ref, acc_ref):
    j = pl.program_id(1)

    @pl.when(j == 0)
    def _():
        h_ref[...] = _rms(x_ref[...], g_ref[...]).astype(bf16)
        acc_ref[...] = jnp.zeros_like(acc_ref)

    h = h_ref[...]
    g = jnp.dot(h, wg_ref[...], preferred_element_type=f32)
    u = jnp.dot(h, wu_ref[...], preferred_element_type=f32)
    a = (g * jax.nn.sigmoid(g) * u).astype(bf16)
    acc_ref[...] += jnp.dot(a, wd_ref[...], preferred_element_type=f32)

    @pl.when(j == pl.num_programs(1) - 1)
    def _():
        o_ref[...] = x_ref[...] + 0.5 * acc_ref[...]


def _ffn(x, gain, wg, wu, wd):
    m, d = x.shape
    ff = wg.shape[1]
    bm = _row_tile(m, 512)
    bf = 512
    return pl.pallas_call(
        _ffn_kernel,
        out_shape=jax.ShapeDtypeStruct((m, d), f32),
        grid=(m // bm, ff // bf),
        in_specs=[
            pl.BlockSpec((bm, d), lambda i, j: (i, 0)),
            pl.BlockSpec((1, d), lambda i, j: (0, 0)),
            pl.BlockSpec((d, bf), lambda i, j: (0, j)),
            pl.BlockSpec((d, bf), lambda i, j: (0, j)),
            pl.BlockSpec((bf, d), lambda i, j: (j, 0)),
        ],
        out_specs=pl.BlockSpec((bm, d), lambda i, j: (i, 0)),
        scratch_shapes=[pltpu.VMEM((bm, d), bf16), pltpu.VMEM((bm, d), f32)],
        compiler_params=_cparams(("parallel", "arbitrary")),
        name="ffn",
    )(x, gain.reshape(1, d), wg, wu, wd)


def _inproj_kernel(x_ref, g_ref, w_ref, o_ref, h_ref):
    @pl.when(pl.program_id(1) == 0)
    def _():
        h_ref[...] = _rms(x_ref[...], g_ref[...]).astype(bf16)

    o_ref[...] = jnp.dot(h_ref[...], w_ref[...], preferred_element_type=f32)


def _inproj(x, gain, w):
    m, d = x.shape
    n = w.shape[1]
    bm = _row_tile(m, 1024)
    bn = 1536
    assert n % bn == 0
    return pl.pallas_call(
        _inproj_kernel,
        out_shape=jax.ShapeDtypeStruct((m, n), f32),
        grid=(m // bm, n // bn),
        in_specs=[
            pl.BlockSpec((bm, d), lambda i, j: (i, 0)),
            pl.BlockSpec((1, d), lambda i, j: (0, 0)),
            pl.BlockSpec((d, bn), lambda i, j: (0, j)),
        ],
        out_specs=pl.BlockSpec((bm, bn), lambda i, j: (i, j)),
        scratch_shapes=[pltpu.VMEM((bm, d), bf16)],
        compiler_params=_cparams(("parallel", "arbitrary")),
        name="inproj",
    )(x, gain.reshape(1, d), w)


def _merge_kernel(oa_ref, ob_ref, ga_ref, gb_ref, x_ref, wpa_ref, wpb_ref, wo_ref, o_ref):
    pa = jnp.dot(oa_ref[...], wpa_ref[...], preferred_element_type=f32)
    pb = jnp.dot(ob_ref[...], wpb_ref[...], preferred_element_type=f32)
    mix = jax.nn.sigmoid(ga_ref[...]) * pa + jax.nn.sigmoid(gb_ref[...]) * pb
    o_ref[...] = x_ref[...] + jnp.dot(mix.astype(bf16), wo_ref[...], preferred_element_type=f32)


def _merge(oa, ob, z, x, wpa, wpb, wo):
    m, d = x.shape
    bm = _row_tile(m, 256)
    ca = _OFF["g_a"][0] // d
    cb = _OFF["g_b"][0] // d
    return pl.pallas_call(
        _merge_kernel,
        out_shape=jax.ShapeDtypeStruct((m, d), f32),
        grid=(m // bm,),
        in_specs=[
            pl.BlockSpec((bm, NSA_DIM), lambda i: (i, 0)),
            pl.BlockSpec((bm, RW_DIM), lambda i: (i, 0)),
            pl.BlockSpec((bm, d), lambda i: (i, ca)),
            pl.BlockSpec((bm, d), lambda i: (i, cb)),
            pl.BlockSpec((bm, d), lambda i: (i, 0)),
            pl.BlockSpec((NSA_DIM, d), lambda i: (0, 0)),
            pl.BlockSpec((RW_DIM, d), lambda i: (0, 0)),
            pl.BlockSpec((d, d), lambda i: (0, 0)),
        ],
        out_specs=pl.BlockSpec((bm, d), lambda i: (i, 0)),
        compiler_params=_cparams(("parallel",)),
        name="merge",
    )(oa, ob, z, z, x, wpa, wpb, wo)


def _compress_kernel(x_ref, pe_ref, w_ref, o_ref):
    nb = o_ref.shape[1]
    acc = [jnp.zeros((nb, 128), f32), jnp.zeros((nb, 128), f32)]
    for j in range(CMP_BLOCK):
        pe_j = pe_ref[pl.ds(j, 1), :]
        w_j = w_ref[j]
        for h in range(2):
            xj = x_ref[0, pl.ds(2 * j + h, nb, stride=2 * CMP_BLOCK), :] + pe_j
            acc[h] = acc[h] + jnp.dot(xj.astype(bf16), w_j, preferred_element_type=f32)
    o_ref[0] = jnp.concatenate(acc, axis=-1)


def _compress(x, pe, w):
    b, l, _ = x.shape
    nb = l // CMP_BLOCK
    pe_t = jnp.tile(pe, (1, 2))
    eye = jnp.eye(2, dtype=f32)
    wbd = jnp.einsum("ab,jde->jadbe", eye, w).reshape(CMP_BLOCK, 128, 128).astype(bf16)
    return pl.pallas_call(
        _compress_kernel,
        out_shape=jax.ShapeDtypeStruct((b, nb, KV_DIM), f32),
        grid=(b,),
        in_specs=[
            pl.BlockSpec((1, 2 * l, 128), lambda i: (i, 0, 0)),
            pl.BlockSpec((CMP_BLOCK, 128), lambda i: (0, 0)),
            pl.BlockSpec((CMP_BLOCK, 128, 128), lambda i: (0, 0, 0)),
        ],
        out_specs=pl.BlockSpec((1, nb, KV_DIM), lambda i: (i, 0, 0)),
        compiler_params=_cparams(("parallel",)),
        name="compress",
    )(x.reshape(b, 2 * l, 128), pe_t, wbd)


_TK = 512
_ROWS = GROUP * Q_BLOCK
_POS_FEATS = 16


def _softmax_cols(s):
    m = jnp.max(s, axis=0, keepdims=True)
    m = jnp.where(jnp.isfinite(m), m, 0.0)
    p = jnp.exp(s - m)
    return p / jnp.maximum(jnp.sum(p, axis=0, keepdims=True), 1e-30)


def _topk_mask_t(vt, n_sel):
    nb = vt.shape[0]
    bi = lax.broadcasted_iota(jnp.int32, vt.shape, 0)
    sel = jnp.zeros(vt.shape, f32)
    for _ in range(n_sel):
        mx = jnp.max(vt, axis=0, keepdims=True)
        idx = jnp.min(jnp.where(vt == mx, bi, nb), axis=0, keepdims=True)
        hit = bi == idx
        sel = jnp.where(hit, 1.0, sel)
        vt = jnp.where(hit, -jnp.inf, vt)
    return sel


def _nsa_kernel(qt_ref, gt_ref, ck_ref, cvt_ref, ke_ref, vst_ref, kw_ref, vwt_ref,
                o_ref, a0w_ref, used_ref):
    kv = pl.program_id(0)
    i = pl.program_id(1)
    nb = ck_ref.shape[1]
    n_sel = min(N_SEL, nb)
    wk = WINDOW + Q_BLOCK

    lane = lax.broadcasted_iota(jnp.int32, (1, _ROWS), 1)
    grp = lane // Q_BLOCK
    tl = lane % Q_BLOCK
    slope = jnp.exp2(-0.5 * (kv * GROUP + grp + 1).astype(f32))
    tlf = tl.astype(f32)

    @pl.when(i == 0)
    def _():
        kroww = lax.broadcasted_iota(jnp.int32, (wk, 1), 0).astype(f32)
        a0w_ref[...] = slope * (tlf + float(WINDOW) - kroww)

    qt = qt_ref[0, 0]
    t0 = i * Q_BLOCK
    tok = t0 + tl

    blk = lax.broadcasted_iota(jnp.int32, (nb, 1), 0)
    blk_mid = (blk * CMP_BLOCK).astype(f32) + (CMP_BLOCK - 1) / 2
    s = jnp.dot(ck_ref[0], qt, preferred_element_type=f32)
    s = s - slope * (tok.astype(f32) - blk_mid)
    s = jnp.where(blk * CMP_BLOCK + (CMP_BLOCK - 1) <= tok, s, -jnp.inf)
    p_c = _softmax_cols(s)
    o_c = jnp.dot(cvt_ref[0], p_c.astype(bf16), preferred_element_type=f32)

    w0 = pl.multiple_of(t0, Q_BLOCK)
    s = jnp.dot(kw_ref[0, pl.ds(w0, wk), :], qt, preferred_element_type=f32) - a0w_ref[...]
    kroww = lax.broadcasted_iota(jnp.int32, (wk, 1), 0)
    dist = tl + WINDOW - kroww
    s = jnp.where((t0 - WINDOW + kroww >= 0) & (dist >= 0) & (dist < WINDOW), s, -jnp.inf)
    p_w = _softmax_cols(s)
    o_w = jnp.dot(vwt_ref[0, :, pl.ds(w0, wk)], p_w.astype(bf16), preferred_element_type=f32)

    imp = p_c[:, 0:Q_BLOCK]
    for g in range(1, GROUP):
        imp = imp + p_c[:, g * Q_BLOCK:(g + 1) * Q_BLOCK]
    tq = t0 + lax.broadcasted_iota(jnp.int32, (1, Q_BLOCK), 1)
    cur = tq // CMP_BLOCK
    forced = (blk == 0) | (blk == cur) | (blk == cur - 1)
    imp = jnp.where(forced, FORCED_SCORE, jnp.where(blk * CMP_BLOCK <= tq, imp, -1.0))
    sel_t = _topk_mask_t(imp, n_sel)
    bias_t = jnp.where(sel_t > 0.0, 0.0, NEG_BIG).astype(bf16)
    bpt = _TK // CMP_BLOCK
    for j in range(nb // bpt):
        used_ref[j] = jnp.max(sel_t[j * bpt:(j + 1) * bpt, :]).astype(jnp.int32)
    s1 = slope.astype(bf16).astype(f32)
    s2 = (slope - s1).astype(bf16).astype(f32)
    s3 = ((slope - s1) - s2).astype(bf16).astype(f32)
    fr = lax.broadcasted_iota(jnp.int32, (_POS_FEATS, 1), 0)
    pieces = jnp.where(fr < 2, s1, jnp.where(fr < 4, s2, jnp.where(fr < 6, s3, 0.0))).astype(bf16)
    rhs = jnp.concatenate([qt, jnp.concatenate([bias_t] * GROUP, axis=1), pieces], axis=0)

    def tile(j, carry, causal):
        m, l, acc = carry
        c0 = pl.multiple_of(j * _TK, _TK)
        s = jnp.dot(ke_ref[0, pl.ds(c0, _TK), :], rhs, preferred_element_type=f32)
        if causal:
            krow = lax.broadcasted_iota(jnp.int32, (_TK, 1), 0)
            s = jnp.where(tok - (c0 + krow) >= 0, s, -jnp.inf)
        off = slope * (tok - c0).astype(f32)
        m_new = jnp.maximum(m, jnp.max(s, axis=0, keepdims=True) - off)
        p = jnp.exp(s - (m_new + off))
        alpha = jnp.exp(m - m_new)
        l = alpha * l + jnp.sum(p, axis=0, keepdims=True)
        acc = alpha * acc + jnp.dot(vst_ref[0, :, pl.ds(c0, _TK)], p.astype(bf16),
                                    preferred_element_type=f32)
        return m_new, l, acc

    jd = t0 // _TK
    init = (jnp.full((1, _ROWS), -jnp.inf, f32), jnp.zeros((1, _ROWS), f32),
            jnp.zeros((HEAD_DIM, _ROWS), f32))
    carry = lax.fori_loop(
        0, jd, lambda j, c: lax.cond(used_ref[j] > 0, lambda cc: tile(j, cc, False), lambda cc: cc, c), init)
    _, l, acc = tile(jd, carry, True)
    o_s = acc / jnp.maximum(l, 1e-30)

    gt = gt_ref[0, 0]
    o_ref[0, 0] = gt[0:1] * o_c + gt[1:2] * o_s + gt[2:3] * o_w


def _nsa_prompt(q, gates, ck, cv, ks, vs, kw, vw):
    t = q.shape[0]
    nqb = t // Q_BLOCK
    nb = ck.shape[0]
    scale = HEAD_DIM ** -0.5
    qt = (q * scale).astype(bf16).reshape(nqb, Q_BLOCK, N_KV, GROUP, HEAD_DIM)
    qt = qt.transpose(2, 0, 4, 3, 1).reshape(N_KV, nqb, HEAD_DIM, _ROWS)
    gt = gates.reshape(nqb, Q_BLOCK, N_KV, GROUP, 3).transpose(2, 0, 4, 3, 1).reshape(N_KV, nqb, 3, _ROWS)
    ckh = ck.astype(bf16).transpose(1, 0, 2)
    cvt = cv.astype(bf16).transpose(1, 2, 0)
    pos = jnp.arange(t)
    onehot = (pos[:, None] // CMP_BLOCK == jnp.arange(nb)[None, :]).astype(bf16)
    off_hi = (pos % _TK) // 16 * 16
    off_lo = pos % 16
    feats = jnp.stack([off_hi, off_lo] * 3 + [jnp.zeros_like(pos)] * (_POS_FEATS - 6), axis=1).astype(bf16)
    ke = jnp.concatenate([ks.astype(bf16).transpose(1, 0, 2),
                          jnp.broadcast_to(jnp.concatenate([onehot, feats], axis=1)[None],
                                           (N_KV, t, nb + _POS_FEATS))], axis=-1)
    vst = vs.astype(bf16).transpose(1, 2, 0)
    kwh = jnp.pad(kw.astype(bf16).transpose(1, 0, 2), ((0, 0), (WINDOW, 0), (0, 0)))
    vwt = jnp.pad(vw.astype(bf16).transpose(1, 2, 0), ((0, 0), (0, 0), (WINDOW, 0)))
    kd = HEAD_DIM + nb + _POS_FEATS
    out = pl.pallas_call(
        _nsa_kernel,
        out_shape=jax.ShapeDtypeStruct((N_KV, nqb, HEAD_DIM, _ROWS), f32),
        grid=(N_KV, nqb),
        in_specs=[
            pl.BlockSpec((1, 1, HEAD_DIM, _ROWS), lambda k, i: (k, i, 0, 0)),
            pl.BlockSpec((1, 1, 3, _ROWS), lambda k, i: (k, i, 0, 0)),
            pl.BlockSpec((1, nb, HEAD_DIM), lambda k, i: (k, 0, 0)),
            pl.BlockSpec((1, HEAD_DIM, nb), lambda k, i: (k, 0, 0)),
            pl.BlockSpec((1, t, kd), lambda k, i: (k, 0, 0)),
            pl.BlockSpec((1, HEAD_DIM, t), lambda k, i: (k, 0, 0)),
            pl.BlockSpec((1, t + WINDOW, HEAD_DIM), lambda k, i: (k, 0, 0)),
            pl.BlockSpec((1, HEAD_DIM, t + WINDOW), lambda k, i: (k, 0, 0)),
        ],
        out_specs=pl.BlockSpec((1, 1, HEAD_DIM, _ROWS), lambda k, i: (k, i, 0, 0)),
        scratch_shapes=[pltpu.VMEM((WINDOW + Q_BLOCK, _ROWS), f32),
                        pltpu.SMEM((max(nb // (_TK // CMP_BLOCK), 1),), jnp.int32)],
        compiler_params=_cparams(("arbitrary", "arbitrary")),
        name="nsa_prompt",
    )(qt, gt, ckh, cvt, ke, vst, kwh, vwt)
    out = out.reshape(N_KV, nqb, HEAD_DIM, GROUP, Q_BLOCK).transpose(1, 4, 0, 3, 2)
    return out.reshape(t, NSA_DIM)


def _pair_sum(x, low):
    s_lo = jnp.sum(jnp.where(low, x, 0.0), axis=-1, keepdims=True)
    s_hi = jnp.sum(jnp.where(low, 0.0, x), axis=-1, keepdims=True)
    return jnp.where(low, s_lo, s_hi)


def _rw_prep_kernel(shift_rows, *refs):
    cur, refs = refs[:6], refs[6:]
    prv, refs = refs[:6], refs[6:]
    if shift_rows:
        st0, refs = refs[:6], refs[6:]
    mus, refs = refs[:6], refs[6:]
    (w0_ref, w2_ref, a0_ref, a2_ref, g2_ref, kk_ref, ka_ref, rk_ref,
     ro_ref, ko_ref, vo_ref, kko_ref, kao_ref, ldo_ref, go_ref, bo_ref) = refs
    first = pl.program_id(0) == 0

    def shifted(n):
        x = cur[n][...]
        if shift_rows:
            prev_row = jnp.where(first, st0[n][pl.ds(7, 1), :], prv[n][pl.ds(7, 1), :])
            rolled = pltpu.roll(x, 1, axis=0)
            rid = lax.broadcasted_iota(jnp.int32, x.shape, 0)
            xp = jnp.where(rid == 0, prev_row, rolled)
        else:
            xp = prv[n][...]
        return x + (xp - x) * mus[n][...]

    r, k, v, gl, wl, al = (shifted(n) for n in range(6))

    y = -(w0_ref[...] + jnp.dot(jnp.tanh(wl).astype(bf16), w2_ref[...], preferred_element_type=f32))
    softplus = jnp.maximum(y, 0.0) + jnp.log1p(jnp.exp(-jnp.abs(y)))
    w_log = -softplus - 0.5
    ld = -jnp.exp(w_log)
    a = jax.nn.sigmoid(a0_ref[...] + jnp.dot(al.astype(bf16), a2_ref[...], preferred_element_type=f32))
    g = jnp.dot(jax.nn.sigmoid(gl).astype(bf16), g2_ref[...], preferred_element_type=f32)
    kk = k * kk_ref[...]
    k2 = k * (1.0 + (a - 1.0) * ka_ref[...])
    rkr = r * k2 * rk_ref[...]
    low = lax.broadcasted_iota(jnp.int32, (1, 2 * RW_N), 1) < RW_N
    for hp in range(RW_HEADS // 2):
        sl = slice(hp * 2 * RW_N, (hp + 1) * 2 * RW_N)
        kkp = kk[:, sl]
        kkp = kkp / jnp.maximum(jnp.sqrt(_pair_sum(kkp * kkp, low)), 1e-12)
        ro_ref[hp] = r[:, sl]
        ko_ref[hp] = k2[:, sl]
        vo_ref[hp] = v[:, sl]
        kko_ref[hp] = kkp
        kao_ref[hp] = kkp * a[:, sl]
        ldo_ref[hp] = ld[:, sl]
        go_ref[hp] = g[:, sl]
        bo_ref[hp] = _pair_sum(rkr[:, sl], low) * v[:, sl]


def _rw_prep(z, prev, mu_p, w0, w2p, a0, a2p, g2, k_k, k_a, r_k, shift_rows):
    m = z.shape[0]
    tm = _row_tile(m, 256)
    names = ("r", "k", "v", "gl", "wl", "al")

    def col_spec(name, rows, imap):
        o, _, pw = _OFF[name]
        return pl.BlockSpec((rows, pw), functools.partial(imap, o // pw))

    cur = [col_spec(n, tm, lambda c, i: (i, c)) for n in names]
    if shift_rows:
        blk8 = tm // 8
        prv = [col_spec(n, 8, lambda c, i: (jnp.maximum(i * blk8 - 1, 0), c)) for n in names]
        prv += [col_spec(n, 8, lambda c, i: (0, c)) for n in names]
        prev_args = [z] * 6 + [prev] * 6
    else:
        prv = [col_spec(n, tm, lambda c, i: (i, c)) for n in names]
        prev_args = [prev] * 6
    mus = [col_spec(n, 1, lambda c, i: (0, c)) for n in names]
    vec = pl.BlockSpec((1, RW_DIM), lambda i: (0, 0))
    out_spec = pl.BlockSpec((RW_HEADS // 2, tm, 2 * RW_N), lambda i: (0, i, 0))
    outs = pl.pallas_call(
        functools.partial(_rw_prep_kernel, shift_rows),
        out_shape=[jax.ShapeDtypeStruct((RW_HEADS // 2, m, 2 * RW_N), f32)] * 8,
        grid=(m // tm,),
        in_specs=cur + prv + mus + [
            vec,
            pl.BlockSpec((128, RW_DIM), lambda i: (0, 0)),
            vec,
            pl.BlockSpec((128, RW_DIM), lambda i: (0, 0)),
            pl.BlockSpec((G_LORA, RW_DIM), lambda i: (0, 0)),
            vec, vec, vec,
        ],
        out_specs=[out_spec] * 8,
        compiler_params=_cparams(("parallel",)),
        name="rw_prep",
    )(*([z] * 6), *prev_args, *([mu_p] * 6),
      w0.reshape(1, RW_DIM), w2p, a0.reshape(1, RW_DIM), a2p, g2,
      k_k.reshape(1, RW_DIM), k_a.reshape(1, RW_DIM), r_k.reshape(1, RW_DIM))
    return outs


_CH = 64
_HEADS_PER_STEP = 16
_NN = (((1,), (0,)), ((), ()))
_NT = (((1,), (1,)), ((), ()))
_TN = (((0,), (0,)), ((), ()))


def _split2(x):
    hi = x.astype(bf16)
    lo = (x - hi.astype(f32)).astype(bf16)
    return hi, lo


def _dot3(a, b, dims=_NN):
    ah, al = _split2(a)
    bh, bl = _split2(b)
    d = functools.partial(lax.dot_general, dimension_numbers=dims, preferred_element_type=f32)
    return d(ah, bh) + (d(ah, bl) + d(al, bh))


def _dot_exact_rhs(a, b, dims=_NN, split_lhs=False):
    x = a if split_lhs else b
    x1 = x.astype(bf16)
    r1 = x - x1.astype(f32)
    x2 = r1.astype(bf16)
    x3 = (r1 - x2.astype(f32)).astype(bf16)
    d = functools.partial(lax.dot_general, dimension_numbers=dims, preferred_element_type=f32)
    if split_lhs:
        return d(x1, b) + (d(x2, b) + d(x3, b))
    return d(a, x1) + (d(a, x2) + d(a, x3))


def _wkv_chunk_kernel(r_ref, k_ref, v_ref, kk_ref, ka_ref, ld_ref, s0_ref, y_ref, sT_ref, st_ref):
    c = pl.program_id(0)

    @pl.when(c == 0)
    def _():
        st_ref[...] = s0_ref[...]

    ti = lax.broadcasted_iota(jnp.int32, (_CH, _CH), 0)
    si = lax.broadcasted_iota(jnp.int32, (_CH, _CH), 1)
    incl = ti >= si
    strict = ti > si
    l_incl = incl.astype(bf16)
    eye = (ti == si).astype(f32)
    ones = jnp.ones((_CH, RW_N), bf16)

    def each(f, *lists):
        return [f(*a) for a in zip(*lists)]

    def cat0(a, b):
        return jnp.concatenate([a, b], axis=0)

    def cat1(a, b):
        return jnp.concatenate([a, b], axis=1)

    def heads_step(heads):
        def load(ref):
            return [ref[hp, :, sub * RW_N:(sub + 1) * RW_N] for hp, sub in heads]

        r, k, v, kk, ka, ld = (load(ref) for ref in (r_ref, k_ref, v_ref, kk_ref, ka_ref, ld_ref))
        lp = each(lambda x: _dot_exact_rhs(l_incl, x), ld)
        lp_end = each(lambda x: _dot_exact_rhs(x, ones, _TN, split_lhs=True), ld)
        e_neg = each(lambda x: jnp.exp(-x), lp)
        at = each(lambda a, x, y: -a * jnp.exp(x - y), kk, lp, ld)
        bt = each(jnp.multiply, ka, e_neg)
        kt = each(jnp.multiply, k, e_neg)
        rt = each(lambda a, x: a * jnp.exp(x), r, lp)
        e_end = each(lambda x: jnp.exp(x[_CH - 1:_CH, :] - x), lp)
        bh = each(jnp.multiply, ka, e_end)
        kh = each(jnp.multiply, k, e_end)
        sc = each(lambda a, b: _dot3(a, b, _NT), each(cat0, at, rt), each(cat0, bt, kt))
        a_b = each(lambda x: jnp.where(strict, x[:_CH, :_CH], 0.0), sc)
        a_k = each(lambda x: jnp.where(strict, x[:_CH, _CH:], 0.0), sc)
        g_b = each(lambda x: jnp.where(incl, x[_CH:, :_CH], 0.0), sc)
        g_k = each(lambda x: jnp.where(incl, x[_CH:, _CH:], 0.0), sc)
        tm = each(lambda x: eye + x, a_b)
        pw = a_b
        for _ in range(5):
            pw = each(_dot3, pw, pw)
            tm = each(lambda t_, p_: t_ + _dot3(t_, p_), tm, pw)
        akv = each(_dot3, a_k, v)
        tx = each(_dot3, tm, each(cat1, at, akv))
        st = [st_ref[2 * hp + sub] for hp, sub in heads]
        ws = each(_dot3, each(lambda x, y: cat0(x[:, :RW_N], y), tx, rt), st)
        u = each(lambda a, x: a[:_CH] + x[:, RW_N:], ws, tx)
        uv = each(cat0, u, v)
        y = each(lambda a, gb, gk, x: a[_CH:] + _dot3(cat1(gb, gk), x), ws, g_b, g_k, uv)
        st_new = each(lambda e, s_, b_, k_, x: jnp.exp(e) * s_ + _dot3(cat0(b_, k_), x, _TN),
                      lp_end, st, bh, kh, uv)
        for (hp, sub), s_ in zip(heads, st_new):
            st_ref[2 * hp + sub] = s_
        return y

    def pairs(i, carry):
        heads = [(_HEADS_PER_STEP // 2 * i + a, sub) for a in range(_HEADS_PER_STEP // 2) for sub in range(2)]
        y = heads_step(heads)
        for a in range(_HEADS_PER_STEP // 2):
            y_ref[_HEADS_PER_STEP // 2 * i + a] = cat1(y[2 * a], y[2 * a + 1])
        return carry

    lax.fori_loop(0, RW_HEADS // _HEADS_PER_STEP, pairs, 0)

    @pl.when(c == pl.num_programs(0) - 1)
    def _():
        sT_ref[...] = st_ref[...]


def _wkv_chunks(r, k, v, kk, ka, ld, s0t):
    t = r.shape[1]
    spec = pl.BlockSpec((RW_HEADS // 2, _CH, 2 * RW_N), lambda c: (0, c, 0))
    sspec = pl.BlockSpec((RW_HEADS, RW_N, RW_N), lambda c: (0, 0, 0))
    return pl.pallas_call(
        _wkv_chunk_kernel,
        out_shape=[jax.ShapeDtypeStruct((RW_HEADS // 2, t, 2 * RW_N), f32),
                   jax.ShapeDtypeStruct((RW_HEADS, RW_N, RW_N), f32)],
        grid=(t // _CH,),
        in_specs=[spec] * 6 + [sspec],
        out_specs=[spec, sspec],
        scratch_shapes=[pltpu.VMEM((RW_HEADS, RW_N, RW_N), f32)],
        compiler_params=_cparams(("arbitrary",)),
        name="wkv_chunks",
    )(r, k, v, kk, ka, ld, s0t)


def _rw_post_kernel(y_ref, b_ref, g_ref, lw_ref, lb_ref, o_ref):
    low = lax.broadcasted_iota(jnp.int32, (1, 2 * RW_N), 1) < RW_N
    for hp in range(RW_HEADS // 2):
        y = y_ref[hp]
        mu = _pair_sum(y, low) / RW_N
        var = _pair_sum(jnp.square(y - mu), low) / RW_N
        yn = (y - mu) * lax.rsqrt(var + GN_EPS) * lw_ref[hp] + lb_ref[hp]
        o_ref[:, hp * 2 * RW_N:(hp + 1) * 2 * RW_N] = ((yn + b_ref[hp]) * g_ref[hp]).astype(o_ref.dtype)


def _rw_post(y, bonus, g, ln_w, ln_b):
    m = y.shape[1]
    tm = _row_tile(m, 512)
    spec = pl.BlockSpec((RW_HEADS // 2, tm, 2 * RW_N), lambda i: (0, i, 0))
    pspec = pl.BlockSpec((RW_HEADS // 2, 1, 2 * RW_N), lambda i: (0, 0, 0))
    return pl.pallas_call(
        _rw_post_kernel,
        out_shape=jax.ShapeDtypeStruct((m, RW_DIM), bf16),
        grid=(m // tm,),
        in_specs=[spec, spec, spec, pspec, pspec],
        out_specs=pl.BlockSpec((tm, RW_DIM), lambda i: (i, 0)),
        compiler_params=_cparams(("parallel",)),
        name="rw_post",
    )(y, bonus, g, ln_w.reshape(RW_HEADS // 2, 1, 2 * RW_N), ln_b.reshape(RW_HEADS // 2, 1, 2 * RW_N))


def _lora_pad(w):
    return jnp.pad(w, ((0, 128 - w.shape[0]), (0, 0))).astype(bf16)


def _rwkv_prompt(z, shift0_p, s0t, p):
    r, k, v, kk, ka, ld, g, bonus = _rw_prep(
        z, shift0_p, p["mu_p"], p["rw_w0"], p["w2p"], p["rw_a0"], p["a2p"], p["g2"],
        p["rw_k_k"], p["rw_k_a"], p["rw_r_k"], True)
    y, st = _wkv_chunks(r, k, v, kk, ka, ld, s0t)
    return _rw_post(y, bonus, g, p["rw_ln_w"], p["rw_ln_b"]), st


def _wkv_step_kernel(s_ref, r_ref, k_ref, vc_ref, kk_ref, ka_ref, ld_ref, y_ref, so_ref):
    s = s_ref[0]
    kk = kk_ref[0][:, None, :]
    sa = jnp.sum(s * (-kk), axis=-1, keepdims=True)
    s = (s * jnp.exp(ld_ref[0])[:, None, :] + sa * ka_ref[0][:, None, :]
         + vc_ref[0] * k_ref[0][:, None, :])
    so_ref[0] = s
    y_ref[0] = jnp.sum(s * r_ref[0][:, None, :], axis=-1, keepdims=True)


def _wkv_step(s0, r, k, v, kk, ka, ld):
    b = s0.shape[0]
    sspec = pl.BlockSpec((1, RW_HEADS, RW_N, RW_N), lambda i: (i, 0, 0, 0))
    vspec = pl.BlockSpec((1, RW_HEADS, RW_N), lambda i: (i, 0, 0))
    cspec = pl.BlockSpec((1, RW_HEADS, RW_N, 1), lambda i: (i, 0, 0, 0))
    y, s1 = pl.pallas_call(
        _wkv_step_kernel,
        out_shape=[jax.ShapeDtypeStruct((b, RW_HEADS, RW_N, 1), f32),
                   jax.ShapeDtypeStruct(s0.shape, f32)],
        grid=(b,),
        in_specs=[sspec, vspec, vspec, cspec, vspec, vspec, vspec],
        out_specs=[cspec, sspec],
        compiler_params=_cparams(("parallel",)),
        name="wkv_step",
    )(s0, r, k, v[..., None], kk, ka, ld)
    return y[..., 0], s1


_GP = 8


_PAGE_ROWS = N_KV * HEAD_DIM


def _compress_pages_kernel(npg, pt_ref, pool_ref, pe_ref, w_ref, o_ref, buf0, buf1, sem):
    g = pl.program_id(0)
    n_steps = pl.num_programs(0)

    def page_copies(seq, buf, slot):
        return [pltpu.make_async_copy(pool_ref.at[pt_ref[seq, pg]],
                                      buf.at[pl.ds(pg * _PAGE_ROWS, _PAGE_ROWS)], sem.at[slot])
                for pg in range(npg)]

    def start(seq, buf, slot):
        for cp in page_copies(seq, buf, slot):
            cp.start()

    def wait(seq, buf, slot):
        for cp in page_copies(seq, buf, slot):
            cp.wait()

    def compress(buf, out_slot):
        acc = jnp.zeros((npg * N_KV, 2 * CMP_BLOCK), f32)
        for d in range(HEAD_DIM):
            x = buf[pl.ds(d, npg * N_KV, stride=HEAD_DIM), :] + pe_ref[pl.ds(d, 1), :]
            acc = acc + jnp.dot(x.astype(bf16), w_ref[d], preferred_element_type=f32)
        o_ref[out_slot] = acc

    @pl.when(g == 0)
    def _():
        start(0, buf0, 0)

    start(2 * g + 1, buf1, 1)
    wait(2 * g, buf0, 0)
    compress(buf0, 0)

    @pl.when(g + 1 < n_steps)
    def _():
        start(2 * g + 2, buf0, 0)

    wait(2 * g + 1, buf1, 1)
    compress(buf1, 1)


def _compress_pages(pool_t, table, pe, w):
    b, npg = table.shape
    page = pool_t.shape[-1]
    assert b % 2 == 0 and page == 2 * CMP_BLOCK
    pe_t = jnp.tile(pe.T, (1, 2))
    eye = jnp.eye(2, dtype=f32)
    wd = jnp.einsum("ab,jde->dajbe", eye, w).reshape(HEAD_DIM, page, page).astype(bf16)
    out = pl.pallas_call(
        functools.partial(_compress_pages_kernel, npg),
        out_shape=jax.ShapeDtypeStruct((b, npg * N_KV, page), f32),
        grid_spec=pltpu.PrefetchScalarGridSpec(
            num_scalar_prefetch=1,
            grid=(b // 2,),
            in_specs=[
                pl.BlockSpec(memory_space=pl.ANY),
                pl.BlockSpec((HEAD_DIM, page), lambda i, pt: (0, 0)),
                pl.BlockSpec((HEAD_DIM, page, page), lambda i, pt: (0, 0, 0)),
            ],
            out_specs=pl.BlockSpec((2, npg * N_KV, page), lambda i, pt: (i, 0, 0)),
            scratch_shapes=[pltpu.VMEM((npg * _PAGE_ROWS, page), f32),
                            pltpu.VMEM((npg * _PAGE_ROWS, page), f32),
                            pltpu.SemaphoreType.DMA((2,))]),
        compiler_params=_cparams(("arbitrary",)),
        name="compress_pages",
    )(table, pool_t, pe_t, wd)
    out = out.reshape(b, npg, N_KV, 2, HEAD_DIM).transpose(0, 1, 3, 2, 4)
    return out.reshape(b, 2 * npg, N_KV, HEAD_DIM)


def _dec_cmp_kernel(n_pick, t_pos, q_ref, ckt_ref, cv_ref, xk_ref, xv_ref, wk_ref, wv_ref, gkc_ref,
                    oc_ref, idx_ref):
    nb = ckt_ref.shape[-1]
    blk = lax.broadcasted_iota(jnp.int32, (1, nb), 1)
    blk_mid = (blk * CMP_BLOCK).astype(f32) + (CMP_BLOCK - 1) / 2
    ck_new = _rms(jnp.dot(xk_ref[0].astype(bf16), wk_ref[...], preferred_element_type=f32), gkc_ref[...])
    cv_new = jnp.dot(xv_ref[0].astype(bf16), wv_ref[...], preferred_element_type=f32)
    new_mid = float(nb * CMP_BLOCK) + (CMP_BLOCK - 1) / 2
    new_ok = nb * CMP_BLOCK + (CMP_BLOCK - 1) <= t_pos
    lane = lax.broadcasted_iota(jnp.int32, (1, 128), 1)
    imps = []
    for kv in range(N_KV):
        q = q_ref[0, kv]
        g1 = lax.broadcasted_iota(jnp.int32, (_GP, 1), 0) + (kv * GROUP + 1)
        slope = jnp.exp2(-0.5 * g1.astype(f32))
        s = jnp.dot(q.astype(bf16), ckt_ref[0, kv], preferred_element_type=f32)
        s = s - slope * (float(t_pos) - blk_mid)
        s = jnp.where(blk * CMP_BLOCK + (CMP_BLOCK - 1) <= t_pos, s, -jnp.inf)
        qn = q.astype(bf16).astype(f32)
        s_new = jnp.sum(qn * ck_new[kv:kv + 1].astype(bf16).astype(f32), axis=-1, keepdims=True)
        s_new = s_new - slope * (float(t_pos) - new_mid)
        s_new = jnp.where(new_ok, s_new, -jnp.inf)
        m = jnp.maximum(jnp.max(s, axis=-1, keepdims=True), s_new)
        m = jnp.where(jnp.isfinite(m), m, 0.0)
        p = jnp.exp(s - m)
        p_new = jnp.exp(s_new - m)
        den = jnp.maximum(jnp.sum(p, axis=-1, keepdims=True) + p_new, 1e-30)
        p = p / den
        p_new = p_new / den
        oc = jnp.dot(p.astype(bf16), cv_ref[0, kv], preferred_element_type=f32)
        oc_ref[0, kv] = oc + p_new * cv_new[kv:kv + 1]
        imps.append(jnp.sum(p[:GROUP], axis=0, keepdims=True))
    imp = jnp.concatenate(imps, axis=0)
    cur = t_pos // CMP_BLOCK
    forced = (blk == 0) | (blk == cur) | (blk == cur - 1)
    v = jnp.where(forced, FORCED_SCORE, jnp.where(blk * CMP_BLOCK <= t_pos, imp, -1.0))
    out = jnp.zeros((N_KV, 128), jnp.int32)
    for it in range(n_pick):
        mx = jnp.max(v, axis=-1, keepdims=True)
        idx = jnp.min(jnp.where(v == mx, blk, nb), axis=-1, keepdims=True)
        out = jnp.where(lane == it, idx, out)
        v = jnp.where(blk == idx, -jnp.inf, v)
    idx_ref[0] = out


def _dec_cmp(q, ck, cv, xk, xv, wk, wv, g_kc, t_pos, n_pick):
    b, nb = ck.shape[:2]
    ckt = ck.astype(bf16).transpose(0, 2, 3, 1)
    cvh = cv.astype(bf16).transpose(0, 2, 1, 3)
    kd = CMP_BLOCK * HEAD_DIM
    return pl.pallas_call(
        functools.partial(_dec_cmp_kernel, n_pick, t_pos),
        out_shape=[jax.ShapeDtypeStruct((b, N_KV, _GP, HEAD_DIM), f32),
                   jax.ShapeDtypeStruct((b, N_KV, 128), jnp.int32)],
        grid=(b,),
        in_specs=[
            pl.BlockSpec((1, N_KV, _GP, HEAD_DIM), lambda i: (i, 0, 0, 0)),
            pl.BlockSpec((1, N_KV, HEAD_DIM, nb), lambda i: (i, 0, 0, 0)),
            pl.BlockSpec((1, N_KV, nb, HEAD_DIM), lambda i: (i, 0, 0, 0)),
            pl.BlockSpec((1, _GP, kd), lambda i: (i, 0, 0)),
            pl.BlockSpec((1, _GP, kd), lambda i: (i, 0, 0)),
            pl.BlockSpec((kd, HEAD_DIM), lambda i: (0, 0)),
            pl.BlockSpec((kd, HEAD_DIM), lambda i: (0, 0)),
            pl.BlockSpec((1, HEAD_DIM), lambda i: (0, 0)),
        ],
        out_specs=[pl.BlockSpec((1, N_KV, _GP, HEAD_DIM), lambda i: (i, 0, 0, 0)),
                   pl.BlockSpec((1, N_KV, 128), lambda i: (i, 0, 0))],
        compiler_params=_cparams(("parallel",)),
        name="dec_cmp",
    )(q, ckt, cvh, xk, xv, wk, wv, g_kc.reshape(1, HEAD_DIM))


def _dec_sel_kernel(t_pos, n_pick, pt_ref, idx_ref, q_ref, gt_ref, oc_ref, ksn_ref, vsn_ref, kwn_ref,
                    vwn_ref, wk_ref, wv_ref, pk_ref, pv_ref, o_ref, kb0, vb0, kb1, vb1, sem):
    g = pl.program_id(0)
    n_steps = pl.num_programs(0)
    n_buf = wk_ref.shape[-1]
    page = 2 * CMP_BLOCK

    def copies(seq, kb, vb, slot):
        out = []
        for kv in range(N_KV):
            for s in range(n_pick):
                pg = pt_ref[seq, idx_ref[seq, kv * 128 + s] // 2]
                dst = pl.ds(s * page, page)
                out.append(pltpu.make_async_copy(pk_ref.at[pg, kv], kb.at[kv, :, dst], sem.at[slot]))
                out.append(pltpu.make_async_copy(pv_ref.at[pg, kv], vb.at[kv, :, dst], sem.at[slot]))
        return out

    def start(seq, kb, vb, slot):
        for cp in copies(seq, kb, vb, slot):
            cp.start()

    def wait(seq, kb, vb, slot):
        for cp in copies(seq, kb, vb, slot):
            cp.wait()

    def slopes(kv):
        g1 = lax.broadcasted_iota(jnp.int32, (_GP, 1), 0) + (kv * GROUP + 1)
        return jnp.exp2(-0.5 * g1.astype(f32))

    lane = lax.broadcasted_iota(jnp.int32, (1, page), 1)
    c = lax.broadcasted_iota(jnp.int32, (1, n_buf), 1)
    kpos = t_pos - n_buf + c
    distw = t_pos - kpos
    okw = (kpos >= 0) & (distw >= 0) & (distw < WINDOW)

    def attend(seq, r, kb, vb):
        for kv in range(N_KV):
            q = q_ref[r, kv].astype(bf16)
            qf = q.astype(f32)
            slope = slopes(kv)
            dist, ok = [], []
            for s in range(n_pick):
                blk = idx_ref[seq, kv * 128 + s]
                d = t_pos - ((blk // 2) * page + lane)
                dist.append(d)
                ok.append((lane // CMP_BLOCK == blk % 2) & (d >= 0))
            dist = jnp.concatenate(dist, axis=1)
            ok = jnp.concatenate(ok, axis=1)
            s_sel = jnp.dot(q, kb[kv].astype(bf16), preferred_element_type=f32)
            s_sel = jnp.where(ok, s_sel - slope * dist.astype(f32), -jnp.inf)
            s_new = jnp.sum(qf * ksn_ref[r, kv:kv + 1].astype(bf16).astype(f32), axis=-1, keepdims=True)
            m = jnp.maximum(jnp.max(s_sel, axis=-1, keepdims=True), s_new)
            p = jnp.exp(s_sel - m)
            p_new = jnp.exp(s_new - m)
            den = jnp.maximum(jnp.sum(p, axis=-1, keepdims=True) + p_new, 1e-30)
            o_s = lax.dot_general((p / den).astype(bf16), vb[kv].astype(bf16), _NT, preferred_element_type=f32)
            o_s = o_s + (p_new / den).astype(bf16).astype(f32) * vsn_ref[r, kv:kv + 1].astype(bf16).astype(f32)
            sw = jnp.dot(q, wk_ref[r, kv].astype(bf16), preferred_element_type=f32)
            sw = jnp.where(okw, sw - slope * distw.astype(f32), -jnp.inf)
            sw_new = jnp.sum(qf * kwn_ref[r, kv:kv + 1].astype(bf16).astype(f32), axis=-1, keepdims=True)
            mw = jnp.maximum(jnp.max(sw, axis=-1, keepdims=True), sw_new)
            pw = jnp.exp(sw - mw)
            pw_new = jnp.exp(sw_new - mw)
            denw = jnp.maximum(jnp.sum(pw, axis=-1, keepdims=True) + pw_new, 1e-30)
            o_w = lax.dot_general((pw / denw).astype(bf16), wv_ref[r, kv].astype(bf16), _NT,
                                  preferred_element_type=f32)
            o_w = o_w + (pw_new / denw).astype(bf16).astype(f32) * vwn_ref[r, kv:kv + 1].astype(bf16).astype(f32)
            gt = gt_ref[r, kv]
            o_ref[r, kv] = gt[:, 0:1] * oc_ref[r, kv] + gt[:, 1:2] * o_s + gt[:, 2:3] * o_w

    @pl.when(g == 0)
    def _():
        start(0, kb0, vb0, 0)

    start(2 * g + 1, kb1, vb1, 1)
    wait(2 * g, kb0, vb0, 0)
    attend(2 * g, 0, kb0, vb0)

    @pl.when(g + 1 < n_steps)
    def _():
        start(2 * g + 2, kb0, vb0, 0)

    wait(2 * g + 1, kb1, vb1, 1)
    attend(2 * g + 1, 1, kb1, vb1)


def _dec_sel(q, gates, o_c, idx, table, pool_k, pool_v, ks_new, vs_new, kw_new, vw_new, win_k, win_v,
             t_pos, n_pick):
    b = q.shape[0]
    assert b % 2 == 0
    n_buf = win_k.shape[-1]
    page = pool_k.shape[-1]
    hspec = pl.BlockSpec((2, N_KV, _GP, HEAD_DIM), lambda i, pt, ix: (i, 0, 0, 0))
    nspec = pl.BlockSpec((2, N_KV, HEAD_DIM), lambda i, pt, ix: (i, 0, 0))
    wspec = pl.BlockSpec((2, N_KV, HEAD_DIM, n_buf), lambda i, pt, ix: (i, 0, 0, 0))
    anyspec = pl.BlockSpec(memory_space=pl.ANY)
    gbuf = pltpu.VMEM((N_KV, HEAD_DIM, n_pick * page), f32)
    return pl.pallas_call(
        functools.partial(_dec_sel_kernel, t_pos, n_pick),
        out_shape=jax.ShapeDtypeStruct((b, N_KV, _GP, HEAD_DIM), f32),
        grid_spec=pltpu.PrefetchScalarGridSpec(
            num_scalar_prefetch=2,
            grid=(b // 2,),
            in_specs=[hspec, pl.BlockSpec((2, N_KV, _GP, 3), lambda i, pt, ix: (i, 0, 0, 0)), hspec,
                      nspec, nspec, nspec, nspec, wspec, wspec, anyspec, anyspec],
            out_specs=hspec,
            scratch_shapes=[gbuf, gbuf, gbuf, gbuf, pltpu.SemaphoreType.DMA((2,))]),
        compiler_params=_cparams(("arbitrary",)),
        name="dec_sel",
    )(table, idx, q, gates, o_c, ks_new, vs_new, kw_new, vw_new, win_k, win_v, pool_k, pool_v)


def _head_rms(x, g):
    ms = jnp.mean(x * x, axis=-1, keepdims=True)
    return x * lax.rsqrt(ms + NORM_EPS) * g


def _prepare(p):
    q = dict(p)
    for n in ("ffn1_gate", "ffn1_up", "ffn1_down", "ffn2_gate", "ffn2_up", "ffn2_down", "w_pa", "w_pb", "w_out"):
        q[n] = p[n].astype(bf16)
    q["w_in_p"] = _pad_cols(p["w_in"]).astype(bf16)
    q["mu_p"] = _pad_rw_cols(p["rw_mu"][None])
    q["w2p"] = _lora_pad(p["rw_w2"])
    q["a2p"] = _lora_pad(p["rw_a2"])
    q["g2"] = p["rw_g2"].astype(bf16)
    return q


def _nsa_proj(z, p):
    m = z.shape[0]
    kvs = (m, N_KV, HEAD_DIM)
    q = _head_rms(_seg(z, "q").reshape(m, N_HEADS, HEAD_DIM), p["g_q"])
    kc = _seg(z, "kc").reshape(kvs)
    vc = _seg(z, "vc").reshape(kvs)
    ks = _head_rms(_seg(z, "ks").reshape(kvs), p["g_ks"])
    vs = _seg(z, "vs").reshape(kvs)
    kw = _head_rms(_seg(z, "kw").reshape(kvs), p["g_kw"])
    vw = _seg(z, "vw").reshape(kvs)
    gates = jax.nn.sigmoid(_seg(z, "ga").reshape(m, N_HEADS, 3))
    return q, gates, kc, vc, ks, vs, kw, vw


def _layer_prompt(x, p):
    t = x.shape[0]
    x = _ffn(x, p["n_ffn1"], p["ffn1_gate"], p["ffn1_up"], p["ffn1_down"])
    z = _inproj(x, p["n_mix"], p["w_in_p"])
    q, gates, kc, vc, ks, vs, kw, vw = _nsa_proj(z, p)
    ck = _head_rms(_compress(kc.reshape(1, t, KV_DIM), p["pe_cmp_k"], p["w_cmp_k"])[0]
                   .reshape(-1, N_KV, HEAD_DIM), p["g_kc"])
    cv = _compress(vc.reshape(1, t, KV_DIM), p["pe_cmp_v"], p["w_cmp_v"])[0].reshape(-1, N_KV, HEAD_DIM)
    o_a = _nsa_prompt(q, gates, ck, cv, ks, vs, kw, vw).astype(bf16)
    shift0 = jnp.zeros((8, _Z_COLS), f32)
    s0t = jnp.zeros((RW_HEADS, RW_N, RW_N), f32)
    o_b, st = _rwkv_prompt(z, shift0, s0t, p)
    x = _merge(o_a, o_b, z, x, p["w_pa"], p["w_pb"], p["w_out"])
    x = _ffn(x, p["n_ffn2"], p["ffn2_gate"], p["ffn2_up"], p["ffn2_down"])
    n_buf = min(WINDOW, t)
    states = (kc, vc, ks, vs, kw[-n_buf:], vw[-n_buf:], _rw_cols(z[-1:]), st.transpose(0, 2, 1)[None])
    return x, states


def _layer_sample(x, p, past):
    b = x.shape[0]
    table = past["page_table"]
    page = past["cmp_k"].shape[1]
    t_pos = table.shape[1] * page
    x = _ffn(x, p["n_ffn1"], p["ffn1_gate"], p["ffn1_up"], p["ffn1_down"])
    z = _inproj(x, p["n_mix"], p["w_in_p"])
    q, gates, kc, vc, ks, vs, kw, vw = _nsa_proj(z, p)

    fm = lambda a: a.transpose(0, 2, 3, 1)
    pool_rows = lambda a: fm(a).reshape(-1, _PAGE_ROWS, page)

    nbp = t_pos // CMP_BLOCK
    ck = _head_rms(_compress_pages(pool_rows(past["cmp_k"]), table, p["pe_cmp_k"], p["w_cmp_k"]), p["g_kc"])
    cv = _compress_pages(pool_rows(past["cmp_v"]), table, p["pe_cmp_v"], p["w_cmp_v"])

    def new_block_rows(k_new, pe):
        first = k_new + pe[0]
        rest = jnp.broadcast_to(pe[1:].reshape(1, 1, -1), (b, N_KV, (CMP_BLOCK - 1) * HEAD_DIM))
        rows = jnp.concatenate([first, rest], axis=-1)
        return jnp.pad(rows, ((0, 0), (0, _GP - N_KV), (0, 0)))

    n_sel = min(N_SEL, nbp + 1)
    n_pick = n_sel - 1
    scale = HEAD_DIM ** -0.5
    pad_g = lambda a: jnp.pad(a, ((0, 0), (0, 0), (0, _GP - GROUP), (0, 0)))
    qh = pad_g((q * scale).reshape(b, N_KV, GROUP, HEAD_DIM))
    gth = pad_g(gates.reshape(b, N_KV, GROUP, 3))
    kd = CMP_BLOCK * HEAD_DIM
    o_c, idx = _dec_cmp(qh, ck, cv, new_block_rows(kc, p["pe_cmp_k"]), new_block_rows(vc, p["pe_cmp_v"]),
                        p["w_cmp_k"].reshape(kd, HEAD_DIM).astype(bf16),
                        p["w_cmp_v"].reshape(kd, HEAD_DIM).astype(bf16), p["g_kc"], t_pos, n_pick)
    o_a = _dec_sel(qh, gth, o_c, idx.reshape(b, N_KV * 128), table, fm(past["slc_k"]), fm(past["slc_v"]),
                   ks, vs, kw, vw, fm(past["win_k"]), fm(past["win_v"]), t_pos, n_pick)
    o_a = o_a[:, :, :GROUP].reshape(b, NSA_DIM).astype(bf16)

    prev = _pad_rw_cols(past["shift"])
    r, k, v, kk, ka, ld, g, bonus = _rw_prep(
        z, prev, p["mu_p"], p["rw_w0"], p["w2p"], p["rw_a0"], p["a2p"], p["g2"],
        p["rw_k_k"], p["rw_k_a"], p["rw_r_k"], False)
    tb = lambda a: a.transpose(1, 0, 2).reshape(b, RW_HEADS, RW_N)
    y, wkv = _wkv_step(past["wkv"], tb(r), tb(k), tb(v), tb(kk), tb(ka), tb(ld))
    y = y.reshape(b, RW_HEADS // 2, 2 * RW_N).transpose(1, 0, 2)
    o_b = _rw_post(y, bonus, g, p["rw_ln_w"], p["rw_ln_b"])

    x = _merge(o_a, o_b, z, x, p["w_pa"], p["w_pb"], p["w_out"])
    x = _ffn(x, p["n_ffn2"], p["ffn2_gate"], p["ffn2_up"], p["ffn2_down"])
    kvs = lambda a: a.reshape(b, 1, N_KV, HEAD_DIM)
    win_k = jnp.concatenate([past["win_k"][:, 1:], kvs(kw)], axis=1)
    win_v = jnp.concatenate([past["win_v"][:, 1:], kvs(vw)], axis=1)
    states = (kvs(kc), kvs(vc), kvs(ks), kvs(vs), win_k, win_v, _rw_cols(z), wkv)
    return x, states


def kernel(x_prompt, x_sample, cache_cmp_k, cache_cmp_v, cache_slc_k, cache_slc_v, cache_win_k, cache_win_v,
           state_shift, state_wkv, page_table,
           n_ffn1, ffn1_gate, ffn1_up, ffn1_down, n_mix, w_in, g_q, g_kc, g_ks, g_kw,
           w_cmp_k, pe_cmp_k, w_cmp_v, pe_cmp_v,
           rw_mu, rw_w0, rw_w2, rw_a0, rw_a2, rw_g2, rw_k_k, rw_k_a, rw_r_k, rw_ln_w, rw_ln_b,
           w_pa, w_pb, w_out, n_ffn2, ffn2_gate, ffn2_up, ffn2_down):
    assert x_prompt.shape[0] == 1 and x_sample.shape[1] == 1 and n_ffn1.shape[0] == 1
    l = 0
    p = _prepare(dict(
        n_ffn1=n_ffn1[l], ffn1_gate=ffn1_gate[l], ffn1_up=ffn1_up[l], ffn1_down=ffn1_down[l],
        n_mix=n_mix[l], w_in=w_in[l], g_q=g_q[l], g_kc=g_kc[l], g_ks=g_ks[l], g_kw=g_kw[l],
        w_cmp_k=w_cmp_k[l], pe_cmp_k=pe_cmp_k[l], w_cmp_v=w_cmp_v[l], pe_cmp_v=pe_cmp_v[l],
        rw_mu=rw_mu[l], rw_w0=rw_w0[l], rw_w2=rw_w2[l], rw_a0=rw_a0[l], rw_a2=rw_a2[l], rw_g2=rw_g2[l],
        rw_k_k=rw_k_k[l], rw_k_a=rw_k_a[l], rw_r_k=rw_r_k[l], rw_ln_w=rw_ln_w[l], rw_ln_b=rw_ln_b[l],
        w_pa=w_pa[l], w_pb=w_pb[l], w_out=w_out[l],
        n_ffn2=n_ffn2[l], ffn2_gate=ffn2_gate[l], ffn2_up=ffn2_up[l], ffn2_down=ffn2_down[l]))
    t = x_prompt.shape[1]
    y_p, sp = _layer_prompt(x_prompt[0], p)
    past = dict(page_table=page_table, cmp_k=cache_cmp_k[l], cmp_v=cache_cmp_v[l], slc_k=cache_slc_k[l],
                slc_v=cache_slc_v[l], win_k=cache_win_k[l], win_v=cache_win_v[l], shift=state_shift[l],
                wkv=state_wkv[l])
    y_s, ss = _layer_sample(x_sample[:, 0], p, past)
    kvp = lambda a: a.reshape(1, 1, -1, N_KV, HEAD_DIM)
    outs_p = (kvp(sp[0]), kvp(sp[1]), kvp(sp[2]), kvp(sp[3]), kvp(sp[4]), kvp(sp[5]), sp[6][None], sp[7][None])
    outs_s = tuple(a[None] for a in ss)
    return (y_p.reshape(1, t, D_MODEL), y_s[:, None, :]) + outs_p + outs_s
```

```python
import functools

import jax
import jax.numpy as jnp
from jax import lax
from jax.experimental import pallas as pl
from jax.experimental.pallas import tpu as pltpu

f32 = jnp.float32
bf16 = jnp.bfloat16

D_MODEL = 2048
N_HEADS = 16
N_KV = 4
GROUP = 4
HEAD_DIM = 64
NSA_DIM = N_HEADS * HEAD_DIM
KV_DIM = N_KV * HEAD_DIM
CMP_BLOCK = 64
N_SEL = 16
WINDOW = 512
Q_BLOCK = 128
FORCED_SCORE = 1e3
RW_HEADS = 16
RW_N = 64
RW_DIM = RW_HEADS * RW_N
W_LORA = 96
A_LORA = 96
G_LORA = 256
D_FF = 5632
NORM_EPS = 1e-6
GN_EPS = 64e-5
NEG_BIG = -1e30

VMEM_LIMIT = 56 * 1024 * 1024

_SEGS = (
    ("r", 2608, 1024, 1024),
    ("k", 3728, 1024, 1024),
    ("v", 4752, 1024, 1024),
    ("q", 0, 1024, 1024),
    ("g_a", 6128, 2048, 2048),
    ("g_b", 8176, 2048, 2048),
    ("kc", 1024, 256, 256),
    ("vc", 1280, 256, 256),
    ("ks", 1536, 256, 256),
    ("vs", 1792, 256, 256),
    ("kw", 2048, 256, 256),
    ("vw", 2304, 256, 256),
    ("gl", 5872, 256, 256),
    ("ga", 2560, 48, 128),
    ("wl", 3632, 96, 128),
    ("al", 5776, 96, 128),
)
_Z_COLS = 10752


def _seg_offsets():
    offs, o = {}, 0
    for name, _, w, pw in _SEGS:
        assert o % pw == 0
        offs[name] = (o, w, pw)
        o += pw
    return offs, o


_OFF, _USED = _seg_offsets()


_IN_COLS = 10224
_RW_START = 2608
_RW_COLS = 3520


def _pad_cols(x):
    parts = []
    for _, s, w, pw in _SEGS:
        seg = x[..., s:s + w]
        if pw != w:
            seg = jnp.pad(seg, [(0, 0)] * (x.ndim - 1) + [(0, pw - w)])
        parts.append(seg)
    parts.append(jnp.zeros(x.shape[:-1] + (_Z_COLS - _USED,), x.dtype))
    return jnp.concatenate(parts, axis=-1)


def _pad_rw_cols(x):
    pad = [(0, 0)] * (x.ndim - 1) + [(_RW_START, _IN_COLS - _RW_START - _RW_COLS)]
    return _pad_cols(jnp.pad(x, pad))


def _seg(z, name):
    o, w, _ = _OFF[name]
    return z[..., o:o + w]


def _rw_cols(z):
    return jnp.concatenate([_seg(z, n) for n in ("r", "wl", "k", "v", "al", "gl")], axis=-1)


def _cparams(sem, vmem=VMEM_LIMIT):
    return pltpu.CompilerParams(dimension_semantics=sem, vmem_limit_bytes=vmem)


def _row_tile(m, pref):
    return pref if m % pref == 0 else m


def _rms(x, g):
    ms = jnp.mean(x * x, axis=-1, keepdims=True)
    return x * lax.rsqrt(ms + NORM_EPS) * g


def _ffn_kernel(x_ref, g_ref, wg_ref, wu_ref, wd_ref, o_ref, h_ref, acc_ref):
    j = pl.program_id(1)

    @pl.when(j == 0)
    def _():
        h_ref[...] = _rms(x_ref[...], g_ref[...]).astype(bf16)
        acc_ref[...] = jnp.zeros_like(acc_ref)

    h = h_ref[...]
    g = jnp.dot(h, wg_ref[...], preferred_element_type=f32)
    u = jnp.dot(h, wu_ref[...], preferred_element_type=f32)
    a = (g * jax.nn.sigmoid(g) * u).astype(bf16)
    acc_ref[...] += jnp.dot(a, wd_ref[...], preferred_element_type=f32)

    @pl.when(j == pl.num_programs(1) - 1)
    def _():
        o_ref[...] = x_ref[...] + 0.5 * acc_ref[...]


def _ffn(x, gain, wg, wu, wd):
    m, d = x.shape
    ff = wg.shape[1]
    bm = _row_tile(m, 512)
    bf = 512
    return pl.pallas_call(
        _ffn_kernel,
        out_shape=jax.ShapeDtypeStruct((m, d), f32),
        grid=(m // bm, ff // bf),
        in_specs=[
            pl.BlockSpec((bm, d), lambda i, j: (i, 0)),
            pl.BlockSpec((1, d), lambda i, j: (0, 0)),
            pl.BlockSpec((d, bf), lambda i, j: (0, j)),
            pl.BlockSpec((d, bf), lambda i, j: (0, j)),
            pl.BlockSpec((bf, d), lambda i, j: (j, 0)),
        ],
        out_specs=pl.BlockSpec((bm, d), lambda i, j: (i, 0)),
        scratch_shapes=[pltpu.VMEM((bm, d), bf16), pltpu.VMEM((bm, d), f32)],
        compiler_params=_cparams(("parallel", "arbitrary")),
        name="ffn",
    )(x, gain.reshape(1, d), wg, wu, wd)


def _inproj_kernel(x_ref, g_ref, w_ref, o_ref, h_ref):
    @pl.when(pl.program_id(1) == 0)
    def _():
        h_ref[...] = _rms(x_ref[...], g_ref[...]).astype(bf16)

    o_ref[...] = jnp.dot(h_ref[...], w_ref[...], preferred_element_type=f32)


def _inproj(x, gain, w):
    m, d = x.shape
    n = w.shape[1]
    bm = _row_tile(m, 1024)
    bn = 1536
    assert n % bn == 0
    return pl.pallas_call(
        _inproj_kernel,
        out_shape=jax.ShapeDtypeStruct((m, n), f32),
        grid=(m // bm, n // bn),
        in_specs=[
            pl.BlockSpec((bm, d), lambda i, j: (i, 0)),
            pl.BlockSpec((1, d), lambda i, j: (0, 0)),
            pl.BlockSpec((d, bn), lambda i, j: (0, j)),
        ],
        out_specs=pl.BlockSpec((bm, bn), lambda i, j: (i, j)),
        scratch_shapes=[pltpu.VMEM((bm, d), bf16)],
        compiler_params=_cparams(("parallel", "arbitrary")),
        name="inproj",
    )(x, gain.reshape(1, d), w)


def _merge_kernel(oa_ref, ob_ref, ga_ref, gb_ref, x_ref, wpa_ref, wpb_ref, wo_ref, o_ref):
    pa = jnp.dot(oa_ref[...], wpa_ref[...], preferred_element_type=f32)
    pb = jnp.dot(ob_ref[...], wpb_ref[...], preferred_element_type=f32)
    mix = jax.nn.sigmoid(ga_ref[...]) * pa + jax.nn.sigmoid(gb_ref[...]) * pb
    o_ref[...] = x_ref[...] + jnp.dot(mix.astype(bf16), wo_ref[...], preferred_element_type=f32)


def _merge(oa, ob, z, x, wpa, wpb, wo):
    m, d = x.shape
    bm = _row_tile(m, 256)
    ca = _OFF["g_a"][0] // d
    cb = _OFF["g_b"][0] // d
    return pl.pallas_call(
        _merge_kernel,
        out_shape=jax.ShapeDtypeStruct((m, d), f32),
        grid=(m // bm,),
        in_specs=[
            pl.BlockSpec((bm, NSA_DIM), lambda i: (i, 0)),
            pl.BlockSpec((bm, RW_DIM), lambda i: (i, 0)),
            pl.BlockSpec((bm, d), lambda i: (i, ca)),
            pl.BlockSpec((bm, d), lambda i: (i, cb)),
            pl.BlockSpec((bm, d), lambda i: (i, 0)),
            pl.BlockSpec((NSA_DIM, d), lambda i: (0, 0)),
            pl.BlockSpec((RW_DIM, d), lambda i: (0, 0)),
            pl.BlockSpec((d, d), lambda i: (0, 0)),
        ],
        out_specs=pl.BlockSpec((bm, d), lambda i: (i, 0)),
        compiler_params=_cparams(("parallel",)),
        name="merge",
    )(oa, ob, z, z, x, wpa, wpb, wo)


def _compress_kernel(x_ref, pe_ref, w_ref, o_ref):
    nb = o_ref.shape[1]
    acc = [jnp.zeros((nb, 128), f32), jnp.zeros((nb, 128), f32)]
    for j in range(CMP_BLOCK):
        pe_j = pe_ref[pl.ds(j, 1), :]
        w_j = w_ref[j]
        for h in range(2):
            xj = x_ref[0, pl.ds(2 * j + h, nb, stride=2 * CMP_BLOCK), :] + pe_j
            acc[h] = acc[h] + jnp.dot(xj.astype(bf16), w_j, preferred_element_type=f32)
    o_ref[0] = jnp.concatenate(acc, axis=-1)


def _blockdiag2(m):
    z = jnp.zeros_like(m)
    return jnp.concatenate([jnp.concatenate([m, z], axis=2), jnp.concatenate([z, m], axis=2)], axis=1)


def _compress(x, pe, w):
    b, l, _ = x.shape
    nb = l // CMP_BLOCK
    pe_t = jnp.tile(pe, (1, 2))
    wbd = _blockdiag2(w).astype(bf16)
    return pl.pallas_call(
        _compress_kernel,
        out_shape=jax.ShapeDtypeStruct((b, nb, KV_DIM), f32),
        grid=(b,),
        in_specs=[
            pl.BlockSpec((1, 2 * l, 128), lambda i: (i, 0, 0)),
            pl.BlockSpec((CMP_BLOCK, 128), lambda i: (0, 0)),
            pl.BlockSpec((CMP_BLOCK, 128, 128), lambda i: (0, 0, 0)),
        ],
        out_specs=pl.BlockSpec((1, nb, KV_DIM), lambda i: (i, 0, 0)),
        compiler_params=_cparams(("parallel",)),
        name="compress",
    )(x.reshape(b, 2 * l, 128), pe_t, wbd)


_TK = 512
_ROWS = GROUP * Q_BLOCK
_POS_FEATS = 16


def _softmax_cols(s):
    m = jnp.max(s, axis=0, keepdims=True)
    m = jnp.where(jnp.isfinite(m), m, 0.0)
    p = jnp.exp(s - m)
    return p / jnp.maximum(jnp.sum(p, axis=0, keepdims=True), 1e-30)


def _topk_mask_t(vt, n_sel):
    nb = vt.shape[0]
    bi = lax.broadcasted_iota(jnp.int32, vt.shape, 0)
    sel = jnp.zeros(vt.shape, f32)
    for _ in range(n_sel):
        mx = jnp.max(vt, axis=0, keepdims=True)
        idx = jnp.min(jnp.where(vt == mx, bi, nb), axis=0, keepdims=True)
        hit = bi == idx
        sel = jnp.where(hit, 1.0, sel)
        vt = jnp.where(hit, -jnp.inf, vt)
    return sel


def _nsa_kernel(qt_ref, gt_ref, ck_ref, cvt_ref, ke_ref, vst_ref, kw_ref, vwt_ref,
                o_ref, a0w_ref, used_ref):
    kv = pl.program_id(0)
    i = pl.program_id(1)
    nb = ck_ref.shape[1]
    n_sel = min(N_SEL, nb)
    wk = WINDOW + Q_BLOCK

    lane = lax.broadcasted_iota(jnp.int32, (1, _ROWS), 1)
    grp = lane // Q_BLOCK
    tl = lane % Q_BLOCK
    slope = jnp.exp2(-0.5 * (kv * GROUP + grp + 1).astype(f32))
    tlf = tl.astype(f32)

    @pl.when(i == 0)
    def _():
        kroww = lax.broadcasted_iota(jnp.int32, (wk, 1), 0).astype(f32)
        a0w_ref[...] = slope * (tlf + float(WINDOW) - kroww)

    qt = qt_ref[0, 0]
    t0 = i * Q_BLOCK
    tok = t0 + tl

    blk = lax.broadcasted_iota(jnp.int32, (nb, 1), 0)
    blk_mid = (blk * CMP_BLOCK).astype(f32) + (CMP_BLOCK - 1) / 2
    s = jnp.dot(ck_ref[0], qt, preferred_element_type=f32)
    s = s - slope * (tok.astype(f32) - blk_mid)
    s = jnp.where(blk * CMP_BLOCK + (CMP_BLOCK - 1) <= tok, s, -jnp.inf)
    p_c = _softmax_cols(s)
    o_c = jnp.dot(cvt_ref[0], p_c.astype(bf16), preferred_element_type=f32)

    w0 = pl.multiple_of(t0, Q_BLOCK)
    s = jnp.dot(kw_ref[0, pl.ds(w0, wk), :], qt, preferred_element_type=f32) - a0w_ref[...]
    kroww = lax.broadcasted_iota(jnp.int32, (wk, 1), 0)
    dist = tl + WINDOW - kroww
    s = jnp.where((t0 - WINDOW + kroww >= 0) & (dist >= 0) & (dist < WINDOW), s, -jnp.inf)
    p_w = _softmax_cols(s)
    o_w = jnp.dot(vwt_ref[0, :, pl.ds(w0, wk)], p_w.astype(bf16), preferred_element_type=f32)

    imp = p_c[:, 0:Q_BLOCK]
    for g in range(1, GROUP):
        imp = imp + p_c[:, g * Q_BLOCK:(g + 1) * Q_BLOCK]
    tq = t0 + lax.broadcasted_iota(jnp.int32, (1, Q_BLOCK), 1)
    cur = tq // CMP_BLOCK
    forced = (blk == 0) | (blk == cur) | (blk == cur - 1)
    imp = jnp.where(forced, FORCED_SCORE, jnp.where(blk * CMP_BLOCK <= tq, imp, -1.0))
    sel_t = _topk_mask_t(imp, n_sel)
    bias_t = jnp.where(sel_t > 0.0, 0.0, NEG_BIG).astype(bf16)
    bpt = _TK // CMP_BLOCK
    for j in range(nb // bpt):
        used_ref[j] = jnp.max(sel_t[j * bpt:(j + 1) * bpt, :]).astype(jnp.int32)
    s1 = slope.astype(bf16).astype(f32)
    s2 = (slope - s1).astype(bf16).astype(f32)
    s3 = ((slope - s1) - s2).astype(bf16).astype(f32)
    fr = lax.broadcasted_iota(jnp.int32, (_POS_FEATS, 1), 0)
    pieces = jnp.where(fr < 2, s1, jnp.where(fr < 4, s2, jnp.where(fr < 6, s3, 0.0))).astype(bf16)
    rhs = jnp.concatenate([qt, jnp.concatenate([bias_t] * GROUP, axis=1), pieces], axis=0)

    def tile(j, carry, causal):
        m, l, acc = carry
        c0 = pl.multiple_of(j * _TK, _TK)
        s = jnp.dot(ke_ref[0, pl.ds(c0, _TK), :], rhs, preferred_element_type=f32)
        if causal:
            krow = lax.broadcasted_iota(jnp.int32, (_TK, 1), 0)
            s = jnp.where(tok - (c0 + krow) >= 0, s, -jnp.inf)
        off = slope * (tok - c0).astype(f32)
        m_new = jnp.maximum(m, jnp.max(s, axis=0, keepdims=True) - off)
        p = jnp.exp(s - (m_new + off))
        alpha = jnp.exp(m - m_new)
        l = alpha * l + jnp.sum(p, axis=0, keepdims=True)
        acc = alpha * acc + jnp.dot(vst_ref[0, :, pl.ds(c0, _TK)], p.astype(bf16),
                                    preferred_element_type=f32)
        return m_new, l, acc

    jd = t0 // _TK
    init = (jnp.full((1, _ROWS), -jnp.inf, f32), jnp.zeros((1, _ROWS), f32),
            jnp.zeros((HEAD_DIM, _ROWS), f32))
    carry = lax.fori_loop(
        0, jd, lambda j, c: lax.cond(used_ref[j] > 0, lambda cc: tile(j, cc, False), lambda cc: cc, c), init)
    _, l, acc = tile(jd, carry, True)
    o_s = acc / jnp.maximum(l, 1e-30)

    gt = gt_ref[0, 0]
    out = gt[0:1] * o_c + gt[1:2] * o_s + gt[2:3] * o_w
    o_ref[...] = jnp.concatenate([out[:, g * Q_BLOCK:(g + 1) * Q_BLOCK].T for g in range(GROUP)],
                                 axis=1).astype(o_ref.dtype)


def _nsa_prompt(q, gates, ck, cv, ks, vs, kw, vw):
    t = q.shape[0]
    nqb = t // Q_BLOCK
    nb = ck.shape[0]
    scale = HEAD_DIM ** -0.5
    qt = (q * scale).astype(bf16).reshape(nqb, Q_BLOCK, N_KV, GROUP, HEAD_DIM)
    qt = qt.transpose(2, 0, 4, 3, 1).reshape(N_KV, nqb, HEAD_DIM, _ROWS)
    gt = gates.reshape(nqb, Q_BLOCK, N_KV, GROUP, 3).transpose(2, 0, 4, 3, 1).reshape(N_KV, nqb, 3, _ROWS)
    ckh = ck.astype(bf16).transpose(1, 0, 2)
    cvt = cv.astype(bf16).transpose(1, 2, 0)
    pos = jnp.arange(t)
    onehot = (pos[:, None] // CMP_BLOCK == jnp.arange(nb)[None, :]).astype(bf16)
    off_hi = (pos % _TK) // 16 * 16
    off_lo = pos % 16
    feats = jnp.stack([off_hi, off_lo] * 3 + [jnp.zeros_like(pos)] * (_POS_FEATS - 6), axis=1).astype(bf16)
    ke = jnp.concatenate([ks.astype(bf16).transpose(1, 0, 2),
                          jnp.broadcast_to(jnp.concatenate([onehot, feats], axis=1)[None],
                                           (N_KV, t, nb + _POS_FEATS))], axis=-1)
    vst = vs.astype(bf16).transpose(1, 2, 0)
    kwh = jnp.pad(kw.astype(bf16).transpose(1, 0, 2), ((0, 0), (WINDOW, 0), (0, 0)))
    vwt = jnp.pad(vw.astype(bf16).transpose(1, 2, 0), ((0, 0), (0, 0), (WINDOW, 0)))
    kd = HEAD_DIM + nb + _POS_FEATS
    return pl.pallas_call(
        _nsa_kernel,
        out_shape=jax.ShapeDtypeStruct((t, NSA_DIM), bf16),
        grid=(N_KV, nqb),
        in_specs=[
            pl.BlockSpec((1, 1, HEAD_DIM, _ROWS), lambda k, i: (k, i, 0, 0)),
            pl.BlockSpec((1, 1, 3, _ROWS), lambda k, i: (k, i, 0, 0)),
            pl.BlockSpec((1, nb, HEAD_DIM), lambda k, i: (k, 0, 0)),
            pl.BlockSpec((1, HEAD_DIM, nb), lambda k, i: (k, 0, 0)),
            pl.BlockSpec((1, t, kd), lambda k, i: (k, 0, 0)),
            pl.BlockSpec((1, HEAD_DIM, t), lambda k, i: (k, 0, 0)),
            pl.BlockSpec((1, t + WINDOW, HEAD_DIM), lambda k, i: (k, 0, 0)),
            pl.BlockSpec((1, HEAD_DIM, t + WINDOW), lambda k, i: (k, 0, 0)),
        ],
        out_specs=pl.BlockSpec((Q_BLOCK, GROUP * HEAD_DIM), lambda k, i: (i, k)),
        scratch_shapes=[pltpu.VMEM((WINDOW + Q_BLOCK, _ROWS), f32),
                        pltpu.SMEM((max(nb // (_TK // CMP_BLOCK), 1),), jnp.int32)],
        compiler_params=_cparams(("arbitrary", "arbitrary")),
        name="nsa_prompt",
    )(qt, gt, ckh, cvt, ke, vst, kwh, vwt)


def _pair_sum(x, low):
    s_lo = jnp.sum(jnp.where(low, x, 0.0), axis=-1, keepdims=True)
    s_hi = jnp.sum(jnp.where(low, 0.0, x), axis=-1, keepdims=True)
    return jnp.where(low, s_lo, s_hi)


def _rw_prep_kernel(shift_rows, *refs):
    cur, refs = refs[:6], refs[6:]
    prv, refs = refs[:6], refs[6:]
    if shift_rows:
        st0, refs = refs[:6], refs[6:]
    mus, refs = refs[:6], refs[6:]
    (w0_ref, w2_ref, a0_ref, a2_ref, g2_ref, kk_ref, ka_ref, rk_ref,
     ro_ref, ko_ref, vo_ref, kko_ref, kao_ref, ldo_ref, go_ref, bo_ref) = refs
    first = pl.program_id(0) == 0

    def shifted(n):
        x = cur[n][...]
        if shift_rows:
            prev_row = jnp.where(first, st0[n][pl.ds(7, 1), :], prv[n][pl.ds(7, 1), :])
            rolled = pltpu.roll(x, 1, axis=0)
            rid = lax.broadcasted_iota(jnp.int32, x.shape, 0)
            xp = jnp.where(rid == 0, prev_row, rolled)
        else:
            xp = prv[n][...]
        return x + (xp - x) * mus[n][...]

    r, k, v, gl, wl, al = (shifted(n) for n in range(6))

    y = -(w0_ref[...] + jnp.dot(jnp.tanh(wl).astype(bf16), w2_ref[...], preferred_element_type=f32))
    softplus = jnp.maximum(y, 0.0) + jnp.log1p(jnp.exp(-jnp.abs(y)))
    w_log = -softplus - 0.5
    ld = -jnp.exp(w_log)
    a = jax.nn.sigmoid(a0_ref[...] + jnp.dot(al.astype(bf16), a2_ref[...], preferred_element_type=f32))
    g = jnp.dot(jax.nn.sigmoid(gl).astype(bf16), g2_ref[...], preferred_element_type=f32)
    kk = k * kk_ref[...]
    k2 = k * (1.0 + (a - 1.0) * ka_ref[...])
    rkr = r * k2 * rk_ref[...]
    low = lax.broadcasted_iota(jnp.int32, (1, 2 * RW_N), 1) < RW_N
    for hp in range(RW_HEADS // 2):
        sl = slice(hp * 2 * RW_N, (hp + 1) * 2 * RW_N)
        kkp = kk[:, sl]
        kkp = kkp / jnp.maximum(jnp.sqrt(_pair_sum(kkp * kkp, low)), 1e-12)
        ro_ref[hp] = r[:, sl]
        ko_ref[hp] = k2[:, sl]
        vo_ref[hp] = v[:, sl]
        kko_ref[hp] = kkp
        kao_ref[hp] = kkp * a[:, sl]
        ldo_ref[hp] = ld[:, sl]
        go_ref[hp] = g[:, sl]
        bo_ref[hp] = _pair_sum(rkr[:, sl], low) * v[:, sl]


def _rw_prep(z, prev, mu_p, w0, w2p, a0, a2p, g2, k_k, k_a, r_k, shift_rows):
    m = z.shape[0]
    tm = _row_tile(m, 256)
    names = ("r", "k", "v", "gl", "wl", "al")

    def col_spec(name, rows, imap):
        o, _, pw = _OFF[name]
        return pl.BlockSpec((rows, pw), functools.partial(imap, o // pw))

    cur = [col_spec(n, tm, lambda c, i: (i, c)) for n in names]
    if shift_rows:
        blk8 = tm // 8
        prv = [col_spec(n, 8, lambda c, i: (jnp.maximum(i * blk8 - 1, 0), c)) for n in names]
        prv += [col_spec(n, 8, lambda c, i: (0, c)) for n in names]
        prev_args = [z] * 6 + [prev] * 6
    else:
        prv = [col_spec(n, tm, lambda c, i: (i, c)) for n in names]
        prev_args = [prev] * 6
    mus = [col_spec(n, 1, lambda c, i: (0, c)) for n in names]
    vec = pl.BlockSpec((1, RW_DIM), lambda i: (0, 0))
    out_spec = pl.BlockSpec((RW_HEADS // 2, tm, 2 * RW_N), lambda i: (0, i, 0))
    outs = pl.pallas_call(
        functools.partial(_rw_prep_kernel, shift_rows),
        out_shape=[jax.ShapeDtypeStruct((RW_HEADS // 2, m, 2 * RW_N), f32)] * 8,
        grid=(m // tm,),
        in_specs=cur + prv + mus + [
            vec,
            pl.BlockSpec((128, RW_DIM), lambda i: (0, 0)),
            vec,
            pl.BlockSpec((128, RW_DIM), lambda i: (0, 0)),
            pl.BlockSpec((G_LORA, RW_DIM), lambda i: (0, 0)),
            vec, vec, vec,
        ],
        out_specs=[out_spec] * 8,
        compiler_params=_cparams(("parallel",)),
        name="rw_prep",
    )(*([z] * 6), *prev_args, *([mu_p] * 6),
      w0.reshape(1, RW_DIM), w2p, a0.reshape(1, RW_DIM), a2p, g2,
      k_k.reshape(1, RW_DIM), k_a.reshape(1, RW_DIM), r_k.reshape(1, RW_DIM))
    return outs


_CH = 64
_HEADS_PER_STEP = 16
_NN = (((1,), (0,)), ((), ()))
_NT = (((1,), (1,)), ((), ()))
_TN = (((0,), (0,)), ((), ()))


def _split2(x):
    hi = x.astype(bf16)
    lo = (x - hi.astype(f32)).astype(bf16)
    return hi, lo


def _dot3(a, b, dims=_NN):
    ah, al = _split2(a)
    bh, bl = _split2(b)
    d = functools.partial(lax.dot_general, dimension_numbers=dims, preferred_element_type=f32)
    return d(ah, bh) + (d(ah, bl) + d(al, bh))


def _split3(x):
    x1 = x.astype(bf16)
    r1 = x - x1.astype(f32)
    x2 = r1.astype(bf16)
    return x1, x2, (r1 - x2.astype(f32)).astype(bf16)


def _dot_exact_rhs(a01, b):
    b1, b2, b3 = _split3(b)
    d = functools.partial(jnp.dot, preferred_element_type=f32)
    return d(a01, b1) + (d(a01, b2) + d(a01, b3))


def _dot_exact_lhs_tn(a, b01):
    a1, a2, a3 = _split3(a)
    d = functools.partial(lax.dot_general, dimension_numbers=_TN, preferred_element_type=f32)
    return d(a1, b01) + (d(a2, b01) + d(a3, b01))


def _wkv_chunk_kernel(r_ref, k_ref, v_ref, kk_ref, ka_ref, ld_ref, s0_ref, y_ref, sT_ref, st_ref):
    c = pl.program_id(0)

    @pl.when(c == 0)
    def _():
        st_ref[...] = s0_ref[...]

    ti = lax.broadcasted_iota(jnp.int32, (_CH, _CH), 0)
    si = lax.broadcasted_iota(jnp.int32, (_CH, _CH), 1)
    incl = ti >= si
    strict = ti > si
    l_incl = incl.astype(bf16)
    eye = (ti == si).astype(f32)
    ones = jnp.ones((_CH, RW_N), bf16)

    def each(f, *lists):
        return [f(*a) for a in zip(*lists)]

    def cat0(a, b):
        return jnp.concatenate([a, b], axis=0)

    def cat1(a, b):
        return jnp.concatenate([a, b], axis=1)

    def heads_step(heads):
        def load(ref):
            return [ref[hp, :, sub * RW_N:(sub + 1) * RW_N] for hp, sub in heads]

        r, k, v, kk, ka, ld = (load(ref) for ref in (r_ref, k_ref, v_ref, kk_ref, ka_ref, ld_ref))
        lp = each(lambda x: _dot_exact_rhs(l_incl, x), ld)
        lp_end = each(lambda x: _dot_exact_lhs_tn(x, ones), ld)
        e_neg = each(lambda x: jnp.exp(-x), lp)
        at = each(lambda a, x, y: -a * jnp.exp(x - y), kk, lp, ld)
        bt = each(jnp.multiply, ka, e_neg)
        kt = each(jnp.multiply, k, e_neg)
        rt = each(lambda a, x: a * jnp.exp(x), r, lp)
        e_end = each(lambda x: jnp.exp(x[_CH - 1:_CH, :] - x), lp)
        bh = each(jnp.multiply, ka, e_end)
        kh = each(jnp.multiply, k, e_end)
        sc = each(lambda a, b: _dot3(a, b, _NT), each(cat0, at, rt), each(cat0, bt, kt))
        a_b = each(lambda x: jnp.where(strict, x[:_CH, :_CH], 0.0), sc)
        a_k = each(lambda x: jnp.where(strict, x[:_CH, _CH:], 0.0), sc)
        g_b = each(lambda x: jnp.where(incl, x[_CH:, :_CH], 0.0), sc)
        g_k = each(lambda x: jnp.where(incl, x[_CH:, _CH:], 0.0), sc)
        tm = each(lambda x: eye + x, a_b)
        pw = a_b
        for _ in range(5):
            pw = each(_dot3, pw, pw)
            tm = each(lambda t_, p_: t_ + _dot3(t_, p_), tm, pw)
        akv = each(_dot3, a_k, v)
        tx = each(_dot3, tm, each(cat1, at, akv))
        st = [st_ref[2 * hp + sub] for hp, sub in heads]
        ws = each(_dot3, each(lambda x, y: cat0(x[:, :RW_N], y), tx, rt), st)
        u = each(lambda a, x: a[:_CH] + x[:, RW_N:], ws, tx)
        uv = each(cat0, u, v)
        y = each(lambda a, gb, gk, x: a[_CH:] + _dot3(cat1(gb, gk), x), ws, g_b, g_k, uv)
        st_new = each(lambda e, s_, b_, k_, x: jnp.exp(e) * s_ + _dot3(cat0(b_, k_), x, _TN),
                      lp_end, st, bh, kh, uv)
        for (hp, sub), s_ in zip(heads, st_new):
            st_ref[2 * hp + sub] = s_
        return y

    def pairs(i, carry):
        heads = [(_HEADS_PER_STEP // 2 * i + a, sub) for a in range(_HEADS_PER_STEP // 2) for sub in range(2)]
        y = heads_step(heads)
        for a in range(_HEADS_PER_STEP // 2):
            y_ref[_HEADS_PER_STEP // 2 * i + a] = cat1(y[2 * a], y[2 * a + 1])
        return carry

    lax.fori_loop(0, RW_HEADS // _HEADS_PER_STEP, pairs, 0)

    @pl.when(c == pl.num_programs(0) - 1)
    def _():
        sT_ref[...] = st_ref[...]


def _wkv_chunks(r, k, v, kk, ka, ld, s0t):
    t = r.shape[1]
    spec = pl.BlockSpec((RW_HEADS // 2, _CH, 2 * RW_N), lambda c: (0, c, 0))
    sspec = pl.BlockSpec((RW_HEADS, RW_N, RW_N), lambda c: (0, 0, 0))
    return pl.pallas_call(
        _wkv_chunk_kernel,
        out_shape=[jax.ShapeDtypeStruct((RW_HEADS // 2, t, 2 * RW_N), f32),
                   jax.ShapeDtypeStruct((RW_HEADS, RW_N, RW_N), f32)],
        grid=(t // _CH,),
        in_specs=[spec] * 6 + [sspec],
        out_specs=[spec, sspec],
        scratch_shapes=[pltpu.VMEM((RW_HEADS, RW_N, RW_N), f32)],
        compiler_params=_cparams(("arbitrary",)),
        name="wkv_chunks",
    )(r, k, v, kk, ka, ld, s0t)


def _rw_post_kernel(y_ref, b_ref, g_ref, lw_ref, lb_ref, o_ref):
    low = lax.broadcasted_iota(jnp.int32, (1, 2 * RW_N), 1) < RW_N
    for hp in range(RW_HEADS // 2):
        y = y_ref[hp]
        mu = _pair_sum(y, low) / RW_N
        var = _pair_sum(jnp.square(y - mu), low) / RW_N
        yn = (y - mu) * lax.rsqrt(var + GN_EPS) * lw_ref[hp] + lb_ref[hp]
        o_ref[:, hp * 2 * RW_N:(hp + 1) * 2 * RW_N] = ((yn + b_ref[hp]) * g_ref[hp]).astype(o_ref.dtype)


def _rw_post(y, bonus, g, ln_w, ln_b):
    m = y.shape[1]
    tm = _row_tile(m, 512)
    spec = pl.BlockSpec((RW_HEADS // 2, tm, 2 * RW_N), lambda i: (0, i, 0))
    pspec = pl.BlockSpec((RW_HEADS // 2, 1, 2 * RW_N), lambda i: (0, 0, 0))
    return pl.pallas_call(
        _rw_post_kernel,
        out_shape=jax.ShapeDtypeStruct((m, RW_DIM), bf16),
        grid=(m // tm,),
        in_specs=[spec, spec, spec, pspec, pspec],
        out_specs=pl.BlockSpec((tm, RW_DIM), lambda i: (i, 0)),
        compiler_params=_cparams(("parallel",)),
        name="rw_post",
    )(y, bonus, g, ln_w.reshape(RW_HEADS // 2, 1, 2 * RW_N), ln_b.reshape(RW_HEADS // 2, 1, 2 * RW_N))


def _lora_pad(w):
    return jnp.pad(w, ((0, 128 - w.shape[0]), (0, 0))).astype(bf16)


def _rwkv_prompt(z, shift0_p, s0t, p):
    r, k, v, kk, ka, ld, g, bonus = _rw_prep(
        z, shift0_p, p["mu_p"], p["rw_w0"], p["w2p"], p["rw_a0"], p["a2p"], p["g2"],
        p["rw_k_k"], p["rw_k_a"], p["rw_r_k"], True)
    y, st = _wkv_chunks(r, k, v, kk, ka, ld, s0t)
    return _rw_post(y, bonus, g, p["rw_ln_w"], p["rw_ln_b"]), st


def _wkv_step_kernel(s_ref, r_ref, k_ref, vc_ref, kk_ref, ka_ref, ld_ref, y_ref, so_ref):
    s = s_ref[0]
    kk = kk_ref[0][:, None, :]
    sa = jnp.sum(s * (-kk), axis=-1, keepdims=True)
    s = (s * jnp.exp(ld_ref[0])[:, None, :] + sa * ka_ref[0][:, None, :]
         + vc_ref[0] * k_ref[0][:, None, :])
    so_ref[0] = s
    y_ref[0] = jnp.sum(s * r_ref[0][:, None, :], axis=-1, keepdims=True)


def _wkv_step(s0, r, k, v, kk, ka, ld):
    b = s0.shape[0]
    sspec = pl.BlockSpec((1, RW_HEADS, RW_N, RW_N), lambda i: (i, 0, 0, 0))
    vspec = pl.BlockSpec((1, RW_HEADS, RW_N), lambda i: (i, 0, 0))
    cspec = pl.BlockSpec((1, RW_HEADS, RW_N, 1), lambda i: (i, 0, 0, 0))
    y, s1 = pl.pallas_call(
        _wkv_step_kernel,
        out_shape=[jax.ShapeDtypeStruct((b, RW_HEADS, RW_N, 1), f32),
                   jax.ShapeDtypeStruct(s0.shape, f32)],
        grid=(b,),
        in_specs=[sspec, vspec, vspec, cspec, vspec, vspec, vspec],
        out_specs=[cspec, sspec],
        compiler_params=_cparams(("parallel",)),
        name="wkv_step",
    )(s0, r, k, v[..., None], kk, ka, ld)
    return y[..., 0], s1


_GP = 8


_PAGE_ROWS = N_KV * HEAD_DIM


def _compress_pages_kernel(npg, pt_ref, pool_ref, pe_ref, w_ref, o_ref, buf0, buf1, sem):
    g = pl.program_id(0)
    n_steps = pl.num_programs(0)

    def page_copies(seq, buf, slot):
        return [pltpu.make_async_copy(pool_ref.at[pt_ref[seq, pg]],
                                      buf.at[pl.ds(pg * _PAGE_ROWS, _PAGE_ROWS)], sem.at[slot])
                for pg in range(npg)]

    def start(seq, buf, slot):
        for cp in page_copies(seq, buf, slot):
            cp.start()

    def wait(seq, buf, slot):
        for cp in page_copies(seq, buf, slot):
            cp.wait()

    def compress(buf, out_slot):
        acc = jnp.zeros((npg * N_KV, 2 * CMP_BLOCK), f32)
        for d2 in range(HEAD_DIM // 2):
            x = jnp.concatenate(
                [buf[pl.ds(2 * d2 + e, npg * N_KV, stride=HEAD_DIM), :] + pe_ref[pl.ds(2 * d2 + e, 1), :]
                 for e in range(2)], axis=1)
            acc = acc + jnp.dot(x.astype(bf16), w_ref[d2], preferred_element_type=f32)
        o_ref[out_slot] = acc

    @pl.when(g == 0)
    def _():
        start(0, buf0, 0)

    start(2 * g + 1, buf1, 1)
    wait(2 * g, buf0, 0)
    compress(buf0, 0)

    @pl.when(g + 1 < n_steps)
    def _():
        start(2 * g + 2, buf0, 0)

    wait(2 * g + 1, buf1, 1)
    compress(buf1, 1)


def _compress_pages(pool_t, table, pe, w):
    b, npg = table.shape
    page = pool_t.shape[-1]
    assert b % 2 == 0 and page == 2 * CMP_BLOCK
    pe_t = jnp.tile(pe.T, (1, 2))
    wd = _blockdiag2(w.transpose(1, 0, 2)).reshape(HEAD_DIM // 2, 2 * page, page).astype(bf16)
    out = pl.pallas_call(
        functools.partial(_compress_pages_kernel, npg),
        out_shape=jax.ShapeDtypeStruct((b, npg * N_KV, page), f32),
        grid_spec=pltpu.PrefetchScalarGridSpec(
            num_scalar_prefetch=1,
            grid=(b // 2,),
            in_specs=[
                pl.BlockSpec(memory_space=pl.ANY),
                pl.BlockSpec((HEAD_DIM, page), lambda i, pt: (0, 0)),
                pl.BlockSpec((HEAD_DIM // 2, 2 * page, page), lambda i, pt: (0, 0, 0)),
            ],
            out_specs=pl.BlockSpec((2, npg * N_KV, page), lambda i, pt: (i, 0, 0)),
            scratch_shapes=[pltpu.VMEM((npg * _PAGE_ROWS, page), f32),
                            pltpu.VMEM((npg * _PAGE_ROWS, page), f32),
                            pltpu.SemaphoreType.DMA((2,))]),
        compiler_params=_cparams(("arbitrary",)),
        name="compress_pages",
    )(table, pool_t, pe_t, wd)
    out = out.reshape(b, npg, N_KV, 2, HEAD_DIM).transpose(0, 1, 3, 2, 4)
    return out.reshape(b, 2 * npg, N_KV, HEAD_DIM)


def _dec_cmp_kernel(n_pick, t_pos, q_ref, ckt_ref, cv_ref, xk_ref, xv_ref, wk_ref, wv_ref, gkc_ref,
                    oc_ref, idx_ref):
    sb = q_ref.shape[0]
    nb = ckt_ref.shape[-1]
    kd = xk_ref.shape[-1]
    blk = lax.broadcasted_iota(jnp.int32, (1, nb), 1)
    blk_mid = (blk * CMP_BLOCK).astype(f32) + (CMP_BLOCK - 1) / 2
    ck_new = _rms(jnp.dot(xk_ref[...].reshape(sb * _GP, kd).astype(bf16), wk_ref[...],
                          preferred_element_type=f32), gkc_ref[...])
    cv_new = jnp.dot(xv_ref[...].reshape(sb * _GP, kd).astype(bf16), wv_ref[...],
                     preferred_element_type=f32)
    new_mid = float(nb * CMP_BLOCK) + (CMP_BLOCK - 1) / 2
    new_ok = nb * CMP_BLOCK + (CMP_BLOCK - 1) <= t_pos
    lane = lax.broadcasted_iota(jnp.int32, (1, 128), 1)
    imps = []
    for r in range(sb):
        for kv in range(N_KV):
            row = r * _GP + kv
            q = q_ref[r, kv]
            g1 = lax.broadcasted_iota(jnp.int32, (_GP, 1), 0) + (kv * GROUP + 1)
            slope = jnp.exp2(-0.5 * g1.astype(f32))
            s = jnp.dot(q.astype(bf16), ckt_ref[r, kv], preferred_element_type=f32)
            s = s - slope * (float(t_pos) - blk_mid)
            s = jnp.where(blk * CMP_BLOCK + (CMP_BLOCK - 1) <= t_pos, s, -jnp.inf)
            qn = q.astype(bf16).astype(f32)
            s_new = jnp.sum(qn * ck_new[row:row + 1].astype(bf16).astype(f32), axis=-1, keepdims=True)
            s_new = s_new - slope * (float(t_pos) - new_mid)
            s_new = jnp.where(new_ok, s_new, -jnp.inf)
            m = jnp.maximum(jnp.max(s, axis=-1, keepdims=True), s_new)
            m = jnp.where(jnp.isfinite(m), m, 0.0)
            p = jnp.exp(s - m)
            p_new = jnp.exp(s_new - m)
            den = jnp.maximum(jnp.sum(p, axis=-1, keepdims=True) + p_new, 1e-30)
            p = p / den
            p_new = p_new / den
            oc = jnp.dot(p.astype(bf16), cv_ref[r, kv], preferred_element_type=f32)
            oc_ref[r, kv] = oc + p_new * cv_new[row:row + 1]
            imps.append(jnp.sum(p[:GROUP], axis=0, keepdims=True))
    imp = jnp.concatenate(imps, axis=0)
    cur = t_pos // CMP_BLOCK
    forced = (blk == 0) | (blk == cur) | (blk == cur - 1)
    v = jnp.where(forced, FORCED_SCORE, jnp.where(blk * CMP_BLOCK <= t_pos, imp, -1.0))
    out = jnp.zeros((sb * N_KV, 128), jnp.int32)
    for it in range(n_pick):
        mx = jnp.max(v, axis=-1, keepdims=True)
        idx = jnp.min(jnp.where(v == mx, blk, nb), axis=-1, keepdims=True)
        out = jnp.where(lane == it, idx, out)
        v = jnp.where(blk == idx, -jnp.inf, v)
    idx_ref[...] = out.reshape(sb, N_KV, 128)


def _dec_cmp(q, ck, cv, xk, xv, wk, wv, g_kc, t_pos, n_pick):
    b, nb = ck.shape[:2]
    ckt = ck.astype(bf16).transpose(0, 2, 3, 1)
    cvh = cv.astype(bf16).transpose(0, 2, 1, 3)
    kd = CMP_BLOCK * HEAD_DIM
    sb = 8 if b % 8 == 0 else b
    return pl.pallas_call(
        functools.partial(_dec_cmp_kernel, n_pick, t_pos),
        out_shape=[jax.ShapeDtypeStruct((b, N_KV, _GP, HEAD_DIM), f32),
                   jax.ShapeDtypeStruct((b, N_KV, 128), jnp.int32)],
        grid=(b // sb,),
        in_specs=[
            pl.BlockSpec((sb, N_KV, _GP, HEAD_DIM), lambda i: (i, 0, 0, 0)),
            pl.BlockSpec((sb, N_KV, HEAD_DIM, nb), lambda i: (i, 0, 0, 0)),
            pl.BlockSpec((sb, N_KV, nb, HEAD_DIM), lambda i: (i, 0, 0, 0)),
            pl.BlockSpec((sb, _GP, kd), lambda i: (i, 0, 0)),
            pl.BlockSpec((sb, _GP, kd), lambda i: (i, 0, 0)),
            pl.BlockSpec((kd, HEAD_DIM), lambda i: (0, 0)),
            pl.BlockSpec((kd, HEAD_DIM), lambda i: (0, 0)),
            pl.BlockSpec((1, HEAD_DIM), lambda i: (0, 0)),
        ],
        out_specs=[pl.BlockSpec((sb, N_KV, _GP, HEAD_DIM), lambda i: (i, 0, 0, 0)),
                   pl.BlockSpec((sb, N_KV, 128), lambda i: (i, 0, 0))],
        compiler_params=_cparams(("parallel",)),
        name="dec_cmp",
    )(q, ckt, cvh, xk, xv, wk, wv, g_kc.reshape(1, HEAD_DIM))


def _dec_sel_kernel(t_pos, n_pick, pt_ref, idx_ref, q_ref, gt_ref, oc_ref, ksn_ref, vsn_ref, kwn_ref,
                    vwn_ref, wk_ref, wv_ref, pk_ref, pv_ref, o_ref, kb0, vb0, kb1, vb1, sem):
    g = pl.program_id(0)
    n_steps = pl.num_programs(0)
    n_buf = wk_ref.shape[-1]
    page = 2 * CMP_BLOCK

    def copies(seq, kb, vb, slot):
        out = []
        for kv in range(N_KV):
            for s in range(n_pick):
                pg = pt_ref[seq, idx_ref[seq, kv * 128 + s] // 2]
                dst = pl.ds(s * page, page)
                out.append(pltpu.make_async_copy(pk_ref.at[pg, kv], kb.at[kv, :, dst], sem.at[slot]))
                out.append(pltpu.make_async_copy(pv_ref.at[pg, kv], vb.at[kv, :, dst], sem.at[slot]))
        return out

    def start(seq, kb, vb, slot):
        for cp in copies(seq, kb, vb, slot):
            cp.start()

    def wait(seq, kb, vb, slot):
        for cp in copies(seq, kb, vb, slot):
            cp.wait()

    def slopes(kv):
        g1 = lax.broadcasted_iota(jnp.int32, (_GP, 1), 0) + (kv * GROUP + 1)
        return jnp.exp2(-0.5 * g1.astype(f32))

    lane = lax.broadcasted_iota(jnp.int32, (1, page), 1)
    c = lax.broadcasted_iota(jnp.int32, (1, n_buf), 1)
    kpos = t_pos - n_buf + c
    distw = t_pos - kpos
    okw = (kpos >= 0) & (distw >= 0) & (distw < WINDOW)

    def attend(seq, r, kb, vb):
        for kv in range(N_KV):
            q = q_ref[r, kv].astype(bf16)
            qf = q.astype(f32)
            slope = slopes(kv)
            dist, ok = [], []
            for s in range(n_pick):
                blk = idx_ref[seq, kv * 128 + s]
                d = t_pos - ((blk // 2) * page + lane)
                dist.append(d)
                ok.append((lane // CMP_BLOCK == blk % 2) & (d >= 0))
            dist = jnp.concatenate(dist, axis=1)
            ok = jnp.concatenate(ok, axis=1)
            s_sel = jnp.dot(q, kb[kv].astype(bf16), preferred_element_type=f32)
            s_sel = jnp.where(ok, s_sel - slope * dist.astype(f32), -jnp.inf)
            s_new = jnp.sum(qf * ksn_ref[r, kv:kv + 1].astype(bf16).astype(f32), axis=-1, keepdims=True)
            m = jnp.maximum(jnp.max(s_sel, axis=-1, keepdims=True), s_new)
            p = jnp.exp(s_sel - m)
            p_new = jnp.exp(s_new - m)
            den = jnp.maximum(jnp.sum(p, axis=-1, keepdims=True) + p_new, 1e-30)
            o_s = lax.dot_general((p / den).astype(bf16), vb[kv].astype(bf16), _NT, preferred_element_type=f32)
            o_s = o_s + (p_new / den).astype(bf16).astype(f32) * vsn_ref[r, kv:kv + 1].astype(bf16).astype(f32)
            sw = jnp.dot(q, wk_ref[r, kv].astype(bf16), preferred_element_type=f32)
            sw = jnp.where(okw, sw - slope * distw.astype(f32), -jnp.inf)
            sw_new = jnp.sum(qf * kwn_ref[r, kv:kv + 1].astype(bf16).astype(f32), axis=-1, keepdims=True)
            mw = jnp.maximum(jnp.max(sw, axis=-1, keepdims=True), sw_new)
            pw = jnp.exp(sw - mw)
            pw_new = jnp.exp(sw_new - mw)
            denw = jnp.maximum(jnp.sum(pw, axis=-1, keepdims=True) + pw_new, 1e-30)
            o_w = lax.dot_general((pw / denw).astype(bf16), wv_ref[r, kv].astype(bf16), _NT,
                                  preferred_element_type=f32)
            o_w = o_w + (pw_new / denw).astype(bf16).astype(f32) * vwn_ref[r, kv:kv + 1].astype(bf16).astype(f32)
            gt = gt_ref[r, kv]
            o_ref[r, kv] = gt[:, 0:1] * oc_ref[r, kv] + gt[:, 1:2] * o_s + gt[:, 2:3] * o_w

    @pl.when(g == 0)
    def _():
        start(0, kb0, vb0, 0)

    start(2 * g + 1, kb1, vb1, 1)
    wait(2 * g, kb0, vb0, 0)
    attend(2 * g, 0, kb0, vb0)

    @pl.when(g + 1 < n_steps)
    def _():
        start(2 * g + 2, kb0, vb0, 0)

    wait(2 * g + 1, kb1, vb1, 1)
    attend(2 * g + 1, 1, kb1, vb1)


def _dec_sel(q, gates, o_c, idx, table, pool_k, pool_v, ks_new, vs_new, kw_new, vw_new, win_k, win_v,
             t_pos, n_pick):
    b = q.shape[0]
    assert b % 2 == 0
    n_buf = win_k.shape[-1]
    page = pool_k.shape[-1]
    hspec = pl.BlockSpec((2, N_KV, _GP, HEAD_DIM), lambda i, pt, ix: (i, 0, 0, 0))
    nspec = pl.BlockSpec((2, N_KV, HEAD_DIM), lambda i, pt, ix: (i, 0, 0))
    wspec = pl.BlockSpec((2, N_KV, HEAD_DIM, n_buf), lambda i, pt, ix: (i, 0, 0, 0))
    anyspec = pl.BlockSpec(memory_space=pl.ANY)
    gbuf = pltpu.VMEM((N_KV, HEAD_DIM, n_pick * page), f32)
    return pl.pallas_call(
        functools.partial(_dec_sel_kernel, t_pos, n_pick),
        out_shape=jax.ShapeDtypeStruct((b, N_KV, _GP, HEAD_DIM), f32),
        grid_spec=pltpu.PrefetchScalarGridSpec(
            num_scalar_prefetch=2,
            grid=(b // 2,),
            in_specs=[hspec, pl.BlockSpec((2, N_KV, _GP, 3), lambda i, pt, ix: (i, 0, 0, 0)), hspec,
                      nspec, nspec, nspec, nspec, wspec, wspec, anyspec, anyspec],
            out_specs=hspec,
            scratch_shapes=[gbuf, gbuf, gbuf, gbuf, pltpu.SemaphoreType.DMA((2,))]),
        compiler_params=_cparams(("arbitrary",)),
        name="dec_sel",
    )(table, idx, q, gates, o_c, ks_new, vs_new, kw_new, vw_new, win_k, win_v, pool_k, pool_v)


def _head_rms(x, g):
    ms = jnp.mean(x * x, axis=-1, keepdims=True)
    return x * lax.rsqrt(ms + NORM_EPS) * g


def _prepare(p):
    q = dict(p)
    for n in ("ffn1_gate", "ffn1_up", "ffn1_down", "ffn2_gate", "ffn2_up", "ffn2_down", "w_pa", "w_pb", "w_out"):
        q[n] = p[n].astype(bf16)
    q["w_in_p"] = _pad_cols(p["w_in"]).astype(bf16)
    q["mu_p"] = _pad_rw_cols(p["rw_mu"][None])
    q["w2p"] = _lora_pad(p["rw_w2"])
    q["a2p"] = _lora_pad(p["rw_a2"])
    q["g2"] = p["rw_g2"].astype(bf16)
    return q


def _nsa_proj(z, p):
    m = z.shape[0]
    kvs = (m, N_KV, HEAD_DIM)
    q = _head_rms(_seg(z, "q").reshape(m, N_HEADS, HEAD_DIM), p["g_q"])
    kc = _seg(z, "kc").reshape(kvs)
    vc = _seg(z, "vc").reshape(kvs)
    ks = _head_rms(_seg(z, "ks").reshape(kvs), p["g_ks"])
    vs = _seg(z, "vs").reshape(kvs)
    kw = _head_rms(_seg(z, "kw").reshape(kvs), p["g_kw"])
    vw = _seg(z, "vw").reshape(kvs)
    gates = jax.nn.sigmoid(_seg(z, "ga").reshape(m, N_HEADS, 3))
    return q, gates, kc, vc, ks, vs, kw, vw


def _layer_prompt(x, p):
    t = x.shape[0]
    x = _ffn(x, p["n_ffn1"], p["ffn1_gate"], p["ffn1_up"], p["ffn1_down"])
    z = _inproj(x, p["n_mix"], p["w_in_p"])
    q, gates, kc, vc, ks, vs, kw, vw = _nsa_proj(z, p)
    ck = _head_rms(_compress(kc.reshape(1, t, KV_DIM), p["pe_cmp_k"], p["w_cmp_k"])[0]
                   .reshape(-1, N_KV, HEAD_DIM), p["g_kc"])
    cv = _compress(vc.reshape(1, t, KV_DIM), p["pe_cmp_v"], p["w_cmp_v"])[0].reshape(-1, N_KV, HEAD_DIM)
    o_a = _nsa_prompt(q, gates, ck, cv, ks, vs, kw, vw)
    shift0 = jnp.zeros((8, _Z_COLS), f32)
    s0t = jnp.zeros((RW_HEADS, RW_N, RW_N), f32)
    o_b, st = _rwkv_prompt(z, shift0, s0t, p)
    x = _merge(o_a, o_b, z, x, p["w_pa"], p["w_pb"], p["w_out"])
    x = _ffn(x, p["n_ffn2"], p["ffn2_gate"], p["ffn2_up"], p["ffn2_down"])
    n_buf = min(WINDOW, t)
    states = (kc, vc, ks, vs, kw[-n_buf:], vw[-n_buf:], _rw_cols(z[-1:]), st.transpose(0, 2, 1)[None])
    return x, states


def _layer_sample(x, p, past):
    b = x.shape[0]
    table = past["page_table"]
    page = past["cmp_k"].shape[1]
    t_pos = table.shape[1] * page
    x = _ffn(x, p["n_ffn1"], p["ffn1_gate"], p["ffn1_up"], p["ffn1_down"])
    z = _inproj(x, p["n_mix"], p["w_in_p"])
    q, gates, kc, vc, ks, vs, kw, vw = _nsa_proj(z, p)

    fm = lambda a: a.transpose(0, 2, 3, 1)
    pool_rows = lambda a: fm(a).reshape(-1, _PAGE_ROWS, page)

    nbp = t_pos // CMP_BLOCK
    ck = _head_rms(_compress_pages(pool_rows(past["cmp_k"]), table, p["pe_cmp_k"], p["w_cmp_k"]), p["g_kc"])
    cv = _compress_pages(pool_rows(past["cmp_v"]), table, p["pe_cmp_v"], p["w_cmp_v"])

    def new_block_rows(k_new, pe):
        first = k_new + pe[0]
        rest = jnp.broadcast_to(pe[1:].reshape(1, 1, -1), (b, N_KV, (CMP_BLOCK - 1) * HEAD_DIM))
        rows = jnp.concatenate([first, rest], axis=-1)
        return jnp.pad(rows, ((0, 0), (0, _GP - N_KV), (0, 0)))

    n_sel = min(N_SEL, nbp + 1)
    n_pick = n_sel - 1
    scale = HEAD_DIM ** -0.5
    pad_g = lambda a: jnp.pad(a, ((0, 0), (0, 0), (0, _GP - GROUP), (0, 0)))
    qh = pad_g((q * scale).reshape(b, N_KV, GROUP, HEAD_DIM))
    gth = pad_g(gates.reshape(b, N_KV, GROUP, 3))
    kd = CMP_BLOCK * HEAD_DIM
    o_c, idx = _dec_cmp(qh, ck, cv, new_block_rows(kc, p["pe_cmp_k"]), new_block_rows(vc, p["pe_cmp_v"]),
                        p["w_cmp_k"].reshape(kd, HEAD_DIM).astype(bf16),
                        p["w_cmp_v"].reshape(kd, HEAD_DIM).astype(bf16), p["g_kc"], t_pos, n_pick)
    o_a = _dec_sel(qh, gth, o_c, idx.reshape(b, N_KV * 128), table, fm(past["slc_k"]), fm(past["slc_v"]),
                   ks, vs, kw, vw, fm(past["win_k"]), fm(past["win_v"]), t_pos, n_pick)
    o_a = o_a[:, :, :GROUP].reshape(b, NSA_DIM).astype(bf16)

    prev = _pad_rw_cols(past["shift"])
    r, k, v, kk, ka, ld, g, bonus = _rw_prep(
        z, prev, p["mu_p"], p["rw_w0"], p["w2p"], p["rw_a0"], p["a2p"], p["g2"],
        p["rw_k_k"], p["rw_k_a"], p["rw_r_k"], False)
    tb = lambda a: a.transpose(1, 0, 2).reshape(b, RW_HEADS, RW_N)
    y, wkv = _wkv_step(past["wkv"], tb(r), tb(k), tb(v), tb(kk), tb(ka), tb(ld))
    y = y.reshape(b, RW_HEADS // 2, 2 * RW_N).transpose(1, 0, 2)
    o_b = _rw_post(y, bonus, g, p["rw_ln_w"], p["rw_ln_b"])

    x = _merge(o_a, o_b, z, x, p["w_pa"], p["w_pb"], p["w_out"])
    x = _ffn(x, p["n_ffn2"], p["ffn2_gate"], p["ffn2_up"], p["ffn2_down"])
    kvs = lambda a: a.reshape(b, 1, N_KV, HEAD_DIM)
    win_k = jnp.concatenate([past["win_k"][:, 1:], kvs(kw)], axis=1)
    win_v = jnp.concatenate([past["win_v"][:, 1:], kvs(vw)], axis=1)
    states = (kvs(kc), kvs(vc), kvs(ks), kvs(vs), win_k, win_v, _rw_cols(z), wkv)
    return x, states


def kernel(x_prompt, x_sample, cache_cmp_k, cache_cmp_v, cache_slc_k, cache_slc_v, cache_win_k, cache_win_v,
           state_shift, state_wkv, page_table,
           n_ffn1, ffn1_gate, ffn1_up, ffn1_down, n_mix, w_in, g_q, g_kc, g_ks, g_kw,
           w_cmp_k, pe_cmp_k, w_cmp_v, pe_cmp_v,
           rw_mu, rw_w0, rw_w2, rw_a0, rw_a2, rw_g2, rw_k_k, rw_k_a, rw_r_k, rw_ln_w, rw_ln_b,
           w_pa, w_pb, w_out, n_ffn2, ffn2_gate, ffn2_up, ffn2_down):
    assert x_prompt.shape[0] == 1 and x_sample.shape[1] == 1 and n_ffn1.shape[0] == 1
    l = 0
    p = _prepare(dict(
        n_ffn1=n_ffn1[l], ffn1_gate=ffn1_gate[l], ffn1_up=ffn1_up[l], ffn1_down=ffn1_down[l],
        n_mix=n_mix[l], w_in=w_in[l], g_q=g_q[l], g_kc=g_kc[l], g_ks=g_ks[l], g_kw=g_kw[l],
        w_cmp_k=w_cmp_k[l], pe_cmp_k=pe_cmp_k[l], w_cmp_v=w_cmp_v[l], pe_cmp_v=pe_cmp_v[l],
        rw_mu=rw_mu[l], rw_w0=rw_w0[l], rw_w2=rw_w2[l], rw_a0=rw_a0[l], rw_a2=rw_a2[l], rw_g2=rw_g2[l],
        rw_k_k=rw_k_k[l], rw_k_a=rw_k_a[l], rw_r_k=rw_r_k[l], rw_ln_w=rw_ln_w[l], rw_ln_b=rw_ln_b[l],
        w_pa=w_pa[l], w_pb=w_pb[l], w_out=w_out[l],
        n_ffn2=n_ffn2[l], ffn2_gate=ffn2_gate[l], ffn2_up=ffn2_up[l], ffn2_down=ffn2_down[l]))
    t = x_prompt.shape[1]
    y_p, sp = _layer_prompt(x_prompt[0], p)
    past = dict(page_table=page_table, cmp_k=cache_cmp_k[l], cmp_v=cache_cmp_v[l], slc_k=cache_slc_k[l],
                slc_v=cache_slc_v[l], win_k=cache_win_k[l], win_v=cache_win_v[l], shift=state_shift[l],
                wkv=state_wkv[l])
    y_s, ss = _layer_sample(x_sample[:, 0], p, past)
    kvp = lambda a: a.reshape(1, 1, -1, N_KV, HEAD_DIM)
    outs_p = (kvp(sp[0]), kvp(sp[1]), kvp(sp[2]), kvp(sp[3]), kvp(sp[4]), kvp(sp[5]), sp[6][None], sp[7][None])
    outs_s = tuple(a[None] for a in ss)
    return (y_p.reshape(1, t, D_MODEL), y_s[:, None, :]) + outs_p + outs_s
```

```python
import functools

import jax
import jax.numpy as jnp
from jax import lax
from jax.experimental import pallas as pl
from jax.experimental.pallas import tpu as pltpu

f32 = jnp.float32
bf16 = jnp.bfloat16

D_MODEL = 2048
N_HEADS = 16
N_KV = 4
GROUP = 4
HEAD_DIM = 64
NSA_DIM = N_HEADS * HEAD_DIM
KV_DIM = N_KV * HEAD_DIM
CMP_BLOCK = 64
N_SEL = 16
WINDOW = 512
Q_BLOCK = 128
FORCED_SCORE = 1e3
RW_HEADS = 16
RW_N = 64
RW_DIM = RW_HEADS * RW_N
W_LORA = 96
A_LORA = 96
G_LORA = 256
D_FF = 5632
NORM_EPS = 1e-6
GN_EPS = 64e-5
NEG_BIG = -1e30

VMEM_LIMIT = 56 * 1024 * 1024

_SEGS = (
    ("r", 2608, 1024, 1024),
    ("k", 3728, 1024, 1024),
    ("v", 4752, 1024, 1024),
    ("q", 0, 1024, 1024),
    ("g_a", 6128, 2048, 2048),
    ("g_b", 8176, 2048, 2048),
    ("kc", 1024, 256, 256),
    ("vc", 1280, 256, 256),
    ("ks", 1536, 256, 256),
    ("vs", 1792, 256, 256),
    ("kw", 2048, 256, 256),
    ("vw", 2304, 256, 256),
    ("gl", 5872, 256, 256),
    ("ga", 2560, 48, 128),
    ("wl", 3632, 96, 128),
    ("al", 5776, 96, 128),
)
_Z_COLS = 10752


def _seg_offsets():
    offs, o = {}, 0
    for name, _, w, pw in _SEGS:
        assert o % pw == 0
        offs[name] = (o, w, pw)
        o += pw
    return offs, o


_OFF, _USED = _seg_offsets()


_IN_COLS = 10224
_RW_START = 2608
_RW_COLS = 3520


def _pad_cols(x):
    parts = []
    for _, s, w, pw in _SEGS:
        seg = x[..., s:s + w]
        if pw != w:
            seg = jnp.pad(seg, [(0, 0)] * (x.ndim - 1) + [(0, pw - w)])
        parts.append(seg)
    parts.append(jnp.zeros(x.shape[:-1] + (_Z_COLS - _USED,), x.dtype))
    return jnp.concatenate(parts, axis=-1)


def _pad_rw_cols(x):
    pad = [(0, 0)] * (x.ndim - 1) + [(_RW_START, _IN_COLS - _RW_START - _RW_COLS)]
    return _pad_cols(jnp.pad(x, pad))


def _seg(z, name):
    o, w, _ = _OFF[name]
    return z[..., o:o + w]


def _rw_cols(z):
    return jnp.concatenate([_seg(z, n) for n in ("r", "wl", "k", "v", "al", "gl")], axis=-1)


def _cparams(sem, vmem=VMEM_LIMIT):
    return pltpu.CompilerParams(dimension_semantics=sem, vmem_limit_bytes=vmem)


def _row_tile(m, pref):
    return pref if m % pref == 0 else m


def _rms(x, g):
    ms = jnp.mean(x * x, axis=-1, keepdims=True)
    return x * lax.rsqrt(ms + NORM_EPS) * g


def _ffn_kernel(x_ref, g_ref, wg_ref, wu_ref, wd_ref, o_ref, h_ref, acc_ref):
    j = pl.program_id(1)

    @pl.when(j == 0)
    def _():
        h_ref[...] = _rms(x_ref[...], g_ref[...]).astype(bf16)
        acc_ref[...] = jnp.zeros_like(acc_ref)

    h = h_ref[...]
    g = jnp.dot(h, wg_ref[...], preferred_element_type=f32)
    u = jnp.dot(h, wu_ref[...], preferred_element_type=f32)
    a = (g * jax.nn.sigmoid(g) * u).astype(bf16)
    acc_ref[...] += jnp.dot(a, wd_ref[...], preferred_element_type=f32)

    @pl.when(j == pl.num_programs(1) - 1)
    def _():
        o_ref[...] = x_ref[...] + 0.5 * acc_ref[...]


def _ffn(x, gain, wg, wu, wd):
    m, d = x.shape
    ff = wg.shape[1]
    bm = _row_tile(m, 512)
    bf = 512
    return pl.pallas_call(
        _ffn_kernel,
        out_shape=jax.ShapeDtypeStruct((m, d), f32),
        grid=(m // bm, ff // bf),
        in_specs=[
            pl.BlockSpec((bm, d), lambda i, j: (i, 0)),
            pl.BlockSpec((1, d), lambda i, j: (0, 0)),
            pl.BlockSpec((d, bf), lambda i, j: (0, j)),
            pl.BlockSpec((d, bf), lambda i, j: (0, j)),
            pl.BlockSpec((bf, d), lambda i, j: (j, 0)),
        ],
        out_specs=pl.BlockSpec((bm, d), lambda i, j: (i, 0)),
        scratch_shapes=[pltpu.VMEM((bm, d), bf16), pltpu.VMEM((bm, d), f32)],
        compiler_params=_cparams(("parallel", "arbitrary")),
        name="ffn",
    )(x, gain.reshape(1, d), wg, wu, wd)


def _inproj_kernel(x_ref, g_ref, w_ref, o_ref, h_ref):
    @pl.when(pl.program_id(1) == 0)
    def _():
        h_ref[...] = _rms(x_ref[...], g_ref[...]).astype(bf16)

    o_ref[...] = jnp.dot(h_ref[...], w_ref[...], preferred_element_type=f32)


def _inproj(x, gain, w):
    m, d = x.shape
    n = w.shape[1]
    bm = _row_tile(m, 1024)
    bn = 1536
    assert n % bn == 0
    return pl.pallas_call(
        _inproj_kernel,
        out_shape=jax.ShapeDtypeStruct((m, n), f32),
        grid=(m // bm, n // bn),
        in_specs=[
            pl.BlockSpec((bm, d), lambda i, j: (i, 0)),
            pl.BlockSpec((1, d), lambda i, j: (0, 0)),
            pl.BlockSpec((d, bn), lambda i, j: (0, j)),
        ],
        out_specs=pl.BlockSpec((bm, bn), lambda i, j: (i, j)),
        scratch_shapes=[pltpu.VMEM((bm, d), bf16)],
        compiler_params=_cparams(("parallel", "arbitrary")),
        name="inproj",
    )(x, gain.reshape(1, d), w)


def _merge_kernel(oa_ref, ob_ref, ga_ref, gb_ref, x_ref, wpa_ref, wpb_ref, wo_ref, o_ref):
    pa = jnp.dot(oa_ref[...], wpa_ref[...], preferred_element_type=f32)
    pb = jnp.dot(ob_ref[...], wpb_ref[...], preferred_element_type=f32)
    mix = jax.nn.sigmoid(ga_ref[...]) * pa + jax.nn.sigmoid(gb_ref[...]) * pb
    o_ref[...] = x_ref[...] + jnp.dot(mix.astype(bf16), wo_ref[...], preferred_element_type=f32)


def _merge(oa, ob, z, x, wpa, wpb, wo):
    m, d = x.shape
    bm = _row_tile(m, 256)
    ca = _OFF["g_a"][0] // d
    cb = _OFF["g_b"][0] // d
    return pl.pallas_call(
        _merge_kernel,
        out_shape=jax.ShapeDtypeStruct((m, d), f32),
        grid=(m // bm,),
        in_specs=[
            pl.BlockSpec((bm, NSA_DIM), lambda i: (i, 0)),
            pl.BlockSpec((bm, RW_DIM), lambda i: (i, 0)),
            pl.BlockSpec((bm, d), lambda i: (i, ca)),
            pl.BlockSpec((bm, d), lambda i: (i, cb)),
            pl.BlockSpec((bm, d), lambda i: (i, 0)),
            pl.BlockSpec((NSA_DIM, d), lambda i: (0, 0)),
            pl.BlockSpec((RW_DIM, d), lambda i: (0, 0)),
            pl.BlockSpec((d, d), lambda i: (0, 0)),
        ],
        out_specs=pl.BlockSpec((bm, d), lambda i: (i, 0)),
        compiler_params=_cparams(("parallel",)),
        name="merge",
    )(oa, ob, z, z, x, wpa, wpb, wo)


def _compress_kernel(x_ref, pe_ref, w_ref, o_ref):
    nb = o_ref.shape[1]
    acc = [jnp.zeros((nb, 128), f32), jnp.zeros((nb, 128), f32)]
    for j in range(CMP_BLOCK):
        pe_j = pe_ref[pl.ds(j, 1), :]
        w_j = w_ref[j]
        for h in range(2):
            xj = x_ref[0, pl.ds(2 * j + h, nb, stride=2 * CMP_BLOCK), :] + pe_j
            acc[h] = acc[h] + jnp.dot(xj.astype(bf16), w_j, preferred_element_type=f32)
    o_ref[0] = jnp.concatenate(acc, axis=-1)


def _blockdiag2(m):
    z = jnp.zeros_like(m)
    return jnp.concatenate([jnp.concatenate([m, z], axis=2), jnp.concatenate([z, m], axis=2)], axis=1)


def _compress(x, pe, w):
    b, l, _ = x.shape
    nb = l // CMP_BLOCK
    pe_t = jnp.tile(pe, (1, 2))
    wbd = _blockdiag2(w).astype(bf16)
    return pl.pallas_call(
        _compress_kernel,
        out_shape=jax.ShapeDtypeStruct((b, nb, KV_DIM), f32),
        grid=(b,),
        in_specs=[
            pl.BlockSpec((1, 2 * l, 128), lambda i: (i, 0, 0)),
            pl.BlockSpec((CMP_BLOCK, 128), lambda i: (0, 0)),
            pl.BlockSpec((CMP_BLOCK, 128, 128), lambda i: (0, 0, 0)),
        ],
        out_specs=pl.BlockSpec((1, nb, KV_DIM), lambda i: (i, 0, 0)),
        compiler_params=_cparams(("parallel",)),
        name="compress",
    )(x.reshape(b, 2 * l, 128), pe_t, wbd)


_TK = 512
_ROWS = GROUP * Q_BLOCK
_POS_FEATS = 16


def _softmax_cols(s):
    m = jnp.max(s, axis=0, keepdims=True)
    m = jnp.where(jnp.isfinite(m), m, 0.0)
    p = jnp.exp(s - m)
    return p / jnp.maximum(jnp.sum(p, axis=0, keepdims=True), 1e-30)


def _topk_mask_t(vt, n_pick, sel):
    nb = vt.shape[0]
    bi = lax.broadcasted_iota(jnp.int32, vt.shape, 0)
    for _ in range(n_pick):
        mx = jnp.max(vt, axis=0, keepdims=True)
        idx = jnp.min(jnp.where(vt == mx, bi, nb), axis=0, keepdims=True)
        hit = bi == idx
        sel = jnp.where(hit, 1.0, sel)
        vt = jnp.where(hit, -jnp.inf, vt)
    return sel


def _nsa_kernel(q_ref, gq_ref, gt_ref, ck_ref, cvt_ref, ke_ref, vst_ref, kw_ref, vwt_ref,
                o_ref, a0w_ref, used_ref):
    kv = pl.program_id(0)
    i = pl.program_id(1)
    nb = ck_ref.shape[1]
    n_sel = min(N_SEL, nb)
    wk = WINDOW + Q_BLOCK

    lane = lax.broadcasted_iota(jnp.int32, (1, _ROWS), 1)
    grp = lane // Q_BLOCK
    tl = lane % Q_BLOCK
    slope = jnp.exp2(-0.5 * (kv * GROUP + grp + 1).astype(f32))

    @pl.when(i == 0)
    def _():
        kroww = lax.broadcasted_iota(jnp.int32, (wk, 1), 0)
        dist = tl + WINDOW - kroww
        a0w_ref[...] = jnp.where((dist >= 0) & (dist < WINDOW), slope * dist.astype(f32), -NEG_BIG)

    q_raw = q_ref[...]
    scale = HEAD_DIM ** -0.5
    parts = []
    for g in range(GROUP):
        qg = q_raw[:, g * HEAD_DIM:(g + 1) * HEAD_DIM]
        ms = jnp.sum(qg * qg, axis=-1, keepdims=True) / HEAD_DIM
        parts.append((qg * lax.rsqrt(ms + NORM_EPS) * gq_ref[...] * scale).T)
    qt = jnp.concatenate(parts, axis=1).astype(bf16)
    t0 = i * Q_BLOCK
    tok = t0 + tl

    blk = lax.broadcasted_iota(jnp.int32, (nb, 1), 0)
    blk_mid = (blk * CMP_BLOCK).astype(f32) + (CMP_BLOCK - 1) / 2
    s = jnp.dot(ck_ref[0], qt, preferred_element_type=f32)
    s = s - slope * (tok.astype(f32) - blk_mid)
    s = jnp.where(blk * CMP_BLOCK + (CMP_BLOCK - 1) <= tok, s, -jnp.inf)
    p_c = _softmax_cols(s)
    o_c = jnp.dot(cvt_ref[0], p_c.astype(bf16), preferred_element_type=f32)

    w0 = pl.multiple_of(t0, Q_BLOCK)
    s = jnp.dot(kw_ref[0, pl.ds(w0, wk), :], qt, preferred_element_type=f32) - a0w_ref[...]
    kroww = lax.broadcasted_iota(jnp.int32, (wk, 1), 0)
    s = jnp.where(t0 - WINDOW + kroww >= 0, s, -jnp.inf)
    p_w = jnp.exp(s - jnp.max(s, axis=0, keepdims=True))
    o_w = jnp.dot(vwt_ref[0, :, pl.ds(w0, wk)], p_w.astype(bf16), preferred_element_type=f32)
    o_w = o_w / jnp.maximum(jnp.sum(p_w, axis=0, keepdims=True), 1e-30)

    imp = p_c[:, 0:Q_BLOCK]
    for g in range(1, GROUP):
        imp = imp + p_c[:, g * Q_BLOCK:(g + 1) * Q_BLOCK]
    tq = t0 + lax.broadcasted_iota(jnp.int32, (1, Q_BLOCK), 1)
    cur = tq // CMP_BLOCK
    forced = (blk == 0) | (blk == cur) | (blk == cur - 1)
    imp = jnp.where(forced, -jnp.inf, jnp.where(blk * CMP_BLOCK <= tq, imp, -1.0))
    sel_t = _topk_mask_t(imp, max(n_sel - 3, 0), forced.astype(f32))
    bias_t = jnp.where(sel_t > 0.0, 0.0, NEG_BIG).astype(bf16)
    bpt = _TK // CMP_BLOCK
    for j in range(nb // bpt):
        used_ref[j] = jnp.max(sel_t[j * bpt:(j + 1) * bpt, :]).astype(jnp.int32)
    s1 = slope.astype(bf16).astype(f32)
    s2 = (slope - s1).astype(bf16).astype(f32)
    s3 = ((slope - s1) - s2).astype(bf16).astype(f32)
    fr = lax.broadcasted_iota(jnp.int32, (_POS_FEATS, 1), 0)
    pieces = jnp.where(fr < 2, s1, jnp.where(fr < 4, s2, jnp.where(fr < 6, s3, 0.0))).astype(bf16)
    rhs = jnp.concatenate([qt, jnp.concatenate([bias_t] * GROUP, axis=1), pieces], axis=0)

    def tile(j, carry, causal):
        m, l, acc = carry
        c0 = pl.multiple_of(j * _TK, _TK)
        s = jnp.dot(ke_ref[0, pl.ds(c0, _TK), :], rhs, preferred_element_type=f32)
        if causal:
            krow = lax.broadcasted_iota(jnp.int32, (_TK, 1), 0)
            s = jnp.where(tok - (c0 + krow) >= 0, s, -jnp.inf)
        off = slope * (tok - c0).astype(f32)
        m_new = jnp.maximum(m, jnp.max(s, axis=0, keepdims=True) - off)
        p = jnp.exp(s - (m_new + off))
        alpha = jnp.exp(m - m_new)
        l = alpha * l + jnp.sum(p, axis=0, keepdims=True)
        acc = alpha * acc + jnp.dot(vst_ref[0, :, pl.ds(c0, _TK)], p.astype(bf16),
                                    preferred_element_type=f32)
        return m_new, l, acc

    jd = t0 // _TK
    init = (jnp.full((1, _ROWS), -jnp.inf, f32), jnp.zeros((1, _ROWS), f32),
            jnp.zeros((HEAD_DIM, _ROWS), f32))
    carry = lax.fori_loop(
        0, jd, lambda j, c: lax.cond(used_ref[j] > 0, lambda cc: tile(j, cc, False), lambda cc: cc, c), init)
    _, l, acc = tile(jd, carry, True)
    o_s = acc / jnp.maximum(l, 1e-30)

    gt = gt_ref[0, 0]
    out = gt[0:1] * o_c + gt[1:2] * o_s + gt[2:3] * o_w
    o_ref[...] = jnp.concatenate([out[:, g * Q_BLOCK:(g + 1) * Q_BLOCK].T for g in range(GROUP)],
                                 axis=1).astype(o_ref.dtype)


def _nsa_prompt(z, g_q, gates, ck, cv, ks, vs, kw, vw):
    t = z.shape[0]
    nqb = t // Q_BLOCK
    nb = ck.shape[0]
    qcol = _OFF["q"][0] // (GROUP * HEAD_DIM)
    gt = gates.reshape(nqb, Q_BLOCK, N_KV, GROUP, 3).transpose(2, 0, 4, 3, 1).reshape(N_KV, nqb, 3, _ROWS)
    ckh = ck.astype(bf16).transpose(1, 0, 2)
    cvt = cv.astype(bf16).transpose(1, 2, 0)
    pos = jnp.arange(t)
    onehot = (pos[:, None] // CMP_BLOCK == jnp.arange(nb)[None, :]).astype(bf16)
    off_hi = (pos % _TK) // 16 * 16
    off_lo = pos % 16
    feats = jnp.stack([off_hi, off_lo] * 3 + [jnp.zeros_like(pos)] * (_POS_FEATS - 6), axis=1).astype(bf16)
    ke = jnp.concatenate([ks.astype(bf16).transpose(1, 0, 2),
                          jnp.broadcast_to(jnp.concatenate([onehot, feats], axis=1)[None],
                                           (N_KV, t, nb + _POS_FEATS))], axis=-1)
    vst = vs.astype(bf16).transpose(1, 2, 0)
    kwh = jnp.pad(kw.astype(bf16).transpose(1, 0, 2), ((0, 0), (WINDOW, 0), (0, 0)))
    vwt = jnp.pad(vw.astype(bf16).transpose(1, 2, 0), ((0, 0), (0, 0), (WINDOW, 0)))
    kd = HEAD_DIM + nb + _POS_FEATS
    return pl.pallas_call(
        _nsa_kernel,
        out_shape=jax.ShapeDtypeStruct((t, NSA_DIM), bf16),
        grid=(N_KV, nqb),
        in_specs=[
            pl.BlockSpec((Q_BLOCK, GROUP * HEAD_DIM), lambda k, i: (i, qcol + k)),
            pl.BlockSpec((1, HEAD_DIM), lambda k, i: (0, 0)),
            pl.BlockSpec((1, 1, 3, _ROWS), lambda k, i: (k, i, 0, 0)),
            pl.BlockSpec((1, nb, HEAD_DIM), lambda k, i: (k, 0, 0)),
            pl.BlockSpec((1, HEAD_DIM, nb), lambda k, i: (k, 0, 0)),
            pl.BlockSpec((1, t, kd), lambda k, i: (k, 0, 0)),
            pl.BlockSpec((1, HEAD_DIM, t), lambda k, i: (k, 0, 0)),
            pl.BlockSpec((1, t + WINDOW, HEAD_DIM), lambda k, i: (k, 0, 0)),
            pl.BlockSpec((1, HEAD_DIM, t + WINDOW), lambda k, i: (k, 0, 0)),
        ],
        out_specs=pl.BlockSpec((Q_BLOCK, GROUP * HEAD_DIM), lambda k, i: (i, k)),
        scratch_shapes=[pltpu.VMEM((WINDOW + Q_BLOCK, _ROWS), f32),
                        pltpu.SMEM((max(nb // (_TK // CMP_BLOCK), 1),), jnp.int32)],
        compiler_params=_cparams(("arbitrary", "arbitrary")),
        name="nsa_prompt",
    )(z, g_q.reshape(1, HEAD_DIM), gt, ckh, cvt, ke, vst, kwh, vwt)


def _pair_sum(x, low):
    s_lo = jnp.sum(jnp.where(low, x, 0.0), axis=-1, keepdims=True)
    s_hi = jnp.sum(jnp.where(low, 0.0, x), axis=-1, keepdims=True)
    return jnp.where(low, s_lo, s_hi)


def _rw_prep_kernel(shift_rows, *refs):
    cur, refs = refs[:6], refs[6:]
    prv, refs = refs[:6], refs[6:]
    if shift_rows:
        st0, refs = refs[:6], refs[6:]
    mus, refs = refs[:6], refs[6:]
    (w0_ref, w2_ref, a0_ref, a2_ref, g2_ref, kk_ref, ka_ref, rk_ref,
     ro_ref, ko_ref, vo_ref, kko_ref, kao_ref, ldo_ref, go_ref, bo_ref) = refs
    first = pl.program_id(0) == 0

    def shifted(n):
        x = cur[n][...]
        if shift_rows:
            prev_row = jnp.where(first, st0[n][pl.ds(7, 1), :], prv[n][pl.ds(7, 1), :])
            rolled = pltpu.roll(x, 1, axis=0)
            rid = lax.broadcasted_iota(jnp.int32, x.shape, 0)
            xp = jnp.where(rid == 0, prev_row, rolled)
        else:
            xp = prv[n][...]
        return x + (xp - x) * mus[n][...]

    r, k, v, gl, wl, al = (shifted(n) for n in range(6))

    y = -(w0_ref[...] + jnp.dot(jnp.tanh(wl).astype(bf16), w2_ref[...], preferred_element_type=f32))
    softplus = jnp.maximum(y, 0.0) + jnp.log1p(jnp.exp(-jnp.abs(y)))
    w_log = -softplus - 0.5
    ld = -jnp.exp(w_log)
    a = jax.nn.sigmoid(a0_ref[...] + jnp.dot(al.astype(bf16), a2_ref[...], preferred_element_type=f32))
    g = jnp.dot(jax.nn.sigmoid(gl).astype(bf16), g2_ref[...], preferred_element_type=f32)
    kk = k * kk_ref[...]
    k2 = k * (1.0 + (a - 1.0) * ka_ref[...])
    rkr = r * k2 * rk_ref[...]
    low = lax.broadcasted_iota(jnp.int32, (1, 2 * RW_N), 1) < RW_N
    for hp in range(RW_HEADS // 2):
        sl = slice(hp * 2 * RW_N, (hp + 1) * 2 * RW_N)
        kkp = kk[:, sl]
        kkp = kkp / jnp.maximum(jnp.sqrt(_pair_sum(kkp * kkp, low)), 1e-12)
        ro_ref[hp] = r[:, sl]
        ko_ref[hp] = k2[:, sl]
        vo_ref[hp] = v[:, sl]
        kko_ref[hp] = kkp
        kao_ref[hp] = kkp * a[:, sl]
        ldo_ref[hp] = ld[:, sl]
        go_ref[hp] = g[:, sl]
        bo_ref[hp] = _pair_sum(rkr[:, sl], low) * v[:, sl]


def _rw_prep(z, prev, mu_p, w0, w2p, a0, a2p, g2, k_k, k_a, r_k, shift_rows):
    m = z.shape[0]
    tm = _row_tile(m, 256)
    names = ("r", "k", "v", "gl", "wl", "al")

    def col_spec(name, rows, imap):
        o, _, pw = _OFF[name]
        return pl.BlockSpec((rows, pw), functools.partial(imap, o // pw))

    cur = [col_spec(n, tm, lambda c, i: (i, c)) for n in names]
    if shift_rows:
        blk8 = tm // 8
        prv = [col_spec(n, 8, lambda c, i: (jnp.maximum(i * blk8 - 1, 0), c)) for n in names]
        prv += [col_spec(n, 8, lambda c, i: (0, c)) for n in names]
        prev_args = [z] * 6 + [prev] * 6
    else:
        prv = [col_spec(n, tm, lambda c, i: (i, c)) for n in names]
        prev_args = [prev] * 6
    mus = [col_spec(n, 1, lambda c, i: (0, c)) for n in names]
    vec = pl.BlockSpec((1, RW_DIM), lambda i: (0, 0))
    out_spec = pl.BlockSpec((RW_HEADS // 2, tm, 2 * RW_N), lambda i: (0, i, 0))
    outs = pl.pallas_call(
        functools.partial(_rw_prep_kernel, shift_rows),
        out_shape=[jax.ShapeDtypeStruct((RW_HEADS // 2, m, 2 * RW_N), f32)] * 8,
        grid=(m // tm,),
        in_specs=cur + prv + mus + [
            vec,
            pl.BlockSpec((128, RW_DIM), lambda i: (0, 0)),
            vec,
            pl.BlockSpec((128, RW_DIM), lambda i: (0, 0)),
            pl.BlockSpec((G_LORA, RW_DIM), lambda i: (0, 0)),
            vec, vec, vec,
        ],
        out_specs=[out_spec] * 8,
        compiler_params=_cparams(("parallel",)),
        name="rw_prep",
    )(*([z] * 6), *prev_args, *([mu_p] * 6),
      w0.reshape(1, RW_DIM), w2p, a0.reshape(1, RW_DIM), a2p, g2,
      k_k.reshape(1, RW_DIM), k_a.reshape(1, RW_DIM), r_k.reshape(1, RW_DIM))
    return outs


_CH = 64
_HEADS_PER_STEP = 16
_NN = (((1,), (0,)), ((), ()))
_NT = (((1,), (1,)), ((), ()))
_TN = (((0,), (0,)), ((), ()))


def _split2(x):
    hi = x.astype(bf16)
    lo = (x - hi.astype(f32)).astype(bf16)
    return hi, lo


def _dot3(a, b, dims=_NN):
    ah, al = _split2(a)
    bh, bl = _split2(b)
    d = functools.partial(lax.dot_general, dimension_numbers=dims, preferred_element_type=f32)
    return d(ah, bh) + (d(ah, bl) + d(al, bh))


def _split3(x):
    x1 = x.astype(bf16)
    r1 = x - x1.astype(f32)
    x2 = r1.astype(bf16)
    return x1, x2, (r1 - x2.astype(f32)).astype(bf16)


def _dot_exact_rhs(a01, b):
    b1, b2, b3 = _split3(b)
    d = functools.partial(jnp.dot, preferred_element_type=f32)
    return d(a01, b1) + (d(a01, b2) + d(a01, b3))


def _dot_exact_lhs_tn(a, b01):
    a1, a2, a3 = _split3(a)
    d = functools.partial(lax.dot_general, dimension_numbers=_TN, preferred_element_type=f32)
    return d(a1, b01) + (d(a2, b01) + d(a3, b01))


def _wkv_chunk_kernel(r_ref, k_ref, v_ref, kk_ref, ka_ref, ld_ref, s0_ref, y_ref, sT_ref, st_ref):
    c = pl.program_id(0)

    @pl.when(c == 0)
    def _():
        st_ref[...] = s0_ref[...]

    ti = lax.broadcasted_iota(jnp.int32, (_CH, _CH), 0)
    si = lax.broadcasted_iota(jnp.int32, (_CH, _CH), 1)
    incl = ti >= si
    strict = ti > si
    l_incl = incl.astype(bf16)
    eye = (ti == si).astype(f32)
    ones = jnp.ones((_CH, RW_N), bf16)

    def each(f, *lists):
        return [f(*a) for a in zip(*lists)]

    def cat0(a, b):
        return jnp.concatenate([a, b], axis=0)

    def cat1(a, b):
        return jnp.concatenate([a, b], axis=1)

    def heads_step(heads):
        def load(ref):
            return [ref[hp, :, sub * RW_N:(sub + 1) * RW_N] for hp, sub in heads]

        r, k, v, kk, ka, ld = (load(ref) for ref in (r_ref, k_ref, v_ref, kk_ref, ka_ref, ld_ref))
        lp = each(lambda x: _dot_exact_rhs(l_incl, x), ld)
        lp_end = each(lambda x: _dot_exact_lhs_tn(x, ones), ld)
        e_neg = each(lambda x: jnp.exp(-x), lp)
        at = each(lambda a, x, y: -a * jnp.exp(x - y), kk, lp, ld)
        bt = each(jnp.multiply, ka, e_neg)
        kt = each(jnp.multiply, k, e_neg)
        rt = each(lambda a, x: a * jnp.exp(x), r, lp)
        e_end = each(lambda x: jnp.exp(x[_CH - 1:_CH, :] - x), lp)
        bh = each(jnp.multiply, ka, e_end)
        kh = each(jnp.multiply, k, e_end)
        sc = each(lambda a, b: _dot3(a, b, _NT), each(cat0, at, rt), each(cat0, bt, kt))
        a_b = each(lambda x: jnp.where(strict, x[:_CH, :_CH], 0.0), sc)
        a_k = each(lambda x: jnp.where(strict, x[:_CH, _CH:], 0.0), sc)
        g_b = each(lambda x: jnp.where(incl, x[_CH:, :_CH], 0.0), sc)
        g_k = each(lambda x: jnp.where(incl, x[_CH:, _CH:], 0.0), sc)
        tm = each(lambda x: eye + x, a_b)
        pw = a_b
        for _ in range(5):
            pw = each(_dot3, pw, pw)
            tm = each(lambda t_, p_: t_ + _dot3(t_, p_), tm, pw)
        akv = each(_dot3, a_k, v)
        tx = each(_dot3, tm, each(cat1, at, akv))
        st = [st_ref[2 * hp + sub] for hp, sub in heads]
        ws = each(_dot3, each(lambda x, y: cat0(x[:, :RW_N], y), tx, rt), st)
        u = each(lambda a, x: a[:_CH] + x[:, RW_N:], ws, tx)
        uv = each(cat0, u, v)
        y = each(lambda a, gb, gk, x: a[_CH:] + _dot3(cat1(gb, gk), x), ws, g_b, g_k, uv)
        st_new = each(lambda e, s_, b_, k_, x: jnp.exp(e) * s_ + _dot3(cat0(b_, k_), x, _TN),
                      lp_end, st, bh, kh, uv)
        for (hp, sub), s_ in zip(heads, st_new):
            st_ref[2 * hp + sub] = s_
        return y

    def pairs(i, carry):
        heads = [(_HEADS_PER_STEP // 2 * i + a, sub) for a in range(_HEADS_PER_STEP // 2) for sub in range(2)]
        y = heads_step(heads)
        for a in range(_HEADS_PER_STEP // 2):
            y_ref[_HEADS_PER_STEP // 2 * i + a] = cat1(y[2 * a], y[2 * a + 1])
        return carry

    lax.fori_loop(0, RW_HEADS // _HEADS_PER_STEP, pairs, 0)

    @pl.when(c == pl.num_programs(0) - 1)
    def _():
        sT_ref[...] = st_ref[...]


def _wkv_chunks(r, k, v, kk, ka, ld, s0t):
    t = r.shape[1]
    spec = pl.BlockSpec((RW_HEADS // 2, _CH, 2 * RW_N), lambda c: (0, c, 0))
    sspec = pl.BlockSpec((RW_HEADS, RW_N, RW_N), lambda c: (0, 0, 0))
    return pl.pallas_call(
        _wkv_chunk_kernel,
        out_shape=[jax.ShapeDtypeStruct((RW_HEADS // 2, t, 2 * RW_N), f32),
                   jax.ShapeDtypeStruct((RW_HEADS, RW_N, RW_N), f32)],
        grid=(t // _CH,),
        in_specs=[spec] * 6 + [sspec],
        out_specs=[spec, sspec],
        scratch_shapes=[pltpu.VMEM((RW_HEADS, RW_N, RW_N), f32)],
        compiler_params=_cparams(("arbitrary",)),
        name="wkv_chunks",
    )(r, k, v, kk, ka, ld, s0t)


def _rw_post_kernel(y_ref, b_ref, g_ref, lw_ref, lb_ref, o_ref):
    low = lax.broadcasted_iota(jnp.int32, (1, 2 * RW_N), 1) < RW_N
    for hp in range(RW_HEADS // 2):
        y = y_ref[hp]
        mu = _pair_sum(y, low) / RW_N
        var = _pair_sum(jnp.square(y - mu), low) / RW_N
        yn = (y - mu) * lax.rsqrt(var + GN_EPS) * lw_ref[hp] + lb_ref[hp]
        o_ref[:, hp * 2 * RW_N:(hp + 1) * 2 * RW_N] = ((yn + b_ref[hp]) * g_ref[hp]).astype(o_ref.dtype)


def _rw_post(y, bonus, g, ln_w, ln_b):
    m = y.shape[1]
    tm = _row_tile(m, 512)
    spec = pl.BlockSpec((RW_HEADS // 2, tm, 2 * RW_N), lambda i: (0, i, 0))
    pspec = pl.BlockSpec((RW_HEADS // 2, 1, 2 * RW_N), lambda i: (0, 0, 0))
    return pl.pallas_call(
        _rw_post_kernel,
        out_shape=jax.ShapeDtypeStruct((m, RW_DIM), bf16),
        grid=(m // tm,),
        in_specs=[spec, spec, spec, pspec, pspec],
        out_specs=pl.BlockSpec((tm, RW_DIM), lambda i: (i, 0)),
        compiler_params=_cparams(("parallel",)),
        name="rw_post",
    )(y, bonus, g, ln_w.reshape(RW_HEADS // 2, 1, 2 * RW_N), ln_b.reshape(RW_HEADS // 2, 1, 2 * RW_N))


def _lora_pad(w):
    return jnp.pad(w, ((0, 128 - w.shape[0]), (0, 0))).astype(bf16)


def _rwkv_prompt(z, shift0_p, s0t, p):
    r, k, v, kk, ka, ld, g, bonus = _rw_prep(
        z, shift0_p, p["mu_p"], p["rw_w0"], p["w2p"], p["rw_a0"], p["a2p"], p["g2"],
        p["rw_k_k"], p["rw_k_a"], p["rw_r_k"], True)
    y, st = _wkv_chunks(r, k, v, kk, ka, ld, s0t)
    return _rw_post(y, bonus, g, p["rw_ln_w"], p["rw_ln_b"]), st


def _wkv_step_kernel(s_ref, r_ref, k_ref, vc_ref, kk_ref, ka_ref, ld_ref, y_ref, so_ref):
    s = s_ref[0]
    kk = kk_ref[0][:, None, :]
    sa = jnp.sum(s * (-kk), axis=-1, keepdims=True)
    s = (s * jnp.exp(ld_ref[0])[:, None, :] + sa * ka_ref[0][:, None, :]
         + vc_ref[0] * k_ref[0][:, None, :])
    so_ref[0] = s
    y_ref[0] = jnp.sum(s * r_ref[0][:, None, :], axis=-1, keepdims=True)


def _wkv_step(s0, r, k, v, kk, ka, ld):
    b = s0.shape[0]
    sspec = pl.BlockSpec((1, RW_HEADS, RW_N, RW_N), lambda i: (i, 0, 0, 0))
    vspec = pl.BlockSpec((1, RW_HEADS, RW_N), lambda i: (i, 0, 0))
    cspec = pl.BlockSpec((1, RW_HEADS, RW_N, 1), lambda i: (i, 0, 0, 0))
    y, s1 = pl.pallas_call(
        _wkv_step_kernel,
        out_shape=[jax.ShapeDtypeStruct((b, RW_HEADS, RW_N, 1), f32),
                   jax.ShapeDtypeStruct(s0.shape, f32)],
        grid=(b,),
        in_specs=[sspec, vspec, vspec, cspec, vspec, vspec, vspec],
        out_specs=[cspec, sspec],
        compiler_params=_cparams(("parallel",)),
        name="wkv_step",
    )(s0, r, k, v[..., None], kk, ka, ld)
    return y[..., 0], s1


_GP = 8


_PAGE_ROWS = N_KV * HEAD_DIM


def _compress_pages_kernel(npg, pt_ref, pool_ref, pe_ref, w_ref, o_ref, buf0, buf1, sem):
    g = pl.program_id(0)
    n_steps = pl.num_programs(0)

    def page_copies(seq, buf, slot):
        return [pltpu.make_async_copy(pool_ref.at[pt_ref[seq, pg]],
                                      buf.at[pl.ds(pg * _PAGE_ROWS, _PAGE_ROWS)], sem.at[slot])
                for pg in range(npg)]

    def start(seq, buf, slot):
        for cp in page_copies(seq, buf, slot):
            cp.start()

    def wait(seq, buf, slot):
        for cp in page_copies(seq, buf, slot):
            cp.wait()

    def compress(buf, out_slot):
        acc = jnp.zeros((npg * N_KV, 2 * CMP_BLOCK), f32)
        for d2 in range(HEAD_DIM // 2):
            x = jnp.concatenate(
                [buf[pl.ds(2 * d2 + e, npg * N_KV, stride=HEAD_DIM), :] + pe_ref[pl.ds(2 * d2 + e, 1), :]
                 for e in range(2)], axis=1)
            acc = acc + jnp.dot(x.astype(bf16), w_ref[d2], preferred_element_type=f32)
        o_ref[out_slot] = acc

    @pl.when(g == 0)
    def _():
        start(0, buf0, 0)

    start(2 * g + 1, buf1, 1)
    wait(2 * g, buf0, 0)
    compress(buf0, 0)

    @pl.when(g + 1 < n_steps)
    def _():
        start(2 * g + 2, buf0, 0)

    wait(2 * g + 1, buf1, 1)
    compress(buf1, 1)


def _compress_pages(pool_t, table, pe, w):
    b, npg = table.shape
    page = pool_t.shape[-1]
    assert b % 2 == 0 and page == 2 * CMP_BLOCK
    pe_t = jnp.tile(pe.T, (1, 2))
    wd = _blockdiag2(w.transpose(1, 0, 2)).reshape(HEAD_DIM // 2, 2 * page, page).astype(bf16)
    out = pl.pallas_call(
        functools.partial(_compress_pages_kernel, npg),
        out_shape=jax.ShapeDtypeStruct((b, npg * N_KV, page), f32),
        grid_spec=pltpu.PrefetchScalarGridSpec(
            num_scalar_prefetch=1,
            grid=(b // 2,),
            in_specs=[
                pl.BlockSpec(memory_space=pl.ANY),
                pl.BlockSpec((HEAD_DIM, page), lambda i, pt: (0, 0)),
                pl.BlockSpec((HEAD_DIM // 2, 2 * page, page), lambda i, pt: (0, 0, 0)),
            ],
            out_specs=pl.BlockSpec((2, npg * N_KV, page), lambda i, pt: (i, 0, 0)),
            scratch_shapes=[pltpu.VMEM((npg * _PAGE_ROWS, page), f32),
                            pltpu.VMEM((npg * _PAGE_ROWS, page), f32),
                            pltpu.SemaphoreType.DMA((2,))]),
        compiler_params=_cparams(("arbitrary",)),
        name="compress_pages",
    )(table, pool_t, pe_t, wd)
    out = out.reshape(b, npg, N_KV, 2, HEAD_DIM).transpose(0, 1, 3, 2, 4)
    return out.reshape(b, 2 * npg, N_KV, HEAD_DIM)


def _dec_cmp_kernel(n_pick, t_pos, q_ref, ckt_ref, cv_ref, xk_ref, xv_ref, wk_ref, wv_ref, gkc_ref,
                    oc_ref, idx_ref):
    sb = q_ref.shape[0]
    nb = ckt_ref.shape[-1]
    kd = xk_ref.shape[-1]
    blk = lax.broadcasted_iota(jnp.int32, (1, nb), 1)
    blk_mid = (blk * CMP_BLOCK).astype(f32) + (CMP_BLOCK - 1) / 2
    ck_new = _rms(jnp.dot(xk_ref[...].reshape(sb * _GP, kd).astype(bf16), wk_ref[...],
                          preferred_element_type=f32), gkc_ref[...])
    cv_new = jnp.dot(xv_ref[...].reshape(sb * _GP, kd).astype(bf16), wv_ref[...],
                     preferred_element_type=f32)
    new_mid = float(nb * CMP_BLOCK) + (CMP_BLOCK - 1) / 2
    new_ok = nb * CMP_BLOCK + (CMP_BLOCK - 1) <= t_pos
    lane = lax.broadcasted_iota(jnp.int32, (1, 128), 1)
    imps = []
    for r in range(sb):
        for kv in range(N_KV):
            row = r * _GP + kv
            q = q_ref[r, kv]
            g1 = lax.broadcasted_iota(jnp.int32, (_GP, 1), 0) + (kv * GROUP + 1)
            slope = jnp.exp2(-0.5 * g1.astype(f32))
            s = jnp.dot(q.astype(bf16), ckt_ref[r, kv], preferred_element_type=f32)
            s = s - slope * (float(t_pos) - blk_mid)
            s = jnp.where(blk * CMP_BLOCK + (CMP_BLOCK - 1) <= t_pos, s, -jnp.inf)
            qn = q.astype(bf16).astype(f32)
            s_new = jnp.sum(qn * ck_new[row:row + 1].astype(bf16).astype(f32), axis=-1, keepdims=True)
            s_new = s_new - slope * (float(t_pos) - new_mid)
            s_new = jnp.where(new_ok, s_new, -jnp.inf)
            m = jnp.maximum(jnp.max(s, axis=-1, keepdims=True), s_new)
            m = jnp.where(jnp.isfinite(m), m, 0.0)
            p = jnp.exp(s - m)
            p_new = jnp.exp(s_new - m)
            den = jnp.maximum(jnp.sum(p, axis=-1, keepdims=True) + p_new, 1e-30)
            p = p / den
            p_new = p_new / den
            oc = jnp.dot(p.astype(bf16), cv_ref[r, kv], preferred_element_type=f32)
            oc_ref[r, kv] = oc + p_new * cv_new[row:row + 1]
            imps.append(jnp.sum(p[:GROUP], axis=0, keepdims=True))
    imp = jnp.concatenate(imps, axis=0)
    cur = t_pos // CMP_BLOCK
    forced = (blk == 0) | (blk == cur) | (blk == cur - 1)
    v = jnp.where(forced, FORCED_SCORE, jnp.where(blk * CMP_BLOCK <= t_pos, imp, -1.0))
    out = jnp.zeros((sb * N_KV, 128), jnp.int32)
    for it in range(n_pick):
        mx = jnp.max(v, axis=-1, keepdims=True)
        idx = jnp.min(jnp.where(v == mx, blk, nb), axis=-1, keepdims=True)
        out = jnp.where(lane == it, idx, out)
        v = jnp.where(blk == idx, -jnp.inf, v)
    idx_ref[...] = out.reshape(sb, N_KV, 128)


def _dec_cmp(q, ck, cv, xk, xv, wk, wv, g_kc, t_pos, n_pick):
    b, nb = ck.shape[:2]
    ckt = ck.astype(bf16).transpose(0, 2, 3, 1)
    cvh = cv.astype(bf16).transpose(0, 2, 1, 3)
    kd = CMP_BLOCK * HEAD_DIM
    sb = 8 if b % 8 == 0 else b
    return pl.pallas_call(
        functools.partial(_dec_cmp_kernel, n_pick, t_pos),
        out_shape=[jax.ShapeDtypeStruct((b, N_KV, _GP, HEAD_DIM), f32),
                   jax.ShapeDtypeStruct((b, N_KV, 128), jnp.int32)],
        grid=(b // sb,),
        in_specs=[
            pl.BlockSpec((sb, N_KV, _GP, HEAD_DIM), lambda i: (i, 0, 0, 0)),
            pl.BlockSpec((sb, N_KV, HEAD_DIM, nb), lambda i: (i, 0, 0, 0)),
            pl.BlockSpec((sb, N_KV, nb, HEAD_DIM), lambda i: (i, 0, 0, 0)),
            pl.BlockSpec((sb, _GP, kd), lambda i: (i, 0, 0)),
            pl.BlockSpec((sb, _GP, kd), lambda i: (i, 0, 0)),
            pl.BlockSpec((kd, HEAD_DIM), lambda i: (0, 0)),
            pl.BlockSpec((kd, HEAD_DIM), lambda i: (0, 0)),
            pl.BlockSpec((1, HEAD_DIM), lambda i: (0, 0)),
        ],
        out_specs=[pl.BlockSpec((sb, N_KV, _GP, HEAD_DIM), lambda i: (i, 0, 0, 0)),
                   pl.BlockSpec((sb, N_KV, 128), lambda i: (i, 0, 0))],
        compiler_params=_cparams(("parallel",)),
        name="dec_cmp",
    )(q, ckt, cvh, xk, xv, wk, wv, g_kc.reshape(1, HEAD_DIM))


def _dec_sel_kernel(t_pos, n_pick, pt_ref, idx_ref, q_ref, gt_ref, oc_ref, ksn_ref, vsn_ref, kwn_ref,
                    vwn_ref, wk_ref, wv_ref, pk_ref, pv_ref, o_ref, kb0, vb0, kb1, vb1, sem):
    g = pl.program_id(0)
    n_steps = pl.num_programs(0)
    n_buf = wk_ref.shape[-1]
    page = 2 * CMP_BLOCK

    def copies(seq, kb, vb, slot):
        out = []
        for kv in range(N_KV):
            for s in range(n_pick):
                pg = pt_ref[seq, idx_ref[seq, kv * 128 + s] // 2]
                dst = pl.ds(s * page, page)
                out.append(pltpu.make_async_copy(pk_ref.at[pg, kv], kb.at[kv, :, dst], sem.at[slot]))
                out.append(pltpu.make_async_copy(pv_ref.at[pg, kv], vb.at[kv, :, dst], sem.at[slot]))
        return out

    def start(seq, kb, vb, slot):
        for cp in copies(seq, kb, vb, slot):
            cp.start()

    def wait(seq, kb, vb, slot):
        for cp in copies(seq, kb, vb, slot):
            cp.wait()

    def slopes(kv):
        g1 = lax.broadcasted_iota(jnp.int32, (_GP, 1), 0) + (kv * GROUP + 1)
        return jnp.exp2(-0.5 * g1.astype(f32))

    lane = lax.broadcasted_iota(jnp.int32, (1, page), 1)
    c = lax.broadcasted_iota(jnp.int32, (1, n_buf), 1)
    kpos = t_pos - n_buf + c
    distw = t_pos - kpos
    okw = (kpos >= 0) & (distw >= 0) & (distw < WINDOW)

    def attend(seq, r, kb, vb):
        for kv in range(N_KV):
            q = q_ref[r, kv].astype(bf16)
            qf = q.astype(f32)
            slope = slopes(kv)
            dist, ok = [], []
            for s in range(n_pick):
                blk = idx_ref[seq, kv * 128 + s]
                d = t_pos - ((blk // 2) * page + lane)
                dist.append(d)
                ok.append((lane // CMP_BLOCK == blk % 2) & (d >= 0))
            dist = jnp.concatenate(dist, axis=1)
            ok = jnp.concatenate(ok, axis=1)
            s_sel = jnp.dot(q, kb[kv].astype(bf16), preferred_element_type=f32)
            s_sel = jnp.where(ok, s_sel - slope * dist.astype(f32), -jnp.inf)
            s_new = jnp.sum(qf * ksn_ref[r, kv:kv + 1].astype(bf16).astype(f32), axis=-1, keepdims=True)
            m = jnp.maximum(jnp.max(s_sel, axis=-1, keepdims=True), s_new)
            p = jnp.exp(s_sel - m)
            p_new = jnp.exp(s_new - m)
            den = jnp.maximum(jnp.sum(p, axis=-1, keepdims=True) + p_new, 1e-30)
            o_s = lax.dot_general((p / den).astype(bf16), vb[kv].astype(bf16), _NT, preferred_element_type=f32)
            o_s = o_s + (p_new / den).astype(bf16).astype(f32) * vsn_ref[r, kv:kv + 1].astype(bf16).astype(f32)
            sw = jnp.dot(q, wk_ref[r, kv].astype(bf16), preferred_element_type=f32)
            sw = jnp.where(okw, sw - slope * distw.astype(f32), -jnp.inf)
            sw_new = jnp.sum(qf * kwn_ref[r, kv:kv + 1].astype(bf16).astype(f32), axis=-1, keepdims=True)
            mw = jnp.maximum(jnp.max(sw, axis=-1, keepdims=True), sw_new)
            pw = jnp.exp(sw - mw)
            pw_new = jnp.exp(sw_new - mw)
            denw = jnp.maximum(jnp.sum(pw, axis=-1, keepdims=True) + pw_new, 1e-30)
            o_w = lax.dot_general((pw / denw).astype(bf16), wv_ref[r, kv].astype(bf16), _NT,
                                  preferred_element_type=f32)
            o_w = o_w + (pw_new / denw).astype(bf16).astype(f32) * vwn_ref[r, kv:kv + 1].astype(bf16).astype(f32)
            gt = gt_ref[r, kv]
            o_ref[r, kv] = gt[:, 0:1] * oc_ref[r, kv] + gt[:, 1:2] * o_s + gt[:, 2:3] * o_w

    @pl.when(g == 0)
    def _():
        start(0, kb0, vb0, 0)

    start(2 * g + 1, kb1, vb1, 1)
    wait(2 * g, kb0, vb0, 0)
    attend(2 * g, 0, kb0, vb0)

    @pl.when(g + 1 < n_steps)
    def _():
        start(2 * g + 2, kb0, vb0, 0)

    wait(2 * g + 1, kb1, vb1, 1)
    attend(2 * g + 1, 1, kb1, vb1)


def _dec_sel(q, gates, o_c, idx, table, pool_k, pool_v, ks_new, vs_new, kw_new, vw_new, win_k, win_v,
             t_pos, n_pick):
    b = q.shape[0]
    assert b % 2 == 0
    n_buf = win_k.shape[-1]
    page = pool_k.shape[-1]
    hspec = pl.BlockSpec((2, N_KV, _GP, HEAD_DIM), lambda i, pt, ix: (i, 0, 0, 0))
    nspec = pl.BlockSpec((2, N_KV, HEAD_DIM), lambda i, pt, ix: (i, 0, 0))
    wspec = pl.BlockSpec((2, N_KV, HEAD_DIM, n_buf), lambda i, pt, ix: (i, 0, 0, 0))
    anyspec = pl.BlockSpec(memory_space=pl.ANY)
    gbuf = pltpu.VMEM((N_KV, HEAD_DIM, n_pick * page), f32)
    return pl.pallas_call(
        functools.partial(_dec_sel_kernel, t_pos, n_pick),
        out_shape=jax.ShapeDtypeStruct((b, N_KV, _GP, HEAD_DIM), f32),
        grid_spec=pltpu.PrefetchScalarGridSpec(
            num_scalar_prefetch=2,
            grid=(b // 2,),
            in_specs=[hspec, pl.BlockSpec((2, N_KV, _GP, 3), lambda i, pt, ix: (i, 0, 0, 0)), hspec,
                      nspec, nspec, nspec, nspec, wspec, wspec, anyspec, anyspec],
            out_specs=hspec,
            scratch_shapes=[gbuf, gbuf, gbuf, gbuf, pltpu.SemaphoreType.DMA((2,))]),
        compiler_params=_cparams(("arbitrary",)),
        name="dec_sel",
    )(table, idx, q, gates, o_c, ks_new, vs_new, kw_new, vw_new, win_k, win_v, pool_k, pool_v)


def _head_rms(x, g):
    ms = jnp.mean(x * x, axis=-1, keepdims=True)
    return x * lax.rsqrt(ms + NORM_EPS) * g


def _prepare(p):
    q = dict(p)
    for n in ("ffn1_gate", "ffn1_up", "ffn1_down", "ffn2_gate", "ffn2_up", "ffn2_down", "w_pa", "w_pb", "w_out"):
        q[n] = p[n].astype(bf16)
    q["w_in_p"] = _pad_cols(p["w_in"]).astype(bf16)
    q["mu_p"] = _pad_rw_cols(p["rw_mu"][None])
    q["w2p"] = _lora_pad(p["rw_w2"])
    q["a2p"] = _lora_pad(p["rw_a2"])
    q["g2"] = p["rw_g2"].astype(bf16)
    return q


def _nsa_proj(z, p):
    m = z.shape[0]
    kvs = (m, N_KV, HEAD_DIM)
    q = _head_rms(_seg(z, "q").reshape(m, N_HEADS, HEAD_DIM), p["g_q"])
    kc = _seg(z, "kc").reshape(kvs)
    vc = _seg(z, "vc").reshape(kvs)
    ks = _head_rms(_seg(z, "ks").reshape(kvs), p["g_ks"])
    vs = _seg(z, "vs").reshape(kvs)
    kw = _head_rms(_seg(z, "kw").reshape(kvs), p["g_kw"])
    vw = _seg(z, "vw").reshape(kvs)
    gates = jax.nn.sigmoid(_seg(z, "ga").reshape(m, N_HEADS, 3))
    return q, gates, kc, vc, ks, vs, kw, vw


def _layer_prompt(x, p):
    t = x.shape[0]
    x = _ffn(x, p["n_ffn1"], p["ffn1_gate"], p["ffn1_up"], p["ffn1_down"])
    z = _inproj(x, p["n_mix"], p["w_in_p"])
    q, gates, kc, vc, ks, vs, kw, vw = _nsa_proj(z, p)
    ck = _head_rms(_compress(kc.reshape(1, t, KV_DIM), p["pe_cmp_k"], p["w_cmp_k"])[0]
                   .reshape(-1, N_KV, HEAD_DIM), p["g_kc"])
    cv = _compress(vc.reshape(1, t, KV_DIM), p["pe_cmp_v"], p["w_cmp_v"])[0].reshape(-1, N_KV, HEAD_DIM)
    o_a = _nsa_prompt(z, p["g_q"], gates, ck, cv, ks, vs, kw, vw)
    shift0 = jnp.zeros((8, _Z_COLS), f32)
    s0t = jnp.zeros((RW_HEADS, RW_N, RW_N), f32)
    o_b, st = _rwkv_prompt(z, shift0, s0t, p)
    x = _merge(o_a, o_b, z, x, p["w_pa"], p["w_pb"], p["w_out"])
    x = _ffn(x, p["n_ffn2"], p["ffn2_gate"], p["ffn2_up"], p["ffn2_down"])
    n_buf = min(WINDOW, t)
    states = (kc, vc, ks, vs, kw[-n_buf:], vw[-n_buf:], _rw_cols(z[-1:]), st.transpose(0, 2, 1)[None])
    return x, states


def _layer_sample(x, p, past):
    b = x.shape[0]
    table = past["page_table"]
    page = past["cmp_k"].shape[1]
    t_pos = table.shape[1] * page
    x = _ffn(x, p["n_ffn1"], p["ffn1_gate"], p["ffn1_up"], p["ffn1_down"])
    z = _inproj(x, p["n_mix"], p["w_in_p"])
    q, gates, kc, vc, ks, vs, kw, vw = _nsa_proj(z, p)

    fm = lambda a: a.transpose(0, 2, 3, 1)
    pool_rows = lambda a: fm(a).reshape(-1, _PAGE_ROWS, page)

    nbp = t_pos // CMP_BLOCK
    ck = _head_rms(_compress_pages(pool_rows(past["cmp_k"]), table, p["pe_cmp_k"], p["w_cmp_k"]), p["g_kc"])
    cv = _compress_pages(pool_rows(past["cmp_v"]), table, p["pe_cmp_v"], p["w_cmp_v"])

    def new_block_rows(k_new, pe):
        first = k_new + pe[0]
        rest = jnp.broadcast_to(pe[1:].reshape(1, 1, -1), (b, N_KV, (CMP_BLOCK - 1) * HEAD_DIM))
        rows = jnp.concatenate([first, rest], axis=-1)
        return jnp.pad(rows, ((0, 0), (0, _GP - N_KV), (0, 0)))

    n_sel = min(N_SEL, nbp + 1)
    n_pick = n_sel - 1
    scale = HEAD_DIM ** -0.5
    pad_g = lambda a: jnp.pad(a, ((0, 0), (0, 0), (0, _GP - GROUP), (0, 0)))
    qh = pad_g((q * scale).reshape(b, N_KV, GROUP, HEAD_DIM))
    gth = pad_g(gates.reshape(b, N_KV, GROUP, 3))
    kd = CMP_BLOCK * HEAD_DIM
    o_c, idx = _dec_cmp(qh, ck, cv, new_block_rows(kc, p["pe_cmp_k"]), new_block_rows(vc, p["pe_cmp_v"]),
                        p["w_cmp_k"].reshape(kd, HEAD_DIM).astype(bf16),
                        p["w_cmp_v"].reshape(kd, HEAD_DIM).astype(bf16), p["g_kc"], t_pos, n_pick)
    o_a = _dec_sel(qh, gth, o_c, idx.reshape(b, N_KV * 128), table, fm(past["slc_k"]), fm(past["slc_v"]),
                   ks, vs, kw, vw, fm(past["win_k"]), fm(past["win_v"]), t_pos, n_pick)
    o_a = o_a[:, :, :GROUP].reshape(b, NSA_DIM).astype(bf16)

    prev = _pad_rw_cols(past["shift"])
    r, k, v, kk, ka, ld, g, bonus = _rw_prep(
        z, prev, p["mu_p"], p["rw_w0"], p["w2p"], p["rw_a0"], p["a2p"], p["g2"],
        p["rw_k_k"], p["rw_k_a"], p["rw_r_k"], False)
    tb = lambda a: a.transpose(1, 0, 2).reshape(b, RW_HEADS, RW_N)
    y, wkv = _wkv_step(past["wkv"], tb(r), tb(k), tb(v), tb(kk), tb(ka), tb(ld))
    y = y.reshape(b, RW_HEADS // 2, 2 * RW_N).transpose(1, 0, 2)
    o_b = _rw_post(y, bonus, g, p["rw_ln_w"], p["rw_ln_b"])

    x = _merge(o_a, o_b, z, x, p["w_pa"], p["w_pb"], p["w_out"])
    x = _ffn(x, p["n_ffn2"], p["ffn2_gate"], p["ffn2_up"], p["ffn2_down"])
    kvs = lambda a: a.reshape(b, 1, N_KV, HEAD_DIM)
    win_k = jnp.concatenate([past["win_k"][:, 1:], kvs(kw)], axis=1)
    win_v = jnp.concatenate([past["win_v"][:, 1:], kvs(vw)], axis=1)
    states = (kvs(kc), kvs(vc), kvs(ks), kvs(vs), win_k, win_v, _rw_cols(z), wkv)
    return x, states


def kernel(x_prompt, x_sample, cache_cmp_k, cache_cmp_v, cache_slc_k, cache_slc_v, cache_win_k, cache_win_v,
           state_shift, state_wkv, page_table,
           n_ffn1, ffn1_gate, ffn1_up, ffn1_down, n_mix, w_in, g_q, g_kc, g_ks, g_kw,
           w_cmp_k, pe_cmp_k, w_cmp_v, pe_cmp_v,
           rw_mu, rw_w0, rw_w2, rw_a0, rw_a2, rw_g2, rw_k_k, rw_k_a, rw_r_k, rw_ln_w, rw_ln_b,
           w_pa, w_pb, w_out, n_ffn2, ffn2_gate, ffn2_up, ffn2_down):
    assert x_prompt.shape[0] == 1 and x_sample.shape[1] == 1 and n_ffn1.shape[0] == 1
    l = 0
    p = _prepare(dict(
        n_ffn1=n_ffn1[l], ffn1_gate=ffn1_gate[l], ffn1_up=ffn1_up[l], ffn1_down=ffn1_down[l],
        n_mix=n_mix[l], w_in=w_in[l], g_q=g_q[l], g_kc=g_kc[l], g_ks=g_ks[l], g_kw=g_kw[l],
        w_cmp_k=w_cmp_k[l], pe_cmp_k=pe_cmp_k[l], w_cmp_v=w_cmp_v[l], pe_cmp_v=pe_cmp_v[l],
        rw_mu=rw_mu[l], rw_w0=rw_w0[l], rw_w2=rw_w2[l], rw_a0=rw_a0[l], rw_a2=rw_a2[l], rw_g2=rw_g2[l],
        rw_k_k=rw_k_k[l], rw_k_a=rw_k_a[l], rw_r_k=rw_r_k[l], rw_ln_w=rw_ln_w[l], rw_ln_b=rw_ln_b[l],
        w_pa=w_pa[l], w_pb=w_pb[l], w_out=w_out[l],
        n_ffn2=n_ffn2[l], ffn2_gate=ffn2_gate[l], ffn2_up=ffn2_up[l], ffn2_down=ffn2_down[l]))
    t = x_prompt.shape[1]
    y_p, sp = _layer_prompt(x_prompt[0], p)
    past = dict(page_table=page_table, cmp_k=cache_cmp_k[l], cmp_v=cache_cmp_v[l], slc_k=cache_slc_k[l],
                slc_v=cache_slc_v[l], win_k=cache_win_k[l], win_v=cache_win_v[l], shift=state_shift[l],
                wkv=state_wkv[l])
    y_s, ss = _layer_sample(x_sample[:, 0], p, past)
    kvp = lambda a: a.reshape(1, 1, -1, N_KV, HEAD_DIM)
    outs_p = (kvp(sp[0]), kvp(sp[1]), kvp(sp[2]), kvp(sp[3]), kvp(sp[4]), kvp(sp[5]), sp[6][None], sp[7][None])
    outs_s = tuple(a[None] for a in ss)
    return (y_p.reshape(1, t, D_MODEL), y_s[:, None, :]) + outs_p + outs_s
```

```python
import functools

import jax
import jax.numpy as jnp
from jax import lax
from jax.experimental import pallas as pl
from jax.experimental.pallas import tpu as pltpu

f32 = jnp.float32
bf16 = jnp.bfloat16

D_MODEL = 2048
N_HEADS = 16
N_KV = 4
GROUP = 4
HEAD_DIM = 64
NSA_DIM = N_HEADS * HEAD_DIM
KV_DIM = N_KV * HEAD_DIM
CMP_BLOCK = 64
N_SEL = 16
WINDOW = 512
Q_BLOCK = 128
FORCED_SCORE = 1e3
RW_HEADS = 16
RW_N = 64
RW_DIM = RW_HEADS * RW_N
W_LORA = 96
A_LORA = 96
G_LORA = 256
D_FF = 5632
NORM_EPS = 1e-6
GN_EPS = 64e-5
NEG_BIG = -1e30

VMEM_LIMIT = 56 * 1024 * 1024

_SEGS = (
    ("r", 2608, 1024, 1024),
    ("k", 3728, 1024, 1024),
    ("v", 4752, 1024, 1024),
    ("q", 0, 1024, 1024),
    ("g_a", 6128, 2048, 2048),
    ("g_b", 8176, 2048, 2048),
    ("kc", 1024, 256, 256),
    ("vc", 1280, 256, 256),
    ("ks", 1536, 256, 256),
    ("vs", 1792, 256, 256),
    ("kw", 2048, 256, 256),
    ("vw", 2304, 256, 256),
    ("gl", 5872, 256, 256),
    ("ga", 2560, 48, 128),
    ("wl", 3632, 96, 128),
    ("al", 5776, 96, 128),
)
_Z_COLS = 10752


def _seg_offsets():
    offs, o = {}, 0
    for name, _, w, pw in _SEGS:
        assert o % pw == 0
        offs[name] = (o, w, pw)
        o += pw
    return offs, o


_OFF, _USED = _seg_offsets()


_IN_COLS = 10224
_RW_START = 2608
_RW_COLS = 3520


def _pad_cols(x):
    parts = []
    for _, s, w, pw in _SEGS:
        seg = x[..., s:s + w]
        if pw != w:
            seg = jnp.pad(seg, [(0, 0)] * (x.ndim - 1) + [(0, pw - w)])
        parts.append(seg)
    parts.append(jnp.zeros(x.shape[:-1] + (_Z_COLS - _USED,), x.dtype))
    return jnp.concatenate(parts, axis=-1)


def _pad_rw_cols(x):
    pad = [(0, 0)] * (x.ndim - 1) + [(_RW_START, _IN_COLS - _RW_START - _RW_COLS)]
    return _pad_cols(jnp.pad(x, pad))


def _seg(z, name):
    o, w, _ = _OFF[name]
    return z[..., o:o + w]


def _rw_cols(z):
    return jnp.concatenate([_seg(z, n) for n in ("r", "wl", "k", "v", "al", "gl")], axis=-1)


def _cparams(sem, vmem=VMEM_LIMIT):
    return pltpu.CompilerParams(dimension_semantics=sem, vmem_limit_bytes=vmem)


def _row_tile(m, pref):
    return pref if m % pref == 0 else m


def _rms(x, g):
    ms = jnp.mean(x * x, axis=-1, keepdims=True)
    return x * lax.rsqrt(ms + NORM_EPS) * g


def _ffn_kernel(x_ref, g_ref, wg_ref, wu_ref, wd_ref, o_ref, h_ref, acc_ref):
    j = pl.program_id(1)

    @pl.when(j == 0)
    def _():
        h_ref[...] = _rms(x_ref[...], g_ref[...]).astype(bf16)
        acc_ref[...] = jnp.zeros_like(acc_ref)

    h = h_ref[...]
    g = jnp.dot(h, wg_ref[...], preferred_element_type=f32)
    u = jnp.dot(h, wu_ref[...], preferred_element_type=f32)
    a = (g * jax.nn.sigmoid(g) * u).astype(bf16)
    acc_ref[...] += jnp.dot(a, wd_ref[...], preferred_element_type=f32)

    @pl.when(j == pl.num_programs(1) - 1)
    def _():
        o_ref[...] = x_ref[...] + 0.5 * acc_ref[...]


def _ffn(x, gain, wg, wu, wd):
    m, d = x.shape
    ff = wg.shape[1]
    bm = _row_tile(m, 512)
    bf = 512
    return pl.pallas_call(
        _ffn_kernel,
        out_shape=jax.ShapeDtypeStruct((m, d), f32),
        grid=(m // bm, ff // bf),
        in_specs=[
            pl.BlockSpec((bm, d), lambda i, j: (i, 0)),
            pl.BlockSpec((1, d), lambda i, j: (0, 0)),
            pl.BlockSpec((d, bf), lambda i, j: (0, j)),
            pl.BlockSpec((d, bf), lambda i, j: (0, j)),
            pl.BlockSpec((bf, d), lambda i, j: (j, 0)),
        ],
        out_specs=pl.BlockSpec((bm, d), lambda i, j: (i, 0)),
        scratch_shapes=[pltpu.VMEM((bm, d), bf16), pltpu.VMEM((bm, d), f32)],
        compiler_params=_cparams(("parallel", "arbitrary")),
        name="ffn",
    )(x, gain.reshape(1, d), wg, wu, wd)


def _inproj_kernel(x_ref, g_ref, w_ref, o_ref, h_ref):
    @pl.when(pl.program_id(1) == 0)
    def _():
        h_ref[...] = _rms(x_ref[...], g_ref[...]).astype(bf16)

    o_ref[...] = jnp.dot(h_ref[...], w_ref[...], preferred_element_type=f32)


def _inproj(x, gain, w):
    m, d = x.shape
    n = w.shape[1]
    bm = _row_tile(m, 1024)
    bn = 1536
    assert n % bn == 0
    return pl.pallas_call(
        _inproj_kernel,
        out_shape=jax.ShapeDtypeStruct((m, n), f32),
        grid=(m // bm, n // bn),
        in_specs=[
            pl.BlockSpec((bm, d), lambda i, j: (i, 0)),
            pl.BlockSpec((1, d), lambda i, j: (0, 0)),
            pl.BlockSpec((d, bn), lambda i, j: (0, j)),
        ],
        out_specs=pl.BlockSpec((bm, bn), lambda i, j: (i, j)),
        scratch_shapes=[pltpu.VMEM((bm, d), bf16)],
        compiler_params=_cparams(("parallel", "arbitrary")),
        name="inproj",
    )(x, gain.reshape(1, d), w)


def _merge_kernel(oa_ref, ob_ref, ga_ref, gb_ref, x_ref, wpa_ref, wpb_ref, wo_ref, o_ref):
    pa = jnp.dot(oa_ref[...], wpa_ref[...], preferred_element_type=f32)
    pb = jnp.dot(ob_ref[...], wpb_ref[...], preferred_element_type=f32)
    mix = jax.nn.sigmoid(ga_ref[...]) * pa + jax.nn.sigmoid(gb_ref[...]) * pb
    o_ref[...] = x_ref[...] + jnp.dot(mix.astype(bf16), wo_ref[...], preferred_element_type=f32)


def _merge(oa, ob, z, x, wpa, wpb, wo):
    m, d = x.shape
    bm = _row_tile(m, 256)
    ca = _OFF["g_a"][0] // d
    cb = _OFF["g_b"][0] // d
    return pl.pallas_call(
        _merge_kernel,
        out_shape=jax.ShapeDtypeStruct((m, d), f32),
        grid=(m // bm,),
        in_specs=[
            pl.BlockSpec((bm, NSA_DIM), lambda i: (i, 0)),
            pl.BlockSpec((bm, RW_DIM), lambda i: (i, 0)),
            pl.BlockSpec((bm, d), lambda i: (i, ca)),
            pl.BlockSpec((bm, d), lambda i: (i, cb)),
            pl.BlockSpec((bm, d), lambda i: (i, 0)),
            pl.BlockSpec((NSA_DIM, d), lambda i: (0, 0)),
            pl.BlockSpec((RW_DIM, d), lambda i: (0, 0)),
            pl.BlockSpec((d, d), lambda i: (0, 0)),
        ],
        out_specs=pl.BlockSpec((bm, d), lambda i: (i, 0)),
        compiler_params=_cparams(("parallel",)),
        name="merge",
    )(oa, ob, z, z, x, wpa, wpb, wo)


def _compress_kernel(x_ref, pe_ref, w_ref, o_ref):
    nb = o_ref.shape[1]
    acc = [jnp.zeros((nb, 128), f32), jnp.zeros((nb, 128), f32)]
    for j in range(CMP_BLOCK):
        pe_j = pe_ref[pl.ds(j, 1), :]
        w_j = w_ref[j]
        for h in range(2):
            xj = x_ref[0, pl.ds(2 * j + h, nb, stride=2 * CMP_BLOCK), :] + pe_j
            acc[h] = acc[h] + jnp.dot(xj.astype(bf16), w_j, preferred_element_type=f32)
    o_ref[0] = jnp.concatenate(acc, axis=-1)


def _blockdiag2(m):
    z = jnp.zeros_like(m)
    return jnp.concatenate([jnp.concatenate([m, z], axis=2), jnp.concatenate([z, m], axis=2)], axis=1)


def _compress(x, pe, w):
    b, l, _ = x.shape
    nb = l // CMP_BLOCK
    pe_t = jnp.tile(pe, (1, 2))
    wbd = _blockdiag2(w).astype(bf16)
    return pl.pallas_call(
        _compress_kernel,
        out_shape=jax.ShapeDtypeStruct((b, nb, KV_DIM), f32),
        grid=(b,),
        in_specs=[
            pl.BlockSpec((1, 2 * l, 128), lambda i: (i, 0, 0)),
            pl.BlockSpec((CMP_BLOCK, 128), lambda i: (0, 0)),
            pl.BlockSpec((CMP_BLOCK, 128, 128), lambda i: (0, 0, 0)),
        ],
        out_specs=pl.BlockSpec((1, nb, KV_DIM), lambda i: (i, 0, 0)),
        compiler_params=_cparams(("parallel",)),
        name="compress",
    )(x.reshape(b, 2 * l, 128), pe_t, wbd)


_TK = 512
_ROWS = GROUP * Q_BLOCK
_POS_FEATS = 16


def _softmax_cols(s):
    m = jnp.max(s, axis=0, keepdims=True)
    m = jnp.where(jnp.isfinite(m), m, 0.0)
    p = jnp.exp(s - m)
    return p / jnp.maximum(jnp.sum(p, axis=0, keepdims=True), 1e-30)


def _topk_mask_t(vt, n_pick, sel):
    nb = vt.shape[0]
    bi = lax.broadcasted_iota(jnp.int32, vt.shape, 0)
    for _ in range(n_pick):
        mx = jnp.max(vt, axis=0, keepdims=True)
        idx = jnp.min(jnp.where(vt == mx, bi, nb), axis=0, keepdims=True)
        hit = bi == idx
        sel = jnp.where(hit, 1.0, sel)
        vt = jnp.where(hit, -jnp.inf, vt)
    return sel


def _nsa_kernel(q_ref, gq_ref, gt_ref, ck_ref, cvt_ref, ke_ref, vst_ref, kw_ref, vwt_ref,
                o_ref, a0w_ref, used_ref):
    kv = pl.program_id(0)
    i = pl.program_id(1)
    nb = ck_ref.shape[1]
    n_sel = min(N_SEL, nb)
    wk = WINDOW + Q_BLOCK

    lane = lax.broadcasted_iota(jnp.int32, (1, _ROWS), 1)
    grp = lane // Q_BLOCK
    tl = lane % Q_BLOCK
    slope = jnp.exp2(-0.5 * (kv * GROUP + grp + 1).astype(f32))

    @pl.when(i == 0)
    def _():
        kroww = lax.broadcasted_iota(jnp.int32, (wk, 1), 0)
        dist = tl + WINDOW - kroww
        a0w_ref[...] = jnp.where((dist >= 0) & (dist < WINDOW), slope * dist.astype(f32), -NEG_BIG)

    q_raw = q_ref[...]
    scale = HEAD_DIM ** -0.5
    parts = []
    for g in range(GROUP):
        qg = q_raw[:, g * HEAD_DIM:(g + 1) * HEAD_DIM]
        ms = jnp.sum(qg * qg, axis=-1, keepdims=True) / HEAD_DIM
        parts.append((qg * lax.rsqrt(ms + NORM_EPS) * gq_ref[...] * scale).T)
    qt = jnp.concatenate(parts, axis=1).astype(bf16)
    t0 = i * Q_BLOCK
    tok = t0 + tl

    blk = lax.broadcasted_iota(jnp.int32, (nb, 1), 0)
    blk_mid = (blk * CMP_BLOCK).astype(f32) + (CMP_BLOCK - 1) / 2
    s = jnp.dot(ck_ref[0], qt, preferred_element_type=f32)
    s = s - slope * (tok.astype(f32) - blk_mid)
    s = jnp.where(blk * CMP_BLOCK + (CMP_BLOCK - 1) <= tok, s, -jnp.inf)
    p_c = _softmax_cols(s)
    o_c = jnp.dot(cvt_ref[0], p_c.astype(bf16), preferred_element_type=f32)

    w0 = pl.multiple_of(t0, Q_BLOCK)
    s = jnp.dot(kw_ref[0, pl.ds(w0, wk), :], qt, preferred_element_type=f32) - a0w_ref[...]
    kroww = lax.broadcasted_iota(jnp.int32, (wk, 1), 0)
    s = jnp.where(t0 - WINDOW + kroww >= 0, s, -jnp.inf)
    p_w = jnp.exp(s - jnp.max(s, axis=0, keepdims=True))
    o_w = jnp.dot(vwt_ref[0, :, pl.ds(w0, wk)], p_w.astype(bf16), preferred_element_type=f32)
    o_w = o_w / jnp.maximum(jnp.sum(p_w, axis=0, keepdims=True), 1e-30)

    imp = p_c[:, 0:Q_BLOCK]
    for g in range(1, GROUP):
        imp = imp + p_c[:, g * Q_BLOCK:(g + 1) * Q_BLOCK]
    tq = t0 + lax.broadcasted_iota(jnp.int32, (1, Q_BLOCK), 1)
    cur = tq // CMP_BLOCK
    forced = (blk == 0) | (blk == cur) | (blk == cur - 1)
    imp = jnp.where(forced, -jnp.inf, jnp.where(blk * CMP_BLOCK <= tq, imp, -1.0))
    sel_t = _topk_mask_t(imp, max(n_sel - 3, 0), forced.astype(f32))
    bias_t = jnp.where(sel_t > 0.0, 0.0, NEG_BIG).astype(bf16)
    bpt = _TK // CMP_BLOCK
    for j in range(nb // bpt):
        used_ref[j] = jnp.max(sel_t[j * bpt:(j + 1) * bpt, :]).astype(jnp.int32)
    s1 = slope.astype(bf16).astype(f32)
    s2 = (slope - s1).astype(bf16).astype(f32)
    s3 = ((slope - s1) - s2).astype(bf16).astype(f32)
    fr = lax.broadcasted_iota(jnp.int32, (_POS_FEATS, 1), 0)
    pieces = jnp.where(fr < 2, s1, jnp.where(fr < 4, s2, jnp.where(fr < 6, s3, 0.0))).astype(bf16)
    rhs = jnp.concatenate([qt, jnp.concatenate([bias_t] * GROUP, axis=1), pieces], axis=0)

    def tile(j, carry, causal):
        m, l, acc = carry
        c0 = pl.multiple_of(j * _TK, _TK)
        s = jnp.dot(ke_ref[0, pl.ds(c0, _TK), :], rhs, preferred_element_type=f32)
        if causal:
            krow = lax.broadcasted_iota(jnp.int32, (_TK, 1), 0)
            s = jnp.where(tok - (c0 + krow) >= 0, s, -jnp.inf)
        off = slope * (tok - c0).astype(f32)
        m_new = jnp.maximum(m, jnp.max(s, axis=0, keepdims=True) - off)
        p = jnp.exp(s - (m_new + off))
        alpha = jnp.exp(m - m_new)
        l = alpha * l + jnp.sum(p, axis=0, keepdims=True)
        acc = alpha * acc + jnp.dot(vst_ref[0, :, pl.ds(c0, _TK)], p.astype(bf16),
                                    preferred_element_type=f32)
        return m_new, l, acc

    jd = t0 // _TK
    init = (jnp.full((1, _ROWS), -jnp.inf, f32), jnp.zeros((1, _ROWS), f32),
            jnp.zeros((HEAD_DIM, _ROWS), f32))
    carry = lax.fori_loop(
        0, jd, lambda j, c: lax.cond(used_ref[j] > 0, lambda cc: tile(j, cc, False), lambda cc: cc, c), init)
    _, l, acc = tile(jd, carry, True)
    o_s = acc / jnp.maximum(l, 1e-30)

    gt = gt_ref[0, 0]
    out = gt[0:1] * o_c + gt[1:2] * o_s + gt[2:3] * o_w
    o_ref[...] = jnp.concatenate([out[:, g * Q_BLOCK:(g + 1) * Q_BLOCK].T for g in range(GROUP)],
                                 axis=1).astype(o_ref.dtype)


def _nsa_prompt(z, g_q, gates, ck, cv, ks, vs, kw, vw):
    t = z.shape[0]
    nqb = t // Q_BLOCK
    nb = ck.shape[0]
    qcol = _OFF["q"][0] // (GROUP * HEAD_DIM)
    gt = gates.reshape(nqb, Q_BLOCK, N_KV, GROUP, 3).transpose(2, 0, 4, 3, 1).reshape(N_KV, nqb, 3, _ROWS)
    ckh = ck.astype(bf16).transpose(1, 0, 2)
    cvt = cv.astype(bf16).transpose(1, 2, 0)
    pos = jnp.arange(t)
    onehot = (pos[:, None] // CMP_BLOCK == jnp.arange(nb)[None, :]).astype(bf16)
    off_hi = (pos % _TK) // 16 * 16
    off_lo = pos % 16
    feats = jnp.stack([off_hi, off_lo] * 3 + [jnp.zeros_like(pos)] * (_POS_FEATS - 6), axis=1).astype(bf16)
    ke = jnp.concatenate([ks.astype(bf16).transpose(1, 0, 2),
                          jnp.broadcast_to(jnp.concatenate([onehot, feats], axis=1)[None],
                                           (N_KV, t, nb + _POS_FEATS))], axis=-1)
    vst = vs.astype(bf16).transpose(1, 2, 0)
    kwh = jnp.pad(kw.astype(bf16).transpose(1, 0, 2), ((0, 0), (WINDOW, 0), (0, 0)))
    vwt = jnp.pad(vw.astype(bf16).transpose(1, 2, 0), ((0, 0), (0, 0), (WINDOW, 0)))
    kd = HEAD_DIM + nb + _POS_FEATS
    return pl.pallas_call(
        _nsa_kernel,
        out_shape=jax.ShapeDtypeStruct((t, NSA_DIM), bf16),
        grid=(N_KV, nqb),
        in_specs=[
            pl.BlockSpec((Q_BLOCK, GROUP * HEAD_DIM), lambda k, i: (i, qcol + k)),
            pl.BlockSpec((1, HEAD_DIM), lambda k, i: (0, 0)),
            pl.BlockSpec((1, 1, 3, _ROWS), lambda k, i: (k, i, 0, 0)),
            pl.BlockSpec((1, nb, HEAD_DIM), lambda k, i: (k, 0, 0)),
            pl.BlockSpec((1, HEAD_DIM, nb), lambda k, i: (k, 0, 0)),
            pl.BlockSpec((1, t, kd), lambda k, i: (k, 0, 0)),
            pl.BlockSpec((1, HEAD_DIM, t), lambda k, i: (k, 0, 0)),
            pl.BlockSpec((1, t + WINDOW, HEAD_DIM), lambda k, i: (k, 0, 0)),
            pl.BlockSpec((1, HEAD_DIM, t + WINDOW), lambda k, i: (k, 0, 0)),
        ],
        out_specs=pl.BlockSpec((Q_BLOCK, GROUP * HEAD_DIM), lambda k, i: (i, k)),
        scratch_shapes=[pltpu.VMEM((WINDOW + Q_BLOCK, _ROWS), f32),
                        pltpu.SMEM((max(nb // (_TK // CMP_BLOCK), 1),), jnp.int32)],
        compiler_params=_cparams(("arbitrary", "arbitrary")),
        name="nsa_prompt",
    )(z, g_q.reshape(1, HEAD_DIM), gt, ckh, cvt, ke, vst, kwh, vwt)


def _pair_sum(x, low):
    s_lo = jnp.sum(jnp.where(low, x, 0.0), axis=-1, keepdims=True)
    s_hi = jnp.sum(jnp.where(low, 0.0, x), axis=-1, keepdims=True)
    return jnp.where(low, s_lo, s_hi)


def _rw_prep_kernel(shift_rows, *refs):
    cur, refs = refs[:6], refs[6:]
    prv, refs = refs[:6], refs[6:]
    if shift_rows:
        st0, refs = refs[:6], refs[6:]
    mus, refs = refs[:6], refs[6:]
    (w0_ref, w2_ref, a0_ref, a2_ref, g2_ref, kk_ref, ka_ref, rk_ref,
     ro_ref, ko_ref, vo_ref, kko_ref, kao_ref, ldo_ref, go_ref, bo_ref) = refs
    first = pl.program_id(0) == 0

    def shifted(n):
        x = cur[n][...]
        if shift_rows:
            prev_row = jnp.where(first, st0[n][pl.ds(7, 1), :], prv[n][pl.ds(7, 1), :])
            rolled = pltpu.roll(x, 1, axis=0)
            rid = lax.broadcasted_iota(jnp.int32, x.shape, 0)
            xp = jnp.where(rid == 0, prev_row, rolled)
        else:
            xp = prv[n][...]
        return x + (xp - x) * mus[n][...]

    r, k, v, gl, wl, al = (shifted(n) for n in range(6))

    y = -(w0_ref[...] + jnp.dot(jnp.tanh(wl).astype(bf16), w2_ref[...], preferred_element_type=f32))
    softplus = jnp.maximum(y, 0.0) + jnp.log1p(jnp.exp(-jnp.abs(y)))
    w_log = -softplus - 0.5
    ld = -jnp.exp(w_log)
    a = jax.nn.sigmoid(a0_ref[...] + jnp.dot(al.astype(bf16), a2_ref[...], preferred_element_type=f32))
    g = jnp.dot(jax.nn.sigmoid(gl).astype(bf16), g2_ref[...], preferred_element_type=f32)
    kk = k * kk_ref[...]
    k2 = k * (1.0 + (a - 1.0) * ka_ref[...])
    rkr = r * k2 * rk_ref[...]
    low = lax.broadcasted_iota(jnp.int32, (1, 2 * RW_N), 1) < RW_N
    for hp in range(RW_HEADS // 2):
        sl = slice(hp * 2 * RW_N, (hp + 1) * 2 * RW_N)
        kkp = kk[:, sl]
        kkp = kkp / jnp.maximum(jnp.sqrt(_pair_sum(kkp * kkp, low)), 1e-12)
        ro_ref[hp] = r[:, sl]
        ko_ref[hp] = k2[:, sl]
        vo_ref[hp] = v[:, sl]
        kko_ref[hp] = kkp
        kao_ref[hp] = kkp * a[:, sl]
        ldo_ref[hp] = ld[:, sl]
        go_ref[hp] = g[:, sl]
        bo_ref[hp] = _pair_sum(rkr[:, sl], low) * v[:, sl]


def _rw_prep(z, prev, mu_p, w0, w2p, a0, a2p, g2, k_k, k_a, r_k, shift_rows):
    m = z.shape[0]
    tm = _row_tile(m, 256)
    names = ("r", "k", "v", "gl", "wl", "al")

    def col_spec(name, rows, imap):
        o, _, pw = _OFF[name]
        return pl.BlockSpec((rows, pw), functools.partial(imap, o // pw))

    cur = [col_spec(n, tm, lambda c, i: (i, c)) for n in names]
    if shift_rows:
        blk8 = tm // 8
        prv = [col_spec(n, 8, lambda c, i: (jnp.maximum(i * blk8 - 1, 0), c)) for n in names]
        prv += [col_spec(n, 8, lambda c, i: (0, c)) for n in names]
        prev_args = [z] * 6 + [prev] * 6
    else:
        prv = [col_spec(n, tm, lambda c, i: (i, c)) for n in names]
        prev_args = [prev] * 6
    mus = [col_spec(n, 1, lambda c, i: (0, c)) for n in names]
    vec = pl.BlockSpec((1, RW_DIM), lambda i: (0, 0))
    out_spec = pl.BlockSpec((RW_HEADS // 2, tm, 2 * RW_N), lambda i: (0, i, 0))
    outs = pl.pallas_call(
        functools.partial(_rw_prep_kernel, shift_rows),
        out_shape=[jax.ShapeDtypeStruct((RW_HEADS // 2, m, 2 * RW_N), f32)] * 8,
        grid=(m // tm,),
        in_specs=cur + prv + mus + [
            vec,
            pl.BlockSpec((128, RW_DIM), lambda i: (0, 0)),
            vec,
            pl.BlockSpec((128, RW_DIM), lambda i: (0, 0)),
            pl.BlockSpec((G_LORA, RW_DIM), lambda i: (0, 0)),
            vec, vec, vec,
        ],
        out_specs=[out_spec] * 8,
        compiler_params=_cparams(("parallel",)),
        name="rw_prep",
    )(*([z] * 6), *prev_args, *([mu_p] * 6),
      w0.reshape(1, RW_DIM), w2p, a0.reshape(1, RW_DIM), a2p, g2,
      k_k.reshape(1, RW_DIM), k_a.reshape(1, RW_DIM), r_k.reshape(1, RW_DIM))
    return outs


_CH = 64
_HEADS_PER_STEP = 16
_NN = (((1,), (0,)), ((), ()))
_NT = (((1,), (1,)), ((), ()))
_TN = (((0,), (0,)), ((), ()))


def _split2(x):
    hi = x.astype(bf16)
    lo = (x - hi.astype(f32)).astype(bf16)
    return hi, lo


def _dot3(a, b, dims=_NN):
    ah, al = _split2(a)
    bh, bl = _split2(b)
    d = functools.partial(lax.dot_general, dimension_numbers=dims, preferred_element_type=f32)
    return d(ah, bh) + (d(ah, bl) + d(al, bh))


def _split3(x):
    x1 = x.astype(bf16)
    r1 = x - x1.astype(f32)
    x2 = r1.astype(bf16)
    return x1, x2, (r1 - x2.astype(f32)).astype(bf16)


def _dot_exact_rhs(a01, b):
    b1, b2, b3 = _split3(b)
    d = functools.partial(jnp.dot, preferred_element_type=f32)
    return d(a01, b1) + (d(a01, b2) + d(a01, b3))


def _dot_exact_lhs_tn(a, b01):
    a1, a2, a3 = _split3(a)
    d = functools.partial(lax.dot_general, dimension_numbers=_TN, preferred_element_type=f32)
    return d(a1, b01) + (d(a2, b01) + d(a3, b01))


def _wkv_chunk_kernel(r_ref, k_ref, v_ref, kk_ref, ka_ref, ld_ref, s0_ref, y_ref, sT_ref, st_ref):
    c = pl.program_id(0)

    @pl.when(c == 0)
    def _():
        st_ref[...] = s0_ref[...]

    ti = lax.broadcasted_iota(jnp.int32, (_CH, _CH), 0)
    si = lax.broadcasted_iota(jnp.int32, (_CH, _CH), 1)
    incl = ti >= si
    strict = ti > si
    l_incl = incl.astype(bf16)
    eye = (ti == si).astype(f32)
    ones = jnp.ones((_CH, RW_N), bf16)

    def each(f, *lists):
        return [f(*a) for a in zip(*lists)]

    def cat0(a, b):
        return jnp.concatenate([a, b], axis=0)

    def cat1(a, b):
        return jnp.concatenate([a, b], axis=1)

    def heads_step(heads):
        def load(ref):
            return [ref[hp, :, sub * RW_N:(sub + 1) * RW_N] for hp, sub in heads]

        r, k, v, kk, ka, ld = (load(ref) for ref in (r_ref, k_ref, v_ref, kk_ref, ka_ref, ld_ref))
        lp = each(lambda x: _dot_exact_rhs(l_incl, x), ld)
        lp_end = each(lambda x: _dot_exact_lhs_tn(x, ones), ld)
        e_neg = each(lambda x: jnp.exp(-x), lp)
        at = each(lambda a, x, y: -a * jnp.exp(x - y), kk, lp, ld)
        bt = each(jnp.multiply, ka, e_neg)
        kt = each(jnp.multiply, k, e_neg)
        rt = each(lambda a, x: a * jnp.exp(x), r, lp)
        e_end = each(lambda x: jnp.exp(x[_CH - 1:_CH, :] - x), lp)
        bh = each(jnp.multiply, ka, e_end)
        kh = each(jnp.multiply, k, e_end)
        sc = each(lambda a, b: _dot3(a, b, _NT), each(cat0, at, rt), each(cat0, bt, kt))
        a_b = each(lambda x: jnp.where(strict, x[:_CH, :_CH], 0.0), sc)
        a_k = each(lambda x: jnp.where(strict, x[:_CH, _CH:], 0.0), sc)
        g_b = each(lambda x: jnp.where(incl, x[_CH:, :_CH], 0.0), sc)
        g_k = each(lambda x: jnp.where(incl, x[_CH:, _CH:], 0.0), sc)
        tm = each(lambda x: eye + x, a_b)
        pw = a_b
        for _ in range(5):
            pw = each(_dot3, pw, pw)
            tm = each(lambda t_, p_: t_ + _dot3(t_, p_), tm, pw)
        akv = each(_dot3, a_k, v)
        tx = each(_dot3, tm, each(cat1, at, akv))
        st = [st_ref[2 * hp + sub] for hp, sub in heads]
        ws = each(_dot3, each(lambda x, y: cat0(x[:, :RW_N], y), tx, rt), st)
        u = each(lambda a, x: a[:_CH] + x[:, RW_N:], ws, tx)
        uv = each(cat0, u, v)
        y = each(lambda a, gb, gk, x: a[_CH:] + _dot3(cat1(gb, gk), x), ws, g_b, g_k, uv)
        st_new = each(lambda e, s_, b_, k_, x: jnp.exp(e) * s_ + _dot3(cat0(b_, k_), x, _TN),
                      lp_end, st, bh, kh, uv)
        for (hp, sub), s_ in zip(heads, st_new):
            st_ref[2 * hp + sub] = s_
        return y

    def pairs(i, carry):
        heads = [(_HEADS_PER_STEP // 2 * i + a, sub) for a in range(_HEADS_PER_STEP // 2) for sub in range(2)]
        y = heads_step(heads)
        for a in range(_HEADS_PER_STEP // 2):
            y_ref[_HEADS_PER_STEP // 2 * i + a] = cat1(y[2 * a], y[2 * a + 1])
        return carry

    lax.fori_loop(0, RW_HEADS // _HEADS_PER_STEP, pairs, 0)

    @pl.when(c == pl.num_programs(0) - 1)
    def _():
        sT_ref[...] = st_ref[...]


def _wkv_chunks(r, k, v, kk, ka, ld, s0t):
    t = r.shape[1]
    spec = pl.BlockSpec((RW_HEADS // 2, _CH, 2 * RW_N), lambda c: (0, c, 0))
    sspec = pl.BlockSpec((RW_HEADS, RW_N, RW_N), lambda c: (0, 0, 0))
    return pl.pallas_call(
        _wkv_chunk_kernel,
        out_shape=[jax.ShapeDtypeStruct((RW_HEADS // 2, t, 2 * RW_N), f32),
                   jax.ShapeDtypeStruct((RW_HEADS, RW_N, RW_N), f32)],
        grid=(t // _CH,),
        in_specs=[spec] * 6 + [sspec],
        out_specs=[spec, sspec],
        scratch_shapes=[pltpu.VMEM((RW_HEADS, RW_N, RW_N), f32)],
        compiler_params=_cparams(("arbitrary",)),
        name="wkv_chunks",
    )(r, k, v, kk, ka, ld, s0t)


def _rw_post_kernel(y_ref, b_ref, g_ref, lw_ref, lb_ref, o_ref):
    low = lax.broadcasted_iota(jnp.int32, (1, 2 * RW_N), 1) < RW_N
    for hp in range(RW_HEADS // 2):
        y = y_ref[hp]
        mu = _pair_sum(y, low) / RW_N
        var = _pair_sum(jnp.square(y - mu), low) / RW_N
        yn = (y - mu) * lax.rsqrt(var + GN_EPS) * lw_ref[hp] + lb_ref[hp]
        o_ref[:, hp * 2 * RW_N:(hp + 1) * 2 * RW_N] = ((yn + b_ref[hp]) * g_ref[hp]).astype(o_ref.dtype)


def _rw_post(y, bonus, g, ln_w, ln_b):
    m = y.shape[1]
    tm = _row_tile(m, 512)
    spec = pl.BlockSpec((RW_HEADS // 2, tm, 2 * RW_N), lambda i: (0, i, 0))
    pspec = pl.BlockSpec((RW_HEADS // 2, 1, 2 * RW_N), lambda i: (0, 0, 0))
    return pl.pallas_call(
        _rw_post_kernel,
        out_shape=jax.ShapeDtypeStruct((m, RW_DIM), bf16),
        grid=(m // tm,),
        in_specs=[spec, spec, spec, pspec, pspec],
        out_specs=pl.BlockSpec((tm, RW_DIM), lambda i: (i, 0)),
        compiler_params=_cparams(("parallel",)),
        name="rw_post",
    )(y, bonus, g, ln_w.reshape(RW_HEADS // 2, 1, 2 * RW_N), ln_b.reshape(RW_HEADS // 2, 1, 2 * RW_N))


def _lora_pad(w):
    return jnp.pad(w, ((0, 128 - w.shape[0]), (0, 0))).astype(bf16)


def _rwkv_prompt(z, shift0_p, s0t, p):
    r, k, v, kk, ka, ld, g, bonus = _rw_prep(
        z, shift0_p, p["mu_p"], p["rw_w0"], p["w2p"], p["rw_a0"], p["a2p"], p["g2"],
        p["rw_k_k"], p["rw_k_a"], p["rw_r_k"], True)
    y, st = _wkv_chunks(r, k, v, kk, ka, ld, s0t)
    return _rw_post(y, bonus, g, p["rw_ln_w"], p["rw_ln_b"]), st


def _wkv_step_kernel(s_ref, r_ref, k_ref, vc_ref, kk_ref, ka_ref, ld_ref, y_ref, so_ref):
    s = s_ref[0]
    kk = kk_ref[0][:, None, :]
    sa = jnp.sum(s * (-kk), axis=-1, keepdims=True)
    s = (s * jnp.exp(ld_ref[0])[:, None, :] + sa * ka_ref[0][:, None, :]
         + vc_ref[0] * k_ref[0][:, None, :])
    so_ref[0] = s
    y_ref[0] = jnp.sum(s * r_ref[0][:, None, :], axis=-1, keepdims=True)


def _wkv_step(s0, r, k, v, kk, ka, ld):
    b = s0.shape[0]
    sspec = pl.BlockSpec((1, RW_HEADS, RW_N, RW_N), lambda i: (i, 0, 0, 0))
    vspec = pl.BlockSpec((1, RW_HEADS, RW_N), lambda i: (i, 0, 0))
    cspec = pl.BlockSpec((1, RW_HEADS, RW_N, 1), lambda i: (i, 0, 0, 0))
    y, s1 = pl.pallas_call(
        _wkv_step_kernel,
        out_shape=[jax.ShapeDtypeStruct((b, RW_HEADS, RW_N, 1), f32),
                   jax.ShapeDtypeStruct(s0.shape, f32)],
        grid=(b,),
        in_specs=[sspec, vspec, vspec, cspec, vspec, vspec, vspec],
        out_specs=[cspec, sspec],
        compiler_params=_cparams(("parallel",)),
        name="wkv_step",
    )(s0, r, k, v[..., None], kk, ka, ld)
    return y[..., 0], s1


_GP = 8


def _compress_pages_kernel(npg, pt_ref, pool_ref, pe_ref, w_ref, o_ref, buf0, buf1, sem):
    g = pl.program_id(0)
    n_steps = pl.num_programs(0)

    def page_copies(seq, buf, slot):
        out = []
        for pg in range(npg):
            page_id = pt_ref[seq, pg]
            for kv in range(N_KV):
                out.append(pltpu.make_async_copy(pool_ref.at[page_id, kv], buf.at[:, pg * N_KV + kv, :],
                                                 sem.at[slot]))
        return out

    def start(seq, buf, slot):
        for cp in page_copies(seq, buf, slot):
            cp.start()

    def wait(seq, buf, slot):
        for cp in page_copies(seq, buf, slot):
            cp.wait()

    def compress(buf, out_slot):
        acc = jnp.zeros((npg * N_KV, 2 * CMP_BLOCK), f32)
        for d2 in range(HEAD_DIM // 2):
            x = jnp.concatenate([buf[2 * d2 + e] + pe_ref[pl.ds(2 * d2 + e, 1), :] for e in range(2)], axis=1)
            acc = acc + jnp.dot(x.astype(bf16), w_ref[d2], preferred_element_type=f32)
        o_ref[out_slot] = acc

    @pl.when(g == 0)
    def _():
        start(0, buf0, 0)

    start(2 * g + 1, buf1, 1)
    wait(2 * g, buf0, 0)
    compress(buf0, 0)

    @pl.when(g + 1 < n_steps)
    def _():
        start(2 * g + 2, buf0, 0)

    wait(2 * g + 1, buf1, 1)
    compress(buf1, 1)


def _compress_pages(pool_t, table, pe, w):
    b, npg = table.shape
    page = pool_t.shape[-1]
    assert b % 2 == 0 and page == 2 * CMP_BLOCK
    pe_t = jnp.tile(pe.T, (1, 2))
    wd = _blockdiag2(w.transpose(1, 0, 2)).reshape(HEAD_DIM // 2, 2 * page, page).astype(bf16)
    out = pl.pallas_call(
        functools.partial(_compress_pages_kernel, npg),
        out_shape=jax.ShapeDtypeStruct((b, npg * N_KV, page), f32),
        grid_spec=pltpu.PrefetchScalarGridSpec(
            num_scalar_prefetch=1,
            grid=(b // 2,),
            in_specs=[
                pl.BlockSpec(memory_space=pl.ANY),
                pl.BlockSpec((HEAD_DIM, page), lambda i, pt: (0, 0)),
                pl.BlockSpec((HEAD_DIM // 2, 2 * page, page), lambda i, pt: (0, 0, 0)),
            ],
            out_specs=pl.BlockSpec((2, npg * N_KV, page), lambda i, pt: (i, 0, 0)),
            scratch_shapes=[pltpu.VMEM((HEAD_DIM, npg * N_KV, page), f32),
                            pltpu.VMEM((HEAD_DIM, npg * N_KV, page), f32),
                            pltpu.SemaphoreType.DMA((2,))]),
        compiler_params=_cparams(("arbitrary",)),
        name="compress_pages",
    )(table, pool_t, pe_t, wd)
    out = out.reshape(b, npg, N_KV, 2, HEAD_DIM).transpose(0, 1, 3, 2, 4)
    return out.reshape(b, 2 * npg, N_KV, HEAD_DIM)


def _dec_cmp_kernel(n_pick, t_pos, q_ref, ckt_ref, cv_ref, xk_ref, xv_ref, wk_ref, wv_ref, gkc_ref,
                    oc_ref, idx_ref):
    sb = q_ref.shape[0]
    nb = ckt_ref.shape[-1]
    kd = xk_ref.shape[-1]
    blk = lax.broadcasted_iota(jnp.int32, (1, nb), 1)
    blk_mid = (blk * CMP_BLOCK).astype(f32) + (CMP_BLOCK - 1) / 2
    ck_new = _rms(jnp.dot(xk_ref[...].reshape(sb * _GP, kd).astype(bf16), wk_ref[...],
                          preferred_element_type=f32), gkc_ref[...])
    cv_new = jnp.dot(xv_ref[...].reshape(sb * _GP, kd).astype(bf16), wv_ref[...],
                     preferred_element_type=f32)
    new_mid = float(nb * CMP_BLOCK) + (CMP_BLOCK - 1) / 2
    new_ok = nb * CMP_BLOCK + (CMP_BLOCK - 1) <= t_pos
    lane = lax.broadcasted_iota(jnp.int32, (1, 128), 1)
    imps = []
    for r in range(sb):
        for kv in range(N_KV):
            row = r * _GP + kv
            q = q_ref[r, kv]
            g1 = lax.broadcasted_iota(jnp.int32, (_GP, 1), 0) + (kv * GROUP + 1)
            slope = jnp.exp2(-0.5 * g1.astype(f32))
            s = jnp.dot(q.astype(bf16), ckt_ref[r, kv], preferred_element_type=f32)
            s = s - slope * (float(t_pos) - blk_mid)
            s = jnp.where(blk * CMP_BLOCK + (CMP_BLOCK - 1) <= t_pos, s, -jnp.inf)
            qn = q.astype(bf16).astype(f32)
            s_new = jnp.sum(qn * ck_new[row:row + 1].astype(bf16).astype(f32), axis=-1, keepdims=True)
            s_new = s_new - slope * (float(t_pos) - new_mid)
            s_new = jnp.where(new_ok, s_new, -jnp.inf)
            m = jnp.maximum(jnp.max(s, axis=-1, keepdims=True), s_new)
            m = jnp.where(jnp.isfinite(m), m, 0.0)
            p = jnp.exp(s - m)
            p_new = jnp.exp(s_new - m)
            den = jnp.maximum(jnp.sum(p, axis=-1, keepdims=True) + p_new, 1e-30)
            p = p / den
            p_new = p_new / den
            oc = jnp.dot(p.astype(bf16), cv_ref[r, kv], preferred_element_type=f32)
            oc_ref[r, kv] = oc + p_new * cv_new[row:row + 1]
            imps.append(jnp.sum(p[:GROUP], axis=0, keepdims=True))
    imp = jnp.concatenate(imps, axis=0)
    cur = t_pos // CMP_BLOCK
    forced = (blk == 0) | (blk == cur) | (blk == cur - 1)
    v = jnp.where(forced, FORCED_SCORE, jnp.where(blk * CMP_BLOCK <= t_pos, imp, -1.0))
    out = jnp.zeros((sb * N_KV, 128), jnp.int32)
    for it in range(n_pick):
        mx = jnp.max(v, axis=-1, keepdims=True)
        idx = jnp.min(jnp.where(v == mx, blk, nb), axis=-1, keepdims=True)
        out = jnp.where(lane == it, idx, out)
        v = jnp.where(blk == idx, -jnp.inf, v)
    idx_ref[...] = out.reshape(sb, N_KV, 128)


def _dec_cmp(q, ck, cv, xk, xv, wk, wv, g_kc, t_pos, n_pick):
    b, nb = ck.shape[:2]
    ckt = ck.astype(bf16).transpose(0, 2, 3, 1)
    cvh = cv.astype(bf16).transpose(0, 2, 1, 3)
    kd = CMP_BLOCK * HEAD_DIM
    sb = 8 if b % 8 == 0 else b
    return pl.pallas_call(
        functools.partial(_dec_cmp_kernel, n_pick, t_pos),
        out_shape=[jax.ShapeDtypeStruct((b, N_KV, _GP, HEAD_DIM), f32),
                   jax.ShapeDtypeStruct((b, N_KV, 128), jnp.int32)],
        grid=(b // sb,),
        in_specs=[
            pl.BlockSpec((sb, N_KV, _GP, HEAD_DIM), lambda i: (i, 0, 0, 0)),
            pl.BlockSpec((sb, N_KV, HEAD_DIM, nb), lambda i: (i, 0, 0, 0)),
            pl.BlockSpec((sb, N_KV, nb, HEAD_DIM), lambda i: (i, 0, 0, 0)),
            pl.BlockSpec((sb, _GP, kd), lambda i: (i, 0, 0)),
            pl.BlockSpec((sb, _GP, kd), lambda i: (i, 0, 0)),
            pl.BlockSpec((kd, HEAD_DIM), lambda i: (0, 0)),
            pl.BlockSpec((kd, HEAD_DIM), lambda i: (0, 0)),
            pl.BlockSpec((1, HEAD_DIM), lambda i: (0, 0)),
        ],
        out_specs=[pl.BlockSpec((sb, N_KV, _GP, HEAD_DIM), lambda i: (i, 0, 0, 0)),
                   pl.BlockSpec((sb, N_KV, 128), lambda i: (i, 0, 0))],
        compiler_params=_cparams(("parallel",)),
        name="dec_cmp",
    )(q, ckt, cvh, xk, xv, wk, wv, g_kc.reshape(1, HEAD_DIM))


def _dec_sel_kernel(t_pos, n_pick, pt_ref, idx_ref, q_ref, gt_ref, oc_ref, ksn_ref, vsn_ref, kwn_ref,
                    vwn_ref, wk_ref, wv_ref, pk_ref, pv_ref, o_ref, kb0, vb0, kb1, vb1, sem):
    g = pl.program_id(0)
    n_steps = pl.num_programs(0)
    n_buf = wk_ref.shape[-1]
    page = 2 * CMP_BLOCK

    def copies(seq, kb, vb, slot):
        out = []
        for kv in range(N_KV):
            for s in range(n_pick):
                pg = pt_ref[seq, idx_ref[seq, kv * 128 + s] // 2]
                dst = pl.ds(s * page, page)
                out.append(pltpu.make_async_copy(pk_ref.at[pg, kv], kb.at[kv, :, dst], sem.at[slot]))
                out.append(pltpu.make_async_copy(pv_ref.at[pg, kv], vb.at[kv, :, dst], sem.at[slot]))
        return out

    def start(seq, kb, vb, slot):
        for cp in copies(seq, kb, vb, slot):
            cp.start()

    def wait(seq, kb, vb, slot):
        for cp in copies(seq, kb, vb, slot):
            cp.wait()

    def slopes(kv):
        g1 = lax.broadcasted_iota(jnp.int32, (_GP, 1), 0) + (kv * GROUP + 1)
        return jnp.exp2(-0.5 * g1.astype(f32))

    lane = lax.broadcasted_iota(jnp.int32, (1, page), 1)
    c = lax.broadcasted_iota(jnp.int32, (1, n_buf), 1)
    kpos = t_pos - n_buf + c
    distw = t_pos - kpos
    okw = (kpos >= 0) & (distw >= 0) & (distw < WINDOW)

    def attend(seq, r, kb, vb):
        for kv in range(N_KV):
            q = q_ref[r, kv].astype(bf16)
            qf = q.astype(f32)
            slope = slopes(kv)
            dist, ok = [], []
            for s in range(n_pick):
                blk = idx_ref[seq, kv * 128 + s]
                d = t_pos - ((blk // 2) * page + lane)
                dist.append(d)
                ok.append((lane // CMP_BLOCK == blk % 2) & (d >= 0))
            dist = jnp.concatenate(dist, axis=1)
            ok = jnp.concatenate(ok, axis=1)
            s_sel = jnp.dot(q, kb[kv].astype(bf16), preferred_element_type=f32)
            s_sel = jnp.where(ok, s_sel - slope * dist.astype(f32), -jnp.inf)
            s_new = jnp.sum(qf * ksn_ref[r, kv:kv + 1].astype(bf16).astype(f32), axis=-1, keepdims=True)
            m = jnp.maximum(jnp.max(s_sel, axis=-1, keepdims=True), s_new)
            p = jnp.exp(s_sel - m)
            p_new = jnp.exp(s_new - m)
            den = jnp.maximum(jnp.sum(p, axis=-1, keepdims=True) + p_new, 1e-30)
            o_s = lax.dot_general((p / den).astype(bf16), vb[kv].astype(bf16), _NT, preferred_element_type=f32)
            o_s = o_s + (p_new / den).astype(bf16).astype(f32) * vsn_ref[r, kv:kv + 1].astype(bf16).astype(f32)
            sw = jnp.dot(q, wk_ref[r, kv].astype(bf16), preferred_element_type=f32)
            sw = jnp.where(okw, sw - slope * distw.astype(f32), -jnp.inf)
            sw_new = jnp.sum(qf * kwn_ref[r, kv:kv + 1].astype(bf16).astype(f32), axis=-1, keepdims=True)
            mw = jnp.maximum(jnp.max(sw, axis=-1, keepdims=True), sw_new)
            pw = jnp.exp(sw - mw)
            pw_new = jnp.exp(sw_new - mw)
            denw = jnp.maximum(jnp.sum(pw, axis=-1, keepdims=True) + pw_new, 1e-30)
            o_w = lax.dot_general((pw / denw).astype(bf16), wv_ref[r, kv].astype(bf16), _NT,
                                  preferred_element_type=f32)
            o_w = o_w + (pw_new / denw).astype(bf16).astype(f32) * vwn_ref[r, kv:kv + 1].astype(bf16).astype(f32)
            gt = gt_ref[r, kv]
            o_ref[r, kv] = gt[:, 0:1] * oc_ref[r, kv] + gt[:, 1:2] * o_s + gt[:, 2:3] * o_w

    @pl.when(g == 0)
    def _():
        start(0, kb0, vb0, 0)

    start(2 * g + 1, kb1, vb1, 1)
    wait(2 * g, kb0, vb0, 0)
    attend(2 * g, 0, kb0, vb0)

    @pl.when(g + 1 < n_steps)
    def _():
        start(2 * g + 2, kb0, vb0, 0)

    wait(2 * g + 1, kb1, vb1, 1)
    attend(2 * g + 1, 1, kb1, vb1)


def _dec_sel(q, gates, o_c, idx, table, pool_k, pool_v, ks_new, vs_new, kw_new, vw_new, win_k, win_v,
             t_pos, n_pick):
    b = q.shape[0]
    assert b % 2 == 0
    n_buf = win_k.shape[-1]
    page = pool_k.shape[-1]
    hspec = pl.BlockSpec((2, N_KV, _GP, HEAD_DIM), lambda i, pt, ix: (i, 0, 0, 0))
    nspec = pl.BlockSpec((2, N_KV, HEAD_DIM), lambda i, pt, ix: (i, 0, 0))
    wspec = pl.BlockSpec((2, N_KV, HEAD_DIM, n_buf), lambda i, pt, ix: (i, 0, 0, 0))
    anyspec = pl.BlockSpec(memory_space=pl.ANY)
    gbuf = pltpu.VMEM((N_KV, HEAD_DIM, n_pick * page), f32)
    return pl.pallas_call(
        functools.partial(_dec_sel_kernel, t_pos, n_pick),
        out_shape=jax.ShapeDtypeStruct((b, N_KV, _GP, HEAD_DIM), f32),
        grid_spec=pltpu.PrefetchScalarGridSpec(
            num_scalar_prefetch=2,
            grid=(b // 2,),
            in_specs=[hspec, pl.BlockSpec((2, N_KV, _GP, 3), lambda i, pt, ix: (i, 0, 0, 0)), hspec,
                      nspec, nspec, nspec, nspec, wspec, wspec, anyspec, anyspec],
            out_specs=hspec,
            scratch_shapes=[gbuf, gbuf, gbuf, gbuf, pltpu.SemaphoreType.DMA((2,))]),
        compiler_params=_cparams(("arbitrary",)),
        name="dec_sel",
    )(table, idx, q, gates, o_c, ks_new, vs_new, kw_new, vw_new, win_k, win_v, pool_k, pool_v)


def _head_rms(x, g):
    ms = jnp.mean(x * x, axis=-1, keepdims=True)
    return x * lax.rsqrt(ms + NORM_EPS) * g


def _prepare(p):
    q = dict(p)
    for n in ("ffn1_gate", "ffn1_up", "ffn1_down", "ffn2_gate", "ffn2_up", "ffn2_down", "w_pa", "w_pb", "w_out"):
        q[n] = p[n].astype(bf16)
    q["w_in_p"] = _pad_cols(p["w_in"]).astype(bf16)
    q["mu_p"] = _pad_rw_cols(p["rw_mu"][None])
    q["w2p"] = _lora_pad(p["rw_w2"])
    q["a2p"] = _lora_pad(p["rw_a2"])
    q["g2"] = p["rw_g2"].astype(bf16)
    return q


def _nsa_proj(z, p):
    m = z.shape[0]
    kvs = (m, N_KV, HEAD_DIM)
    q = _head_rms(_seg(z, "q").reshape(m, N_HEADS, HEAD_DIM), p["g_q"])
    kc = _seg(z, "kc").reshape(kvs)
    vc = _seg(z, "vc").reshape(kvs)
    ks = _head_rms(_seg(z, "ks").reshape(kvs), p["g_ks"])
    vs = _seg(z, "vs").reshape(kvs)
    kw = _head_rms(_seg(z, "kw").reshape(kvs), p["g_kw"])
    vw = _seg(z, "vw").reshape(kvs)
    gates = jax.nn.sigmoid(_seg(z, "ga").reshape(m, N_HEADS, 3))
    return q, gates, kc, vc, ks, vs, kw, vw


def _layer_prompt(x, p):
    t = x.shape[0]
    x = _ffn(x, p["n_ffn1"], p["ffn1_gate"], p["ffn1_up"], p["ffn1_down"])
    z = _inproj(x, p["n_mix"], p["w_in_p"])
    q, gates, kc, vc, ks, vs, kw, vw = _nsa_proj(z, p)
    ck = _head_rms(_compress(kc.reshape(1, t, KV_DIM), p["pe_cmp_k"], p["w_cmp_k"])[0]
                   .reshape(-1, N_KV, HEAD_DIM), p["g_kc"])
    cv = _compress(vc.reshape(1, t, KV_DIM), p["pe_cmp_v"], p["w_cmp_v"])[0].reshape(-1, N_KV, HEAD_DIM)
    o_a = _nsa_prompt(z, p["g_q"], gates, ck, cv, ks, vs, kw, vw)
    shift0 = jnp.zeros((8, _Z_COLS), f32)
    s0t = jnp.zeros((RW_HEADS, RW_N, RW_N), f32)
    o_b, st = _rwkv_prompt(z, shift0, s0t, p)
    x = _merge(o_a, o_b, z, x, p["w_pa"], p["w_pb"], p["w_out"])
    x = _ffn(x, p["n_ffn2"], p["ffn2_gate"], p["ffn2_up"], p["ffn2_down"])
    n_buf = min(WINDOW, t)
    states = (kc, vc, ks, vs, kw[-n_buf:], vw[-n_buf:], _rw_cols(z[-1:]), st.transpose(0, 2, 1)[None])
    return x, states


def _layer_sample(x, p, past):
    b = x.shape[0]
    table = past["page_table"]
    page = past["cmp_k"].shape[1]
    t_pos = table.shape[1] * page
    x = _ffn(x, p["n_ffn1"], p["ffn1_gate"], p["ffn1_up"], p["ffn1_down"])
    z = _inproj(x, p["n_mix"], p["w_in_p"])
    q, gates, kc, vc, ks, vs, kw, vw = _nsa_proj(z, p)

    fm = lambda a: a.transpose(0, 2, 3, 1)

    nbp = t_pos // CMP_BLOCK
    ck = _head_rms(_compress_pages(fm(past["cmp_k"]), table, p["pe_cmp_k"], p["w_cmp_k"]), p["g_kc"])
    cv = _compress_pages(fm(past["cmp_v"]), table, p["pe_cmp_v"], p["w_cmp_v"])

    def new_block_rows(k_new, pe):
        first = k_new + pe[0]
        rest = jnp.broadcast_to(pe[1:].reshape(1, 1, -1), (b, N_KV, (CMP_BLOCK - 1) * HEAD_DIM))
        rows = jnp.concatenate([first, rest], axis=-1)
        return jnp.pad(rows, ((0, 0), (0, _GP - N_KV), (0, 0)))

    n_sel = min(N_SEL, nbp + 1)
    n_pick = n_sel - 1
    scale = HEAD_DIM ** -0.5
    pad_g = lambda a: jnp.pad(a, ((0, 0), (0, 0), (0, _GP - GROUP), (0, 0)))
    qh = pad_g((q * scale).reshape(b, N_KV, GROUP, HEAD_DIM))
    gth = pad_g(gates.reshape(b, N_KV, GROUP, 3))
    kd = CMP_BLOCK * HEAD_DIM
    o_c, idx = _dec_cmp(qh, ck, cv, new_block_rows(kc, p["pe_cmp_k"]), new_block_rows(vc, p["pe_cmp_v"]),
                        p["w_cmp_k"].reshape(kd, HEAD_DIM).astype(bf16),
                        p["w_cmp_v"].reshape(kd, HEAD_DIM).astype(bf16), p["g_kc"], t_pos, n_pick)
    o_a = _dec_sel(qh, gth, o_c, idx.reshape(b, N_KV * 128), table, fm(past["slc_k"]), fm(past["slc_v"]),
                   ks, vs, kw, vw, fm(past["win_k"]), fm(past["win_v"]), t_pos, n_pick)
    o_a = o_a[:, :, :GROUP].reshape(b, NSA_DIM).astype(bf16)

    prev = _pad_rw_cols(past["shift"])
    r, k, v, kk, ka, ld, g, bonus = _rw_prep(
        z, prev, p["mu_p"], p["rw_w0"], p["w2p"], p["rw_a0"], p["a2p"], p["g2"],
        p["rw_k_k"], p["rw_k_a"], p["rw_r_k"], False)
    tb = lambda a: a.transpose(1, 0, 2).reshape(b, RW_HEADS, RW_N)
    y, wkv = _wkv_step(past["wkv"], tb(r), tb(k), tb(v), tb(kk), tb(ka), tb(ld))
    y = y.reshape(b, RW_HEADS // 2, 2 * RW_N).transpose(1, 0, 2)
    o_b = _rw_post(y, bonus, g, p["rw_ln_w"], p["rw_ln_b"])

    x = _merge(o_a, o_b, z, x, p["w_pa"], p["w_pb"], p["w_out"])
    x = _ffn(x, p["n_ffn2"], p["ffn2_gate"], p["ffn2_up"], p["ffn2_down"])
    kvs = lambda a: a.reshape(b, 1, N_KV, HEAD_DIM)
    win_k = jnp.concatenate([past["win_k"][:, 1:], kvs(kw)], axis=1)
    win_v = jnp.concatenate([past["win_v"][:, 1:], kvs(vw)], axis=1)
    states = (kvs(kc), kvs(vc), kvs(ks), kvs(vs), win_k, win_v, _rw_cols(z), wkv)
    return x, states


def kernel(x_prompt, x_sample, cache_cmp_k, cache_cmp_v, cache_slc_k, cache_slc_v, cache_win_k, cache_win_v,
           state_shift, state_wkv, page_table,
           n_ffn1, ffn1_gate, ffn1_up, ffn1_down, n_mix, w_in, g_q, g_kc, g_ks, g_kw,
           w_cmp_k, pe_cmp_k, w_cmp_v, pe_cmp_v,
           rw_mu, rw_w0, rw_w2, rw_a0, rw_a2, rw_g2, rw_k_k, rw_k_a, rw_r_k, rw_ln_w, rw_ln_b,
           w_pa, w_pb, w_out, n_ffn2, ffn2_gate, ffn2_up, ffn2_down):
    assert x_prompt.shape[0] == 1 and x_sample.shape[1] == 1 and n_ffn1.shape[0] == 1
    l = 0
    p = _prepare(dict(
        n_ffn1=n_ffn1[l], ffn1_gate=ffn1_gate[l], ffn1_up=ffn1_up[l], ffn1_down=ffn1_down[l],
        n_mix=n_mix[l], w_in=w_in[l], g_q=g_q[l], g_kc=g_kc[l], g_ks=g_ks[l], g_kw=g_kw[l],
        w_cmp_k=w_cmp_k[l], pe_cmp_k=pe_cmp_k[l], w_cmp_v=w_cmp_v[l], pe_cmp_v=pe_cmp_v[l],
        rw_mu=rw_mu[l], rw_w0=rw_w0[l], rw_w2=rw_w2[l], rw_a0=rw_a0[l], rw_a2=rw_a2[l], rw_g2=rw_g2[l],
        rw_k_k=rw_k_k[l], rw_k_a=rw_k_a[l], rw_r_k=rw_r_k[l], rw_ln_w=rw_ln_w[l], rw_ln_b=rw_ln_b[l],
        w_pa=w_pa[l], w_pb=w_pb[l], w_out=w_out[l],
        n_ffn2=n_ffn2[l], ffn2_gate=ffn2_gate[l], ffn2_up=ffn2_up[l], ffn2_down=ffn2_down[l]))
    t = x_prompt.shape[1]
    y_p, sp = _layer_prompt(x_prompt[0], p)
    past = dict(page_table=page_table, cmp_k=cache_cmp_k[l], cmp_v=cache_cmp_v[l], slc_k=cache_slc_k[l],
                slc_v=cache_slc_v[l], win_k=cache_win_k[l], win_v=cache_win_v[l], shift=state_shift[l],
                wkv=state_wkv[l])
    y_s, ss = _layer_sample(x_sample[:, 0], p, past)
    kvp = lambda a: a.reshape(1, 1, -1, N_KV, HEAD_DIM)
    outs_p = (kvp(sp[0]), kvp(sp[1]), kvp(sp[2]), kvp(sp[3]), kvp(sp[4]), kvp(sp[5]), sp[6][None], sp[7][None])
    outs_s = tuple(a[None] for a in ss)
    return (y_p.reshape(1, t, D_MODEL), y_s[:, None, :]) + outs_p + outs_s
```

```python
import functools

import jax
import jax.numpy as jnp
from jax import lax
from jax.experimental import pallas as pl
from jax.experimental.pallas import tpu as pltpu

f32 = jnp.float32
bf16 = jnp.bfloat16

D_MODEL = 2048
N_HEADS = 16
N_KV = 4
GROUP = 4
HEAD_DIM = 64
NSA_DIM = N_HEADS * HEAD_DIM
KV_DIM = N_KV * HEAD_DIM
CMP_BLOCK = 64
N_SEL = 16
WINDOW = 512
Q_BLOCK = 128
FORCED_SCORE = 1e3
RW_HEADS = 16
RW_N = 64
RW_DIM = RW_HEADS * RW_N
W_LORA = 96
A_LORA = 96
G_LORA = 256
D_FF = 5632
NORM_EPS = 1e-6
GN_EPS = 64e-5
NEG_BIG = -1e30

VMEM_LIMIT = 56 * 1024 * 1024

_SEGS = (
    ("r", 2608, 1024, 1024),
    ("k", 3728, 1024, 1024),
    ("v", 4752, 1024, 1024),
    ("q", 0, 1024, 1024),
    ("g_a", 6128, 2048, 2048),
    ("g_b", 8176, 2048, 2048),
    ("kc", 1024, 256, 256),
    ("vc", 1280, 256, 256),
    ("ks", 1536, 256, 256),
    ("vs", 1792, 256, 256),
    ("kw", 2048, 256, 256),
    ("vw", 2304, 256, 256),
    ("gl", 5872, 256, 256),
    ("ga", 2560, 48, 128),
    ("wl", 3632, 96, 128),
    ("al", 5776, 96, 128),
)
_Z_COLS = 10752


def _seg_offsets():
    offs, o = {}, 0
    for name, _, w, pw in _SEGS:
        assert o % pw == 0
        offs[name] = (o, w, pw)
        o += pw
    return offs, o


_OFF, _USED = _seg_offsets()


_IN_COLS = 10224
_RW_START = 2608
_RW_COLS = 3520


def _pad_cols(x):
    parts = []
    for _, s, w, pw in _SEGS:
        seg = x[..., s:s + w]
        if pw != w:
            seg = jnp.pad(seg, [(0, 0)] * (x.ndim - 1) + [(0, pw - w)])
        parts.append(seg)
    parts.append(jnp.zeros(x.shape[:-1] + (_Z_COLS - _USED,), x.dtype))
    return jnp.concatenate(parts, axis=-1)


def _pad_rw_cols(x):
    pad = [(0, 0)] * (x.ndim - 1) + [(_RW_START, _IN_COLS - _RW_START - _RW_COLS)]
    return _pad_cols(jnp.pad(x, pad))


def _seg(z, name):
    o, w, _ = _OFF[name]
    return z[..., o:o + w]


def _rw_cols(z):
    return jnp.concatenate([_seg(z, n) for n in ("r", "wl", "k", "v", "al", "gl")], axis=-1)


def _cparams(sem, vmem=VMEM_LIMIT):
    return pltpu.CompilerParams(dimension_semantics=sem, vmem_limit_bytes=vmem)


def _row_tile(m, pref):
    return pref if m % pref == 0 else m


def _rms(x, g):
    ms = jnp.mean(x * x, axis=-1, keepdims=True)
    return x * lax.rsqrt(ms + NORM_EPS) * g


def _ffn_kernel(x_ref, g_ref, wg_ref, wu_ref, wd_ref, o_ref, h_ref, acc_ref):
    j = pl.program_id(1)

    @pl.when(j == 0)
    def _():
        h_ref[...] = _rms(x_ref[...], g_ref[...]).astype(bf16)
        acc_ref[...] = jnp.zeros_like(acc_ref)

    h = h_ref[...]
    g = jnp.dot(h, wg_ref[...], preferred_element_type=f32)
    u = jnp.dot(h, wu_ref[...], preferred_element_type=f32)
    a = (g * jax.nn.sigmoid(g) * u).astype(bf16)
    acc_ref[...] += jnp.dot(a, wd_ref[...], preferred_element_type=f32)

    @pl.when(j == pl.num_programs(1) - 1)
    def _():
        o_ref[...] = x_ref[...] + 0.5 * acc_ref[...]


def _ffn(x, gain, wg, wu, wd):
    m, d = x.shape
    ff = wg.shape[1]
    bm = _row_tile(m, 512)
    bf = 512
    return pl.pallas_call(
        _ffn_kernel,
        out_shape=jax.ShapeDtypeStruct((m, d), f32),
        grid=(m // bm, ff // bf),
        in_specs=[
            pl.BlockSpec((bm, d), lambda i, j: (i, 0)),
            pl.BlockSpec((1, d), lambda i, j: (0, 0)),
            pl.BlockSpec((d, bf), lambda i, j: (0, j)),
            pl.BlockSpec((d, bf), lambda i, j: (0, j)),
            pl.BlockSpec((bf, d), lambda i, j: (j, 0)),
        ],
        out_specs=pl.BlockSpec((bm, d), lambda i, j: (i, 0)),
        scratch_shapes=[pltpu.VMEM((bm, d), bf16), pltpu.VMEM((bm, d), f32)],
        compiler_params=_cparams(("parallel", "arbitrary")),
        name="ffn",
    )(x, gain.reshape(1, d), wg, wu, wd)


def _inproj_kernel(x_ref, g_ref, w_ref, o_ref, h_ref):
    @pl.when(pl.program_id(1) == 0)
    def _():
        h_ref[...] = _rms(x_ref[...], g_ref[...]).astype(bf16)

    o_ref[...] = jnp.dot(h_ref[...], w_ref[...], preferred_element_type=f32)


def _inproj(x, gain, w):
    m, d = x.shape
    n = w.shape[1]
    bm = _row_tile(m, 1024)
    bn = 1536
    assert n % bn == 0
    return pl.pallas_call(
        _inproj_kernel,
        out_shape=jax.ShapeDtypeStruct((m, n), f32),
        grid=(m // bm, n // bn),
        in_specs=[
            pl.BlockSpec((bm, d), lambda i, j: (i, 0)),
            pl.BlockSpec((1, d), lambda i, j: (0, 0)),
            pl.BlockSpec((d, bn), lambda i, j: (0, j)),
        ],
        out_specs=pl.BlockSpec((bm, bn), lambda i, j: (i, j)),
        scratch_shapes=[pltpu.VMEM((bm, d), bf16)],
        compiler_params=_cparams(("parallel", "arbitrary")),
        name="inproj",
    )(x, gain.reshape(1, d), w)


def _merge_kernel(oa_ref, ob_ref, ga_ref, gb_ref, x_ref, wpa_ref, wpb_ref, wo_ref, o_ref):
    pa = jnp.dot(oa_ref[...], wpa_ref[...], preferred_element_type=f32)
    pb = jnp.dot(ob_ref[...], wpb_ref[...], preferred_element_type=f32)
    mix = jax.nn.sigmoid(ga_ref[...]) * pa + jax.nn.sigmoid(gb_ref[...]) * pb
    o_ref[...] = x_ref[...] + jnp.dot(mix.astype(bf16), wo_ref[...], preferred_element_type=f32)


def _merge(oa, ob, z, x, wpa, wpb, wo):
    m, d = x.shape
    bm = _row_tile(m, 256)
    ca = _OFF["g_a"][0] // d
    cb = _OFF["g_b"][0] // d
    return pl.pallas_call(
        _merge_kernel,
        out_shape=jax.ShapeDtypeStruct((m, d), f32),
        grid=(m // bm,),
        in_specs=[
            pl.BlockSpec((bm, NSA_DIM), lambda i: (i, 0)),
            pl.BlockSpec((bm, RW_DIM), lambda i: (i, 0)),
            pl.BlockSpec((bm, d), lambda i: (i, ca)),
            pl.BlockSpec((bm, d), lambda i: (i, cb)),
            pl.BlockSpec((bm, d), lambda i: (i, 0)),
            pl.BlockSpec((NSA_DIM, d), lambda i: (0, 0)),
            pl.BlockSpec((RW_DIM, d), lambda i: (0, 0)),
            pl.BlockSpec((d, d), lambda i: (0, 0)),
        ],
        out_specs=pl.BlockSpec((bm, d), lambda i: (i, 0)),
        compiler_params=_cparams(("parallel",)),
        name="merge",
    )(oa, ob, z, z, x, wpa, wpb, wo)


def _compress_kernel(x_ref, pe_ref, w_ref, o_ref):
    nb = o_ref.shape[1]
    acc = [jnp.zeros((nb, 128), f32), jnp.zeros((nb, 128), f32)]
    for j in range(CMP_BLOCK):
        pe_j = pe_ref[pl.ds(j, 1), :]
        w_j = w_ref[j]
        for h in range(2):
            xj = x_ref[0, pl.ds(2 * j + h, nb, stride=2 * CMP_BLOCK), :] + pe_j
            acc[h] = acc[h] + jnp.dot(xj.astype(bf16), w_j, preferred_element_type=f32)
    o_ref[0] = jnp.concatenate(acc, axis=-1)


def _blockdiag2(m):
    z = jnp.zeros_like(m)
    return jnp.concatenate([jnp.concatenate([m, z], axis=2), jnp.concatenate([z, m], axis=2)], axis=1)


def _compress(x, pe, w):
    b, l, _ = x.shape
    nb = l // CMP_BLOCK
    pe_t = jnp.tile(pe, (1, 2))
    wbd = _blockdiag2(w).astype(bf16)
    return pl.pallas_call(
        _compress_kernel,
        out_shape=jax.ShapeDtypeStruct((b, nb, KV_DIM), f32),
        grid=(b,),
        in_specs=[
            pl.BlockSpec((1, 2 * l, 128), lambda i: (i, 0, 0)),
            pl.BlockSpec((CMP_BLOCK, 128), lambda i: (0, 0)),
            pl.BlockSpec((CMP_BLOCK, 128, 128), lambda i: (0, 0, 0)),
        ],
        out_specs=pl.BlockSpec((1, nb, KV_DIM), lambda i: (i, 0, 0)),
        compiler_params=_cparams(("parallel",)),
        name="compress",
    )(x.reshape(b, 2 * l, 128), pe_t, wbd)


_TK = 512
_ROWS = GROUP * Q_BLOCK
_POS_FEATS = 16


def _softmax_cols(s):
    m = jnp.max(s, axis=0, keepdims=True)
    m = jnp.where(jnp.isfinite(m), m, 0.0)
    p = jnp.exp(s - m)
    return p / jnp.maximum(jnp.sum(p, axis=0, keepdims=True), 1e-30)


def _topk_mask_t(vt, n_pick, sel):
    nb = vt.shape[0]
    bi = lax.broadcasted_iota(jnp.int32, vt.shape, 0)
    for _ in range(n_pick):
        mx = jnp.max(vt, axis=0, keepdims=True)
        idx = jnp.min(jnp.where(vt == mx, bi, nb), axis=0, keepdims=True)
        hit = bi == idx
        sel = jnp.where(hit, 1.0, sel)
        vt = jnp.where(hit, -jnp.inf, vt)
    return sel


def _nsa_kernel(q_ref, gq_ref, gt_ref, ck_ref, cvt_ref, ke_ref, vst_ref, kw_ref, vwt_ref,
                o_ref, a0w_ref, act_ref):
    kv = pl.program_id(0)
    i = pl.program_id(1)
    nb = ck_ref.shape[1]
    n_sel = min(N_SEL, nb)
    wk = WINDOW + Q_BLOCK

    lane = lax.broadcasted_iota(jnp.int32, (1, _ROWS), 1)
    grp = lane // Q_BLOCK
    tl = lane % Q_BLOCK
    slope = jnp.exp2(-0.5 * (kv * GROUP + grp + 1).astype(f32))

    @pl.when(i == 0)
    def _():
        kroww = lax.broadcasted_iota(jnp.int32, (wk, 1), 0)
        dist = tl + WINDOW - kroww
        a0w_ref[...] = jnp.where((dist >= 0) & (dist < WINDOW), slope * dist.astype(f32), -NEG_BIG)

    q_raw = q_ref[...]
    scale = HEAD_DIM ** -0.5
    parts = []
    for g in range(GROUP):
        qg = q_raw[:, g * HEAD_DIM:(g + 1) * HEAD_DIM]
        ms = jnp.sum(qg * qg, axis=-1, keepdims=True) / HEAD_DIM
        parts.append((qg * lax.rsqrt(ms + NORM_EPS) * gq_ref[...] * scale).T)
    qt = jnp.concatenate(parts, axis=1).astype(bf16)
    t0 = i * Q_BLOCK
    tok = t0 + tl

    blk = lax.broadcasted_iota(jnp.int32, (nb, 1), 0)
    blk_mid = (blk * CMP_BLOCK).astype(f32) + (CMP_BLOCK - 1) / 2
    s = jnp.dot(ck_ref[0], qt, preferred_element_type=f32)
    s = s - slope * (tok.astype(f32) - blk_mid)
    s = jnp.where(blk * CMP_BLOCK + (CMP_BLOCK - 1) <= tok, s, -jnp.inf)
    p_c = _softmax_cols(s)
    o_c = jnp.dot(cvt_ref[0], p_c.astype(bf16), preferred_element_type=f32)

    w0 = pl.multiple_of(t0, Q_BLOCK)
    s = jnp.dot(kw_ref[0, pl.ds(w0, wk), :], qt, preferred_element_type=f32) - a0w_ref[...]
    kroww = lax.broadcasted_iota(jnp.int32, (wk, 1), 0)
    s = jnp.where(t0 - WINDOW + kroww >= 0, s, -jnp.inf)
    p_w = jnp.exp(s - jnp.max(s, axis=0, keepdims=True))
    o_w = jnp.dot(vwt_ref[0, :, pl.ds(w0, wk)], p_w.astype(bf16), preferred_element_type=f32)
    o_w = o_w / jnp.maximum(jnp.sum(p_w, axis=0, keepdims=True), 1e-30)

    imp = p_c[:, 0:Q_BLOCK]
    for g in range(1, GROUP):
        imp = imp + p_c[:, g * Q_BLOCK:(g + 1) * Q_BLOCK]
    tq = t0 + lax.broadcasted_iota(jnp.int32, (1, Q_BLOCK), 1)
    cur = tq // CMP_BLOCK
    forced = (blk == 0) | (blk == cur) | (blk == cur - 1)
    imp = jnp.where(forced, -jnp.inf, jnp.where(blk * CMP_BLOCK <= tq, imp, -1.0))
    sel_t = _topk_mask_t(imp, max(n_sel - 3, 0), forced.astype(f32))
    bias_t = jnp.where(sel_t > 0.0, 0.0, NEG_BIG).astype(bf16)
    jd = t0 // _TK
    bpt = _TK // CMP_BLOCK
    n_act = jnp.int32(0)
    for j in range(nb // bpt):
        used = (jnp.max(sel_t[j * bpt:(j + 1) * bpt, :]) > 0.0) & (j < jd)
        act_ref[n_act] = j
        n_act = n_act + used.astype(jnp.int32)
    act_ref[n_act] = jd
    s1 = slope.astype(bf16).astype(f32)
    s2 = (slope - s1).astype(bf16).astype(f32)
    s3 = ((slope - s1) - s2).astype(bf16).astype(f32)
    fr = lax.broadcasted_iota(jnp.int32, (_POS_FEATS, 1), 0)
    pieces = jnp.where(fr < 2, s1, jnp.where(fr < 4, s2, jnp.where(fr < 6, s3, 0.0))).astype(bf16)
    rhs = jnp.concatenate([qt, jnp.concatenate([bias_t] * GROUP, axis=1), pieces], axis=0)

    def scores(j):
        c0 = pl.multiple_of(j * _TK, _TK)
        return jnp.dot(ke_ref[0, pl.ds(c0, _TK), :], rhs, preferred_element_type=f32)

    def update(j, s, carry, causal):
        m, l, acc = carry
        c0 = pl.multiple_of(j * _TK, _TK)
        if causal:
            krow = lax.broadcasted_iota(jnp.int32, (_TK, 1), 0)
            s = jnp.where(tok - (c0 + krow) >= 0, s, -jnp.inf)
        off = slope * (tok - c0).astype(f32)
        m_new = jnp.maximum(m, jnp.max(s, axis=0, keepdims=True) - off)
        p = jnp.exp(s - (m_new + off))
        alpha = jnp.exp(m - m_new)
        l = alpha * l + jnp.sum(p, axis=0, keepdims=True)
        acc = alpha * acc + jnp.dot(vst_ref[0, :, pl.ds(c0, _TK)], p.astype(bf16),
                                    preferred_element_type=f32)
        return m_new, l, acc

    init = (jnp.full((1, _ROWS), -jnp.inf, f32), jnp.zeros((1, _ROWS), f32),
            jnp.zeros((HEAD_DIM, _ROWS), f32))

    def body(n, carry):
        s_next = scores(act_ref[n + 1])
        return update(act_ref[n], carry[3], carry[:3], False) + (s_next,)

    carry = lax.fori_loop(0, n_act, body, init + (scores(act_ref[0]),))
    _, l, acc = update(jd, carry[3], carry[:3], True)
    o_s = acc / jnp.maximum(l, 1e-30)

    gt = gt_ref[0, 0]
    out = gt[0:1] * o_c + gt[1:2] * o_s + gt[2:3] * o_w
    o_ref[...] = jnp.concatenate([out[:, g * Q_BLOCK:(g + 1) * Q_BLOCK].T for g in range(GROUP)],
                                 axis=1).astype(o_ref.dtype)


def _nsa_prompt(z, g_q, gates, ck, cv, ks, vs, kw, vw):
    t = z.shape[0]
    nqb = t // Q_BLOCK
    nb = ck.shape[0]
    qcol = _OFF["q"][0] // (GROUP * HEAD_DIM)
    gt = gates.reshape(nqb, Q_BLOCK, N_KV, GROUP, 3).transpose(2, 0, 4, 3, 1).reshape(N_KV, nqb, 3, _ROWS)
    ckh = ck.astype(bf16).transpose(1, 0, 2)
    cvt = cv.astype(bf16).transpose(1, 2, 0)
    pos = jnp.arange(t)
    onehot = (pos[:, None] // CMP_BLOCK == jnp.arange(nb)[None, :]).astype(bf16)
    off_hi = (pos % _TK) // 16 * 16
    off_lo = pos % 16
    feats = jnp.stack([off_hi, off_lo] * 3 + [jnp.zeros_like(pos)] * (_POS_FEATS - 6), axis=1).astype(bf16)
    ke = jnp.concatenate([ks.astype(bf16).transpose(1, 0, 2),
                          jnp.broadcast_to(jnp.concatenate([onehot, feats], axis=1)[None],
                                           (N_KV, t, nb + _POS_FEATS))], axis=-1)
    vst = vs.astype(bf16).transpose(1, 2, 0)
    kwh = jnp.pad(kw.astype(bf16).transpose(1, 0, 2), ((0, 0), (WINDOW, 0), (0, 0)))
    vwt = jnp.pad(vw.astype(bf16).transpose(1, 2, 0), ((0, 0), (0, 0), (WINDOW, 0)))
    kd = HEAD_DIM + nb + _POS_FEATS
    return pl.pallas_call(
        _nsa_kernel,
        out_shape=jax.ShapeDtypeStruct((t, NSA_DIM), bf16),
        grid=(N_KV, nqb),
        in_specs=[
            pl.BlockSpec((Q_BLOCK, GROUP * HEAD_DIM), lambda k, i: (i, qcol + k)),
            pl.BlockSpec((1, HEAD_DIM), lambda k, i: (0, 0)),
            pl.BlockSpec((1, 1, 3, _ROWS), lambda k, i: (k, i, 0, 0)),
            pl.BlockSpec((1, nb, HEAD_DIM), lambda k, i: (k, 0, 0)),
            pl.BlockSpec((1, HEAD_DIM, nb), lambda k, i: (k, 0, 0)),
            pl.BlockSpec((1, t, kd), lambda k, i: (k, 0, 0)),
            pl.BlockSpec((1, HEAD_DIM, t), lambda k, i: (k, 0, 0)),
            pl.BlockSpec((1, t + WINDOW, HEAD_DIM), lambda k, i: (k, 0, 0)),
            pl.BlockSpec((1, HEAD_DIM, t + WINDOW), lambda k, i: (k, 0, 0)),
        ],
        out_specs=pl.BlockSpec((Q_BLOCK, GROUP * HEAD_DIM), lambda k, i: (i, k)),
        scratch_shapes=[pltpu.VMEM((WINDOW + Q_BLOCK, _ROWS), f32),
                        pltpu.SMEM((nb // (_TK // CMP_BLOCK) + 1,), jnp.int32)],
        compiler_params=_cparams(("arbitrary", "arbitrary")),
        name="nsa_prompt",
    )(z, g_q.reshape(1, HEAD_DIM), gt, ckh, cvt, ke, vst, kwh, vwt)


def _pair_sum(x, low):
    s_lo = jnp.sum(jnp.where(low, x, 0.0), axis=-1, keepdims=True)
    s_hi = jnp.sum(jnp.where(low, 0.0, x), axis=-1, keepdims=True)
    return jnp.where(low, s_lo, s_hi)


def _rw_prep_kernel(shift_rows, *refs):
    cur, refs = refs[:6], refs[6:]
    prv, refs = refs[:6], refs[6:]
    if shift_rows:
        st0, refs = refs[:6], refs[6:]
    mus, refs = refs[:6], refs[6:]
    (w0_ref, w2_ref, a0_ref, a2_ref, g2_ref, kk_ref, ka_ref, rk_ref,
     ro_ref, ko_ref, vo_ref, kko_ref, kao_ref, ldo_ref, go_ref, bo_ref) = refs
    first = pl.program_id(0) == 0

    def shifted(n):
        x = cur[n][...]
        if shift_rows:
            prev_row = jnp.where(first, st0[n][pl.ds(7, 1), :], prv[n][pl.ds(7, 1), :])
            rolled = pltpu.roll(x, 1, axis=0)
            rid = lax.broadcasted_iota(jnp.int32, x.shape, 0)
            xp = jnp.where(rid == 0, prev_row, rolled)
        else:
            xp = prv[n][...]
        return x + (xp - x) * mus[n][...]

    r, k, v, gl, wl, al = (shifted(n) for n in range(6))

    y = -(w0_ref[...] + jnp.dot(jnp.tanh(wl).astype(bf16), w2_ref[...], preferred_element_type=f32))
    softplus = jnp.maximum(y, 0.0) + jnp.log1p(jnp.exp(-jnp.abs(y)))
    w_log = -softplus - 0.5
    ld = -jnp.exp(w_log)
    a = jax.nn.sigmoid(a0_ref[...] + jnp.dot(al.astype(bf16), a2_ref[...], preferred_element_type=f32))
    g = jnp.dot(jax.nn.sigmoid(gl).astype(bf16), g2_ref[...], preferred_element_type=f32)
    kk = k * kk_ref[...]
    k2 = k * (1.0 + (a - 1.0) * ka_ref[...])
    rkr = r * k2 * rk_ref[...]
    low = lax.broadcasted_iota(jnp.int32, (1, 2 * RW_N), 1) < RW_N
    for hp in range(RW_HEADS // 2):
        sl = slice(hp * 2 * RW_N, (hp + 1) * 2 * RW_N)
        kkp = kk[:, sl]
        kkp = kkp / jnp.maximum(jnp.sqrt(_pair_sum(kkp * kkp, low)), 1e-12)
        ro_ref[hp] = r[:, sl]
        ko_ref[hp] = k2[:, sl]
        vo_ref[hp] = v[:, sl]
        kko_ref[hp] = kkp
        kao_ref[hp] = kkp * a[:, sl]
        ldo_ref[hp] = ld[:, sl]
        go_ref[hp] = g[:, sl]
        bo_ref[hp] = _pair_sum(rkr[:, sl], low) * v[:, sl]


def _rw_prep(z, prev, mu_p, w0, w2p, a0, a2p, g2, k_k, k_a, r_k, shift_rows):
    m = z.shape[0]
    tm = _row_tile(m, 256)
    names = ("r", "k", "v", "gl", "wl", "al")

    def col_spec(name, rows, imap):
        o, _, pw = _OFF[name]
        return pl.BlockSpec((rows, pw), functools.partial(imap, o // pw))

    cur = [col_spec(n, tm, lambda c, i: (i, c)) for n in names]
    if shift_rows:
        blk8 = tm // 8
        prv = [col_spec(n, 8, lambda c, i: (jnp.maximum(i * blk8 - 1, 0), c)) for n in names]
        prv += [col_spec(n, 8, lambda c, i: (0, c)) for n in names]
        prev_args = [z] * 6 + [prev] * 6
    else:
        prv = [col_spec(n, tm, lambda c, i: (i, c)) for n in names]
        prev_args = [prev] * 6
    mus = [col_spec(n, 1, lambda c, i: (0, c)) for n in names]
    vec = pl.BlockSpec((1, RW_DIM), lambda i: (0, 0))
    out_spec = pl.BlockSpec((RW_HEADS // 2, tm, 2 * RW_N), lambda i: (0, i, 0))
    outs = pl.pallas_call(
        functools.partial(_rw_prep_kernel, shift_rows),
        out_shape=[jax.ShapeDtypeStruct((RW_HEADS // 2, m, 2 * RW_N), f32)] * 8,
        grid=(m // tm,),
        in_specs=cur + prv + mus + [
            vec,
            pl.BlockSpec((128, RW_DIM), lambda i: (0, 0)),
            vec,
            pl.BlockSpec((128, RW_DIM), lambda i: (0, 0)),
            pl.BlockSpec((G_LORA, RW_DIM), lambda i: (0, 0)),
            vec, vec, vec,
        ],
        out_specs=[out_spec] * 8,
        compiler_params=_cparams(("parallel",)),
        name="rw_prep",
    )(*([z] * 6), *prev_args, *([mu_p] * 6),
      w0.reshape(1, RW_DIM), w2p, a0.reshape(1, RW_DIM), a2p, g2,
      k_k.reshape(1, RW_DIM), k_a.reshape(1, RW_DIM), r_k.reshape(1, RW_DIM))
    return outs


_CH = 64
_HEADS_PER_STEP = 16
_NN = (((1,), (0,)), ((), ()))
_NT = (((1,), (1,)), ((), ()))
_TN = (((0,), (0,)), ((), ()))


def _split2(x):
    hi = x.astype(bf16)
    lo = (x - hi.astype(f32)).astype(bf16)
    return hi, lo


def _dot3(a, b, dims=_NN):
    ah, al = _split2(a)
    bh, bl = _split2(b)
    d = functools.partial(lax.dot_general, dimension_numbers=dims, preferred_element_type=f32)
    return d(ah, bh) + (d(ah, bl) + d(al, bh))


def _split3(x):
    x1 = x.astype(bf16)
    r1 = x - x1.astype(f32)
    x2 = r1.astype(bf16)
    return x1, x2, (r1 - x2.astype(f32)).astype(bf16)


def _dot_exact_rhs(a01, b):
    b1, b2, b3 = _split3(b)
    d = functools.partial(jnp.dot, preferred_element_type=f32)
    return d(a01, b1) + (d(a01, b2) + d(a01, b3))


def _dot_exact_lhs_tn(a, b01):
    a1, a2, a3 = _split3(a)
    d = functools.partial(lax.dot_general, dimension_numbers=_TN, preferred_element_type=f32)
    return d(a1, b01) + (d(a2, b01) + d(a3, b01))


def _wkv_chunk_kernel(r_ref, k_ref, v_ref, kk_ref, ka_ref, ld_ref, s0_ref, y_ref, sT_ref, st_ref):
    c = pl.program_id(0)

    @pl.when(c == 0)
    def _():
        st_ref[...] = s0_ref[...]

    ti = lax.broadcasted_iota(jnp.int32, (_CH, _CH), 0)
    si = lax.broadcasted_iota(jnp.int32, (_CH, _CH), 1)
    incl = ti >= si
    strict = ti > si
    l_incl = incl.astype(bf16)
    eye = (ti == si).astype(f32)
    ones = jnp.ones((_CH, RW_N), bf16)
    same_blk = [(ti // b) == (si // b) for b in (8, 16, 32, 64)]

    def each(f, *lists):
        return [f(*a) for a in zip(*lists)]

    def cat0(a, b):
        return jnp.concatenate([a, b], axis=0)

    def cat1(a, b):
        return jnp.concatenate([a, b], axis=1)

    def heads_step(heads):
        def load(ref):
            return [ref[hp, :, sub * RW_N:(sub + 1) * RW_N] for hp, sub in heads]

        r, k, v, kk, ka, ld = (load(ref) for ref in (r_ref, k_ref, v_ref, kk_ref, ka_ref, ld_ref))
        lp = each(lambda x: _dot_exact_rhs(l_incl, x), ld)
        lp_end = each(lambda x: _dot_exact_lhs_tn(x, ones), ld)
        e_neg = each(lambda x: jnp.exp(-x), lp)
        at = each(lambda a, x, y: -a * jnp.exp(x - y), kk, lp, ld)
        bt = each(jnp.multiply, ka, e_neg)
        kt = each(jnp.multiply, k, e_neg)
        rt = each(lambda a, x: a * jnp.exp(x), r, lp)
        e_end = each(lambda x: jnp.exp(x[_CH - 1:_CH, :] - x), lp)
        bh = each(jnp.multiply, ka, e_end)
        kh = each(jnp.multiply, k, e_end)
        sc = each(lambda a, b: _dot3(a, b, _NT), each(cat0, at, rt), each(cat0, bt, kt))
        a_b = each(lambda x: jnp.where(strict, x[:_CH, :_CH], 0.0), sc)
        a_k = each(lambda x: jnp.where(strict, x[:_CH, _CH:], 0.0), sc)
        g_b = each(lambda x: jnp.where(incl, x[_CH:, :_CH], 0.0), sc)
        g_k = each(lambda x: jnp.where(incl, x[_CH:, _CH:], 0.0), sc)
        pw = each(lambda x: jnp.where(same_blk[0], x, 0.0), a_b)
        tm = each(lambda x: eye + x, pw)
        for _ in range(2):
            pw = each(_dot3, pw, pw)
            tm = each(lambda t_, p_: t_ + _dot3(t_, p_), tm, pw)
        for lvl in range(1, len(same_blk)):
            off = each(lambda x: jnp.where(same_blk[lvl] & ~same_blk[lvl - 1], x, 0.0), a_b)
            tm = each(lambda t_, c_: t_ + _dot3(_dot3(t_, c_), t_), tm, off)
        akv = each(_dot3, a_k, v)
        tx = each(_dot3, tm, each(cat1, at, akv))
        st = [st_ref[2 * hp + sub] for hp, sub in heads]
        ws = each(_dot3, each(lambda x, y: cat0(x[:, :RW_N], y), tx, rt), st)
        u = each(lambda a, x: a[:_CH] + x[:, RW_N:], ws, tx)
        uv = each(cat0, u, v)
        y = each(lambda a, gb, gk, x: a[_CH:] + _dot3(cat1(gb, gk), x), ws, g_b, g_k, uv)
        st_new = each(lambda e, s_, b_, k_, x: jnp.exp(e) * s_ + _dot3(cat0(b_, k_), x, _TN),
                      lp_end, st, bh, kh, uv)
        for (hp, sub), s_ in zip(heads, st_new):
            st_ref[2 * hp + sub] = s_
        return y

    def pairs(i, carry):
        heads = [(_HEADS_PER_STEP // 2 * i + a, sub) for a in range(_HEADS_PER_STEP // 2) for sub in range(2)]
        y = heads_step(heads)
        for a in range(_HEADS_PER_STEP // 2):
            y_ref[_HEADS_PER_STEP // 2 * i + a] = cat1(y[2 * a], y[2 * a + 1])
        return carry

    lax.fori_loop(0, RW_HEADS // _HEADS_PER_STEP, pairs, 0)

    @pl.when(c == pl.num_programs(0) - 1)
    def _():
        sT_ref[...] = st_ref[...]


def _wkv_chunks(r, k, v, kk, ka, ld, s0t):
    t = r.shape[1]
    spec = pl.BlockSpec((RW_HEADS // 2, _CH, 2 * RW_N), lambda c: (0, c, 0))
    sspec = pl.BlockSpec((RW_HEADS, RW_N, RW_N), lambda c: (0, 0, 0))
    return pl.pallas_call(
        _wkv_chunk_kernel,
        out_shape=[jax.ShapeDtypeStruct((RW_HEADS // 2, t, 2 * RW_N), f32),
                   jax.ShapeDtypeStruct((RW_HEADS, RW_N, RW_N), f32)],
        grid=(t // _CH,),
        in_specs=[spec] * 6 + [sspec],
        out_specs=[spec, sspec],
        scratch_shapes=[pltpu.VMEM((RW_HEADS, RW_N, RW_N), f32)],
        compiler_params=_cparams(("arbitrary",)),
        name="wkv_chunks",
    )(r, k, v, kk, ka, ld, s0t)


def _rw_post_kernel(y_ref, b_ref, g_ref, lw_ref, lb_ref, o_ref):
    low = lax.broadcasted_iota(jnp.int32, (1, 2 * RW_N), 1) < RW_N
    for hp in range(RW_HEADS // 2):
        y = y_ref[hp]
        mu = _pair_sum(y, low) / RW_N
        var = _pair_sum(jnp.square(y - mu), low) / RW_N
        yn = (y - mu) * lax.rsqrt(var + GN_EPS) * lw_ref[hp] + lb_ref[hp]
        o_ref[:, hp * 2 * RW_N:(hp + 1) * 2 * RW_N] = ((yn + b_ref[hp]) * g_ref[hp]).astype(o_ref.dtype)


def _rw_post(y, bonus, g, ln_w, ln_b):
    m = y.shape[1]
    tm = _row_tile(m, 512)
    spec = pl.BlockSpec((RW_HEADS // 2, tm, 2 * RW_N), lambda i: (0, i, 0))
    pspec = pl.BlockSpec((RW_HEADS // 2, 1, 2 * RW_N), lambda i: (0, 0, 0))
    return pl.pallas_call(
        _rw_post_kernel,
        out_shape=jax.ShapeDtypeStruct((m, RW_DIM), bf16),
        grid=(m // tm,),
        in_specs=[spec, spec, spec, pspec, pspec],
        out_specs=pl.BlockSpec((tm, RW_DIM), lambda i: (i, 0)),
        compiler_params=_cparams(("parallel",)),
        name="rw_post",
    )(y, bonus, g, ln_w.reshape(RW_HEADS // 2, 1, 2 * RW_N), ln_b.reshape(RW_HEADS // 2, 1, 2 * RW_N))


def _lora_pad(w):
    return jnp.pad(w, ((0, 128 - w.shape[0]), (0, 0))).astype(bf16)


def _rwkv_prompt(z, shift0_p, s0t, p):
    r, k, v, kk, ka, ld, g, bonus = _rw_prep(
        z, shift0_p, p["mu_p"], p["rw_w0"], p["w2p"], p["rw_a0"], p["a2p"], p["g2"],
        p["rw_k_k"], p["rw_k_a"], p["rw_r_k"], True)
    y, st = _wkv_chunks(r, k, v, kk, ka, ld, s0t)
    return _rw_post(y, bonus, g, p["rw_ln_w"], p["rw_ln_b"]), st


def _wkv_step_kernel(s_ref, r_ref, k_ref, vc_ref, kk_ref, ka_ref, ld_ref, y_ref, so_ref):
    s = s_ref[0]
    kk = kk_ref[0][:, None, :]
    sa = jnp.sum(s * (-kk), axis=-1, keepdims=True)
    s = (s * jnp.exp(ld_ref[0])[:, None, :] + sa * ka_ref[0][:, None, :]
         + vc_ref[0] * k_ref[0][:, None, :])
    so_ref[0] = s
    y_ref[0] = jnp.sum(s * r_ref[0][:, None, :], axis=-1, keepdims=True)


def _wkv_step(s0, r, k, v, kk, ka, ld):
    b = s0.shape[0]
    sspec = pl.BlockSpec((1, RW_HEADS, RW_N, RW_N), lambda i: (i, 0, 0, 0))
    vspec = pl.BlockSpec((1, RW_HEADS, RW_N), lambda i: (i, 0, 0))
    cspec = pl.BlockSpec((1, RW_HEADS, RW_N, 1), lambda i: (i, 0, 0, 0))
    y, s1 = pl.pallas_call(
        _wkv_step_kernel,
        out_shape=[jax.ShapeDtypeStruct((b, RW_HEADS, RW_N, 1), f32),
                   jax.ShapeDtypeStruct(s0.shape, f32)],
        grid=(b,),
        in_specs=[sspec, vspec, vspec, cspec, vspec, vspec, vspec],
        out_specs=[cspec, sspec],
        compiler_params=_cparams(("parallel",)),
        name="wkv_step",
    )(s0, r, k, v[..., None], kk, ka, ld)
    return y[..., 0], s1


_GP = 8


def _compress_pages_kernel(npg, pt_ref, pool_ref, pe_ref, w_ref, o_ref, buf0, buf1, sem):
    g = pl.program_id(0)
    n_steps = pl.num_programs(0)

    def page_copies(seq, buf, slot):
        out = []
        for pg in range(npg):
            page_id = pt_ref[seq, pg]
            for kv in range(N_KV):
                out.append(pltpu.make_async_copy(pool_ref.at[page_id, kv], buf.at[:, pg * N_KV + kv, :],
                                                 sem.at[slot]))
        return out

    def start(seq, buf, slot):
        for cp in page_copies(seq, buf, slot):
            cp.start()

    def wait(seq, buf, slot):
        for cp in page_copies(seq, buf, slot):
            cp.wait()

    def compress(buf, out_slot):
        acc = jnp.zeros((npg * N_KV, 2 * CMP_BLOCK), f32)
        for d2 in range(HEAD_DIM // 2):
            x = jnp.concatenate([buf[2 * d2 + e] + pe_ref[pl.ds(2 * d2 + e, 1), :] for e in range(2)], axis=1)
            acc = acc + jnp.dot(x.astype(bf16), w_ref[d2], preferred_element_type=f32)
        o_ref[out_slot] = acc

    @pl.when(g == 0)
    def _():
        start(0, buf0, 0)

    start(2 * g + 1, buf1, 1)
    wait(2 * g, buf0, 0)
    compress(buf0, 0)

    @pl.when(g + 1 < n_steps)
    def _():
        start(2 * g + 2, buf0, 0)

    wait(2 * g + 1, buf1, 1)
    compress(buf1, 1)


def _compress_pages(pool_t, table, pe, w):
    b, npg = table.shape
    page = pool_t.shape[-1]
    assert b % 2 == 0 and page == 2 * CMP_BLOCK
    pe_t = jnp.tile(pe.T, (1, 2))
    wd = _blockdiag2(w.transpose(1, 0, 2)).reshape(HEAD_DIM // 2, 2 * page, page).astype(bf16)
    out = pl.pallas_call(
        functools.partial(_compress_pages_kernel, npg),
        out_shape=jax.ShapeDtypeStruct((b, npg * N_KV, page), f32),
        grid_spec=pltpu.PrefetchScalarGridSpec(
            num_scalar_prefetch=1,
            grid=(b // 2,),
            in_specs=[
                pl.BlockSpec(memory_space=pl.ANY),
                pl.BlockSpec((HEAD_DIM, page), lambda i, pt: (0, 0)),
                pl.BlockSpec((HEAD_DIM // 2, 2 * page, page), lambda i, pt: (0, 0, 0)),
            ],
            out_specs=pl.BlockSpec((2, npg * N_KV, page), lambda i, pt: (i, 0, 0)),
            scratch_shapes=[pltpu.VMEM((HEAD_DIM, npg * N_KV, page), f32),
                            pltpu.VMEM((HEAD_DIM, npg * N_KV, page), f32),
                            pltpu.SemaphoreType.DMA((2,))]),
        compiler_params=_cparams(("arbitrary",)),
        name="compress_pages",
    )(table, pool_t, pe_t, wd)
    out = out.reshape(b, npg, N_KV, 2, HEAD_DIM).transpose(0, 1, 3, 2, 4)
    return out.reshape(b, 2 * npg, N_KV, HEAD_DIM)


def _dec_cmp_kernel(n_pick, t_pos, q_ref, ckt_ref, cv_ref, xk_ref, xv_ref, wk_ref, wv_ref, gkc_ref,
                    oc_ref, idx_ref):
    sb = q_ref.shape[0]
    nb = ckt_ref.shape[-1]
    kd = xk_ref.shape[-1]
    blk = lax.broadcasted_iota(jnp.int32, (1, nb), 1)
    blk_mid = (blk * CMP_BLOCK).astype(f32) + (CMP_BLOCK - 1) / 2
    ck_new = _rms(jnp.dot(xk_ref[...].reshape(sb * _GP, kd).astype(bf16), wk_ref[...],
                          preferred_element_type=f32), gkc_ref[...])
    cv_new = jnp.dot(xv_ref[...].reshape(sb * _GP, kd).astype(bf16), wv_ref[...],
                     preferred_element_type=f32)
    new_mid = float(nb * CMP_BLOCK) + (CMP_BLOCK - 1) / 2
    new_ok = nb * CMP_BLOCK + (CMP_BLOCK - 1) <= t_pos
    lane = lax.broadcasted_iota(jnp.int32, (1, 128), 1)
    imps = []
    for r in range(sb):
        for kv in range(N_KV):
            row = r * _GP + kv
            q = q_ref[r, kv]
            g1 = lax.broadcasted_iota(jnp.int32, (_GP, 1), 0) + (kv * GROUP + 1)
            slope = jnp.exp2(-0.5 * g1.astype(f32))
            s = jnp.dot(q.astype(bf16), ckt_ref[r, kv], preferred_element_type=f32)
            s = s - slope * (float(t_pos) - blk_mid)
            s = jnp.where(blk * CMP_BLOCK + (CMP_BLOCK - 1) <= t_pos, s, -jnp.inf)
            qn = q.astype(bf16).astype(f32)
            s_new = jnp.sum(qn * ck_new[row:row + 1].astype(bf16).astype(f32), axis=-1, keepdims=True)
            s_new = s_new - slope * (float(t_pos) - new_mid)
            s_new = jnp.where(new_ok, s_new, -jnp.inf)
            m = jnp.maximum(jnp.max(s, axis=-1, keepdims=True), s_new)
            m = jnp.where(jnp.isfinite(m), m, 0.0)
            p = jnp.exp(s - m)
            p_new = jnp.exp(s_new - m)
            den = jnp.maximum(jnp.sum(p, axis=-1, keepdims=True) + p_new, 1e-30)
            p = p / den
            p_new = p_new / den
            oc = jnp.dot(p.astype(bf16), cv_ref[r, kv], preferred_element_type=f32)
            oc_ref[r, kv] = oc + p_new * cv_new[row:row + 1]
            imps.append(jnp.sum(p[:GROUP], axis=0, keepdims=True))
    imp = jnp.concatenate(imps, axis=0)
    cur = t_pos // CMP_BLOCK
    forced = (blk == 0) | (blk == cur) | (blk == cur - 1)
    v = jnp.where(forced, FORCED_SCORE, jnp.where(blk * CMP_BLOCK <= t_pos, imp, -1.0))
    out = jnp.zeros((sb * N_KV, 128), jnp.int32)
    for it in range(n_pick):
        mx = jnp.max(v, axis=-1, keepdims=True)
        idx = jnp.min(jnp.where(v == mx, blk, nb), axis=-1, keepdims=True)
        out = jnp.where(lane == it, idx, out)
        v = jnp.where(blk == idx, -jnp.inf, v)
    idx_ref[...] = out.reshape(sb, N_KV, 128)


def _dec_cmp(q, ck, cv, xk, xv, wk, wv, g_kc, t_pos, n_pick):
    b, nb = ck.shape[:2]
    ckt = ck.astype(bf16).transpose(0, 2, 3, 1)
    cvh = cv.astype(bf16).transpose(0, 2, 1, 3)
    kd = CMP_BLOCK * HEAD_DIM
    sb = 8 if b % 8 == 0 else b
    return pl.pallas_call(
        functools.partial(_dec_cmp_kernel, n_pick, t_pos),
        out_shape=[jax.ShapeDtypeStruct((b, N_KV, _GP, HEAD_DIM), f32),
                   jax.ShapeDtypeStruct((b, N_KV, 128), jnp.int32)],
        grid=(b // sb,),
        in_specs=[
            pl.BlockSpec((sb, N_KV, _GP, HEAD_DIM), lambda i: (i, 0, 0, 0)),
            pl.BlockSpec((sb, N_KV, HEAD_DIM, nb), lambda i: (i, 0, 0, 0)),
            pl.BlockSpec((sb, N_KV, nb, HEAD_DIM), lambda i: (i, 0, 0, 0)),
            pl.BlockSpec((sb, _GP, kd), lambda i: (i, 0, 0)),
            pl.BlockSpec((sb, _GP, kd), lambda i: (i, 0, 0)),
            pl.BlockSpec((kd, HEAD_DIM), lambda i: (0, 0)),
            pl.BlockSpec((kd, HEAD_DIM), lambda i: (0, 0)),
            pl.BlockSpec((1, HEAD_DIM), lambda i: (0, 0)),
        ],
        out_specs=[pl.BlockSpec((sb, N_KV, _GP, HEAD_DIM), lambda i: (i, 0, 0, 0)),
                   pl.BlockSpec((sb, N_KV, 128), lambda i: (i, 0, 0))],
        compiler_params=_cparams(("parallel",)),
        name="dec_cmp",
    )(q, ckt, cvh, xk, xv, wk, wv, g_kc.reshape(1, HEAD_DIM))


def _dec_sel_kernel(t_pos, n_pick, pt_ref, idx_ref, q_ref, gt_ref, oc_ref, ksn_ref, vsn_ref, kwn_ref,
                    vwn_ref, wk_ref, wv_ref, pk_ref, pv_ref, o_ref, kb0, vb0, kb1, vb1, sem):
    g = pl.program_id(0)
    n_steps = pl.num_programs(0)
    n_buf = wk_ref.shape[-1]
    page = 2 * CMP_BLOCK

    def copies(seq, kb, vb, slot):
        out = []
        for kv in range(N_KV):
            for s in range(n_pick):
                pg = pt_ref[seq, idx_ref[seq, kv * 128 + s] // 2]
                dst = pl.ds(s * page, page)
                out.append(pltpu.make_async_copy(pk_ref.at[pg, kv], kb.at[kv, :, dst], sem.at[slot]))
                out.append(pltpu.make_async_copy(pv_ref.at[pg, kv], vb.at[kv, :, dst], sem.at[slot]))
        return out

    def start(seq, kb, vb, slot):
        for cp in copies(seq, kb, vb, slot):
            cp.start()

    def wait(seq, kb, vb, slot):
        for cp in copies(seq, kb, vb, slot):
            cp.wait()

    def slopes(kv):
        g1 = lax.broadcasted_iota(jnp.int32, (_GP, 1), 0) + (kv * GROUP + 1)
        return jnp.exp2(-0.5 * g1.astype(f32))

    lane = lax.broadcasted_iota(jnp.int32, (1, page), 1)
    c = lax.broadcasted_iota(jnp.int32, (1, n_buf), 1)
    kpos = t_pos - n_buf + c
    distw = t_pos - kpos
    okw = (kpos >= 0) & (distw >= 0) & (distw < WINDOW)

    def attend(seq, r, kb, vb):
        for kv in range(N_KV):
            q = q_ref[r, kv].astype(bf16)
            qf = q.astype(f32)
            slope = slopes(kv)
            dist, ok = [], []
            for s in range(n_pick):
                blk = idx_ref[seq, kv * 128 + s]
                d = t_pos - ((blk // 2) * page + lane)
                dist.append(d)
                ok.append((lane // CMP_BLOCK == blk % 2) & (d >= 0))
            dist = jnp.concatenate(dist, axis=1)
            ok = jnp.concatenate(ok, axis=1)
            s_sel = jnp.dot(q, kb[kv].astype(bf16), preferred_element_type=f32)
            s_sel = jnp.where(ok, s_sel - slope * dist.astype(f32), -jnp.inf)
            s_new = jnp.sum(qf * ksn_ref[r, kv:kv + 1].astype(bf16).astype(f32), axis=-1, keepdims=True)
            m = jnp.maximum(jnp.max(s_sel, axis=-1, keepdims=True), s_new)
            p = jnp.exp(s_sel - m)
            p_new = jnp.exp(s_new - m)
            den = jnp.maximum(jnp.sum(p, axis=-1, keepdims=True) + p_new, 1e-30)
            o_s = lax.dot_general((p / den).astype(bf16), vb[kv].astype(bf16), _NT, preferred_element_type=f32)
            o_s = o_s + (p_new / den).astype(bf16).astype(f32) * vsn_ref[r, kv:kv + 1].astype(bf16).astype(f32)
            sw = jnp.dot(q, wk_ref[r, kv].astype(bf16), preferred_element_type=f32)
            sw = jnp.where(okw, sw - slope * distw.astype(f32), -jnp.inf)
            sw_new = jnp.sum(qf * kwn_ref[r, kv:kv + 1].astype(bf16).astype(f32), axis=-1, keepdims=True)
            mw = jnp.maximum(jnp.max(sw, axis=-1, keepdims=True), sw_new)
            pw = jnp.exp(sw - mw)
            pw_new = jnp.exp(sw_new - mw)
            denw = jnp.maximum(jnp.sum(pw, axis=-1, keepdims=True) + pw_new, 1e-30)
            o_w = lax.dot_general((pw / denw).astype(bf16), wv_ref[r, kv].astype(bf16), _NT,
                                  preferred_element_type=f32)
            o_w = o_w + (pw_new / denw).astype(bf16).astype(f32) * vwn_ref[r, kv:kv + 1].astype(bf16).astype(f32)
            gt = gt_ref[r, kv]
            o_ref[r, kv] = gt[:, 0:1] * oc_ref[r, kv] + gt[:, 1:2] * o_s + gt[:, 2:3] * o_w

    @pl.when(g == 0)
    def _():
        start(0, kb0, vb0, 0)

    start(2 * g + 1, kb1, vb1, 1)
    wait(2 * g, kb0, vb0, 0)
    attend(2 * g, 0, kb0, vb0)

    @pl.when(g + 1 < n_steps)
    def _():
        start(2 * g + 2, kb0, vb0, 0)

    wait(2 * g + 1, kb1, vb1, 1)
    attend(2 * g + 1, 1, kb1, vb1)


def _dec_sel(q, gates, o_c, idx, table, pool_k, pool_v, ks_new, vs_new, kw_new, vw_new, win_k, win_v,
             t_pos, n_pick):
    b = q.shape[0]
    assert b % 2 == 0
    n_buf = win_k.shape[-1]
    page = pool_k.shape[-1]
    hspec = pl.BlockSpec((2, N_KV, _GP, HEAD_DIM), lambda i, pt, ix: (i, 0, 0, 0))
    nspec = pl.BlockSpec((2, N_KV, HEAD_DIM), lambda i, pt, ix: (i, 0, 0))
    wspec = pl.BlockSpec((2, N_KV, HEAD_DIM, n_buf), lambda i, pt, ix: (i, 0, 0, 0))
    anyspec = pl.BlockSpec(memory_space=pl.ANY)
    gbuf = pltpu.VMEM((N_KV, HEAD_DIM, n_pick * page), f32)
    return pl.pallas_call(
        functools.partial(_dec_sel_kernel, t_pos, n_pick),
        out_shape=jax.ShapeDtypeStruct((b, N_KV, _GP, HEAD_DIM), f32),
        grid_spec=pltpu.PrefetchScalarGridSpec(
            num_scalar_prefetch=2,
            grid=(b // 2,),
            in_specs=[hspec, pl.BlockSpec((2, N_KV, _GP, 3), lambda i, pt, ix: (i, 0, 0, 0)), hspec,
                      nspec, nspec, nspec, nspec, wspec, wspec, anyspec, anyspec],
            out_specs=hspec,
            scratch_shapes=[gbuf, gbuf, gbuf, gbuf, pltpu.SemaphoreType.DMA((2,))]),
        compiler_params=_cparams(("arbitrary",)),
        name="dec_sel",
    )(table, idx, q, gates, o_c, ks_new, vs_new, kw_new, vw_new, win_k, win_v, pool_k, pool_v)


def _head_rms(x, g):
    ms = jnp.mean(x * x, axis=-1, keepdims=True)
    return x * lax.rsqrt(ms + NORM_EPS) * g


def _prepare(p):
    q = dict(p)
    for n in ("ffn1_gate", "ffn1_up", "ffn1_down", "ffn2_gate", "ffn2_up", "ffn2_down", "w_pa", "w_pb", "w_out"):
        q[n] = p[n].astype(bf16)
    q["w_in_p"] = _pad_cols(p["w_in"]).astype(bf16)
    q["mu_p"] = _pad_rw_cols(p["rw_mu"][None])
    q["w2p"] = _lora_pad(p["rw_w2"])
    q["a2p"] = _lora_pad(p["rw_a2"])
    q["g2"] = p["rw_g2"].astype(bf16)
    return q


def _nsa_proj(z, p):
    m = z.shape[0]
    kvs = (m, N_KV, HEAD_DIM)
    q = _head_rms(_seg(z, "q").reshape(m, N_HEADS, HEAD_DIM), p["g_q"])
    kc = _seg(z, "kc").reshape(kvs)
    vc = _seg(z, "vc").reshape(kvs)
    ks = _head_rms(_seg(z, "ks").reshape(kvs), p["g_ks"])
    vs = _seg(z, "vs").reshape(kvs)
    kw = _head_rms(_seg(z, "kw").reshape(kvs), p["g_kw"])
    vw = _seg(z, "vw").reshape(kvs)
    gates = jax.nn.sigmoid(_seg(z, "ga").reshape(m, N_HEADS, 3))
    return q, gates, kc, vc, ks, vs, kw, vw


def _layer_prompt(x, p):
    t = x.shape[0]
    x = _ffn(x, p["n_ffn1"], p["ffn1_gate"], p["ffn1_up"], p["ffn1_down"])
    z = _inproj(x, p["n_mix"], p["w_in_p"])
    q, gates, kc, vc, ks, vs, kw, vw = _nsa_proj(z, p)
    ck = _head_rms(_compress(kc.reshape(1, t, KV_DIM), p["pe_cmp_k"], p["w_cmp_k"])[0]
                   .reshape(-1, N_KV, HEAD_DIM), p["g_kc"])
    cv = _compress(vc.reshape(1, t, KV_DIM), p["pe_cmp_v"], p["w_cmp_v"])[0].reshape(-1, N_KV, HEAD_DIM)
    o_a = _nsa_prompt(z, p["g_q"], gates, ck, cv, ks, vs, kw, vw)
    shift0 = jnp.zeros((8, _Z_COLS), f32)
    s0t = jnp.zeros((RW_HEADS, RW_N, RW_N), f32)
    o_b, st = _rwkv_prompt(z, shift0, s0t, p)
    x = _merge(o_a, o_b, z, x, p["w_pa"], p["w_pb"], p["w_out"])
    x = _ffn(x, p["n_ffn2"], p["ffn2_gate"], p["ffn2_up"], p["ffn2_down"])
    n_buf = min(WINDOW, t)
    states = (kc, vc, ks, vs, kw[-n_buf:], vw[-n_buf:], _rw_cols(z[-1:]), st.transpose(0, 2, 1)[None])
    return x, states


def _layer_sample(x, p, past):
    b = x.shape[0]
    table = past["page_table"]
    page = past["cmp_k"].shape[1]
    t_pos = table.shape[1] * page
    x = _ffn(x, p["n_ffn1"], p["ffn1_gate"], p["ffn1_up"], p["ffn1_down"])
    z = _inproj(x, p["n_mix"], p["w_in_p"])
    q, gates, kc, vc, ks, vs, kw, vw = _nsa_proj(z, p)

    fm = lambda a: a.transpose(0, 2, 3, 1)

    nbp = t_pos // CMP_BLOCK
    ck = _head_rms(_compress_pages(fm(past["cmp_k"]), table, p["pe_cmp_k"], p["w_cmp_k"]), p["g_kc"])
    cv = _compress_pages(fm(past["cmp_v"]), table, p["pe_cmp_v"], p["w_cmp_v"])

    def new_block_rows(k_new, pe):
        first = k_new + pe[0]
        rest = jnp.broadcast_to(pe[1:].reshape(1, 1, -1), (b, N_KV, (CMP_BLOCK - 1) * HEAD_DIM))
        rows = jnp.concatenate([first, rest], axis=-1)
        return jnp.pad(rows, ((0, 0), (0, _GP - N_KV), (0, 0)))

    n_sel = min(N_SEL, nbp + 1)
    n_pick = n_sel - 1
    scale = HEAD_DIM ** -0.5
    pad_g = lambda a: jnp.pad(a, ((0, 0), (0, 0), (0, _GP - GROUP), (0, 0)))
    qh = pad_g((q * scale).reshape(b, N_KV, GROUP, HEAD_DIM))
    gth = pad_g(gates.reshape(b, N_KV, GROUP, 3))
    kd = CMP_BLOCK * HEAD_DIM
    o_c, idx = _dec_cmp(qh, ck, cv, new_block_rows(kc, p["pe_cmp_k"]), new_block_rows(vc, p["pe_cmp_v"]),
                        p["w_cmp_k"].reshape(kd, HEAD_DIM).astype(bf16),
                        p["w_cmp_v"].reshape(kd, HEAD_DIM).astype(bf16), p["g_kc"], t_pos, n_pick)
    o_a = _dec_sel(qh, gth, o_c, idx.reshape(b, N_KV * 128), table, fm(past["slc_k"]), fm(past["slc_v"]),
                   ks, vs, kw, vw, fm(past["win_k"]), fm(past["win_v"]), t_pos, n_pick)
    o_a = o_a[:, :, :GROUP].reshape(b, NSA_DIM).astype(bf16)

    prev = _pad_rw_cols(past["shift"])
    r, k, v, kk, ka, ld, g, bonus = _rw_prep(
        z, prev, p["mu_p"], p["rw_w0"], p["w2p"], p["rw_a0"], p["a2p"], p["g2"],
        p["rw_k_k"], p["rw_k_a"], p["rw_r_k"], False)
    tb = lambda a: a.transpose(1, 0, 2).reshape(b, RW_HEADS, RW_N)
    y, wkv = _wkv_step(past["wkv"], tb(r), tb(k), tb(v), tb(kk), tb(ka), tb(ld))
    y = y.reshape(b, RW_HEADS // 2, 2 * RW_N).transpose(1, 0, 2)
    o_b = _rw_post(y, bonus, g, p["rw_ln_w"], p["rw_ln_b"])

    x = _merge(o_a, o_b, z, x, p["w_pa"], p["w_pb"], p["w_out"])
    x = _ffn(x, p["n_ffn2"], p["ffn2_gate"], p["ffn2_up"], p["ffn2_down"])
    kvs = lambda a: a.reshape(b, 1, N_KV, HEAD_DIM)
    win_k = jnp.concatenate([past["win_k"][:, 1:], kvs(kw)], axis=1)
    win_v = jnp.concatenate([past["win_v"][:, 1:], kvs(vw)], axis=1)
    states = (kvs(kc), kvs(vc), kvs(ks), kvs(vs), win_k, win_v, _rw_cols(z), wkv)
    return x, states


def kernel(x_prompt, x_sample, cache_cmp_k, cache_cmp_v, cache_slc_k, cache_slc_v, cache_win_k, cache_win_v,
           state_shift, state_wkv, page_table,
           n_ffn1, ffn1_gate, ffn1_up, ffn1_down, n_mix, w_in, g_q, g_kc, g_ks, g_kw,
           w_cmp_k, pe_cmp_k, w_cmp_v, pe_cmp_v,
           rw_mu, rw_w0, rw_w2, rw_a0, rw_a2, rw_g2, rw_k_k, rw_k_a, rw_r_k, rw_ln_w, rw_ln_b,
           w_pa, w_pb, w_out, n_ffn2, ffn2_gate, ffn2_up, ffn2_down):
    assert x_prompt.shape[0] == 1 and x_sample.shape[1] == 1 and n_ffn1.shape[0] == 1
    l = 0
    p = _prepare(dict(
        n_ffn1=n_ffn1[l], ffn1_gate=ffn1_gate[l], ffn1_up=ffn1_up[l], ffn1_down=ffn1_down[l],
        n_mix=n_mix[l], w_in=w_in[l], g_q=g_q[l], g_kc=g_kc[l], g_ks=g_ks[l], g_kw=g_kw[l],
        w_cmp_k=w_cmp_k[l], pe_cmp_k=pe_cmp_k[l], w_cmp_v=w_cmp_v[l], pe_cmp_v=pe_cmp_v[l],
        rw_mu=rw_mu[l], rw_w0=rw_w0[l], rw_w2=rw_w2[l], rw_a0=rw_a0[l], rw_a2=rw_a2[l], rw_g2=rw_g2[l],
        rw_k_k=rw_k_k[l], rw_k_a=rw_k_a[l], rw_r_k=rw_r_k[l], rw_ln_w=rw_ln_w[l], rw_ln_b=rw_ln_b[l],
        w_pa=w_pa[l], w_pb=w_pb[l], w_out=w_out[l],
        n_ffn2=n_ffn2[l], ffn2_gate=ffn2_gate[l], ffn2_up=ffn2_up[l], ffn2_down=ffn2_down[l]))
    t = x_prompt.shape[1]
    y_p, sp = _layer_prompt(x_prompt[0], p)
    past = dict(page_table=page_table, cmp_k=cache_cmp_k[l], cmp_v=cache_cmp_v[l], slc_k=cache_slc_k[l],
                slc_v=cache_slc_v[l], win_k=cache_win_k[l], win_v=cache_win_v[l], shift=state_shift[l],
                wkv=state_wkv[l])
    y_s, ss = _layer_sample(x_sample[:, 0], p, past)
    kvp = lambda a: a.reshape(1, 1, -1, N_KV, HEAD_DIM)
    outs_p = (kvp(sp[0]), kvp(sp[1]), kvp(sp[2]), kvp(sp[3]), kvp(sp[4]), kvp(sp[5]), sp[6][None], sp[7][None])
    outs_s = tuple(a[None] for a in ss)
    return (y_p.reshape(1, t, D_MODEL), y_s[:, None, :]) + outs_p + outs_s
```

```python
import functools

import jax
import jax.numpy as jnp
from jax import lax
from jax.experimental import pallas as pl
from jax.experimental.pallas import tpu as pltpu

f32 = jnp.float32
bf16 = jnp.bfloat16

D_MODEL = 2048
N_HEADS = 16
N_KV = 4
GROUP = 4
HEAD_DIM = 64
NSA_DIM = N_HEADS * HEAD_DIM
KV_DIM = N_KV * HEAD_DIM
CMP_BLOCK = 64
N_SEL = 16
WINDOW = 512
Q_BLOCK = 128
FORCED_SCORE = 1e3
RW_HEADS = 16
RW_N = 64
RW_DIM = RW_HEADS * RW_N
W_LORA = 96
A_LORA = 96
G_LORA = 256
D_FF = 5632
NORM_EPS = 1e-6
GN_EPS = 64e-5
NEG_BIG = -1e30

VMEM_LIMIT = 56 * 1024 * 1024

_SEGS = (
    ("r", 2608, 1024, 1024),
    ("k", 3728, 1024, 1024),
    ("v", 4752, 1024, 1024),
    ("q", 0, 1024, 1024),
    ("g_a", 6128, 2048, 2048),
    ("g_b", 8176, 2048, 2048),
    ("kc", 1024, 256, 256),
    ("vc", 1280, 256, 256),
    ("ks", 1536, 256, 256),
    ("vs", 1792, 256, 256),
    ("kw", 2048, 256, 256),
    ("vw", 2304, 256, 256),
    ("gl", 5872, 256, 256),
    ("ga", 2560, 48, 128),
    ("wl", 3632, 96, 128),
    ("al", 5776, 96, 128),
)
_Z_COLS = 10752


def _seg_offsets():
    offs, o = {}, 0
    for name, _, w, pw in _SEGS:
        assert o % pw == 0
        offs[name] = (o, w, pw)
        o += pw
    return offs, o


_OFF, _USED = _seg_offsets()


_IN_COLS = 10224
_RW_START = 2608
_RW_COLS = 3520


def _pad_cols(x):
    parts = []
    for _, s, w, pw in _SEGS:
        seg = x[..., s:s + w]
        if pw != w:
            seg = jnp.pad(seg, [(0, 0)] * (x.ndim - 1) + [(0, pw - w)])
        parts.append(seg)
    parts.append(jnp.zeros(x.shape[:-1] + (_Z_COLS - _USED,), x.dtype))
    return jnp.concatenate(parts, axis=-1)


def _pad_rw_cols(x):
    pad = [(0, 0)] * (x.ndim - 1) + [(_RW_START, _IN_COLS - _RW_START - _RW_COLS)]
    return _pad_cols(jnp.pad(x, pad))


def _seg(z, name):
    o, w, _ = _OFF[name]
    return z[..., o:o + w]


def _rw_cols(z):
    return jnp.concatenate([_seg(z, n) for n in ("r", "wl", "k", "v", "al", "gl")], axis=-1)


def _cparams(sem, vmem=VMEM_LIMIT):
    return pltpu.CompilerParams(dimension_semantics=sem, vmem_limit_bytes=vmem)


def _row_tile(m, pref):
    return pref if m % pref == 0 else m


def _rms(x, g):
    ms = jnp.mean(x * x, axis=-1, keepdims=True)
    return x * lax.rsqrt(ms + NORM_EPS) * g


def _ffn_kernel(x_ref, g_ref, wg_ref, wu_ref, wd_ref, o_ref, h_ref, acc_ref):
    j = pl.program_id(1)

    @pl.when(j == 0)
    def _():
        h_ref[...] = _rms(x_ref[...], g_ref[...]).astype(bf16)
        acc_ref[...] = jnp.zeros_like(acc_ref)

    h = h_ref[...]
    g = jnp.dot(h, wg_ref[...], preferred_element_type=f32)
    u = jnp.dot(h, wu_ref[...], preferred_element_type=f32)
    a = (g * jax.nn.sigmoid(g) * u).astype(bf16)
    acc_ref[...] += jnp.dot(a, wd_ref[...], preferred_element_type=f32)

    @pl.when(j == pl.num_programs(1) - 1)
    def _():
        o_ref[...] = x_ref[...] + 0.5 * acc_ref[...]


def _ffn(x, gain, wg, wu, wd):
    m, d = x.shape
    ff = wg.shape[1]
    bm = _row_tile(m, 512)
    bf = 512
    return pl.pallas_call(
        _ffn_kernel,
        out_shape=jax.ShapeDtypeStruct((m, d), f32),
        grid=(m // bm, ff // bf),
        in_specs=[
            pl.BlockSpec((bm, d), lambda i, j: (i, 0)),
            pl.BlockSpec((1, d), lambda i, j: (0, 0)),
            pl.BlockSpec((d, bf), lambda i, j: (0, j)),
            pl.BlockSpec((d, bf), lambda i, j: (0, j)),
            pl.BlockSpec((bf, d), lambda i, j: (j, 0)),
        ],
        out_specs=pl.BlockSpec((bm, d), lambda i, j: (i, 0)),
        scratch_shapes=[pltpu.VMEM((bm, d), bf16), pltpu.VMEM((bm, d), f32)],
        compiler_params=_cparams(("parallel", "arbitrary")),
        name="ffn",
    )(x, gain.reshape(1, d), wg, wu, wd)


def _inproj_kernel(x_ref, g_ref, w_ref, o_ref, h_ref):
    @pl.when(pl.program_id(1) == 0)
    def _():
        h_ref[...] = _rms(x_ref[...], g_ref[...]).astype(bf16)

    o_ref[...] = jnp.dot(h_ref[...], w_ref[...], preferred_element_type=f32)


def _inproj(x, gain, w):
    m, d = x.shape
    n = w.shape[1]
    bm = _row_tile(m, 1024)
    bn = 1536
    assert n % bn == 0
    return pl.pallas_call(
        _inproj_kernel,
        out_shape=jax.ShapeDtypeStruct((m, n), f32),
        grid=(m // bm, n // bn),
        in_specs=[
            pl.BlockSpec((bm, d), lambda i, j: (i, 0)),
            pl.BlockSpec((1, d), lambda i, j: (0, 0)),
            pl.BlockSpec((d, bn), lambda i, j: (0, j)),
        ],
        out_specs=pl.BlockSpec((bm, bn), lambda i, j: (i, j)),
        scratch_shapes=[pltpu.VMEM((bm, d), bf16)],
        compiler_params=_cparams(("parallel", "arbitrary")),
        name="inproj",
    )(x, gain.reshape(1, d), w)


def _merge_kernel(oa_ref, ob_ref, ga_ref, gb_ref, x_ref, wpa_ref, wpb_ref, wo_ref, o_ref):
    pa = jnp.dot(oa_ref[...], wpa_ref[...], preferred_element_type=f32)
    pb = jnp.dot(ob_ref[...], wpb_ref[...], preferred_element_type=f32)
    mix = jax.nn.sigmoid(ga_ref[...]) * pa + jax.nn.sigmoid(gb_ref[...]) * pb
    o_ref[...] = x_ref[...] + jnp.dot(mix.astype(bf16), wo_ref[...], preferred_element_type=f32)


def _merge(oa, ob, z, x, wpa, wpb, wo):
    m, d = x.shape
    bm = _row_tile(m, 256)
    ca = _OFF["g_a"][0] // d
    cb = _OFF["g_b"][0] // d
    return pl.pallas_call(
        _merge_kernel,
        out_shape=jax.ShapeDtypeStruct((m, d), f32),
        grid=(m // bm,),
        in_specs=[
            pl.BlockSpec((bm, NSA_DIM), lambda i: (i, 0)),
            pl.BlockSpec((bm, RW_DIM), lambda i: (i, 0)),
            pl.BlockSpec((bm, d), lambda i: (i, ca)),
            pl.BlockSpec((bm, d), lambda i: (i, cb)),
            pl.BlockSpec((bm, d), lambda i: (i, 0)),
            pl.BlockSpec((NSA_DIM, d), lambda i: (0, 0)),
            pl.BlockSpec((RW_DIM, d), lambda i: (0, 0)),
            pl.BlockSpec((d, d), lambda i: (0, 0)),
        ],
        out_specs=pl.BlockSpec((bm, d), lambda i: (i, 0)),
        compiler_params=_cparams(("parallel",)),
        name="merge",
    )(oa, ob, z, z, x, wpa, wpb, wo)


def _compress_kernel(x_ref, pe_ref, w_ref, o_ref):
    nb = o_ref.shape[1]
    acc = [jnp.zeros((nb, 128), f32), jnp.zeros((nb, 128), f32)]
    for j in range(CMP_BLOCK):
        pe_j = pe_ref[pl.ds(j, 1), :]
        w_j = w_ref[j]
        for h in range(2):
            xj = x_ref[0, pl.ds(2 * j + h, nb, stride=2 * CMP_BLOCK), :] + pe_j
            acc[h] = acc[h] + jnp.dot(xj.astype(bf16), w_j, preferred_element_type=f32)
    o_ref[0] = jnp.concatenate(acc, axis=-1)


def _blockdiag2(m):
    z = jnp.zeros_like(m)
    return jnp.concatenate([jnp.concatenate([m, z], axis=2), jnp.concatenate([z, m], axis=2)], axis=1)


def _compress(x, pe, w):
    b, l, _ = x.shape
    nb = l // CMP_BLOCK
    pe_t = jnp.tile(pe, (1, 2))
    wbd = _blockdiag2(w).astype(bf16)
    return pl.pallas_call(
        _compress_kernel,
        out_shape=jax.ShapeDtypeStruct((b, nb, KV_DIM), f32),
        grid=(b,),
        in_specs=[
            pl.BlockSpec((1, 2 * l, 128), lambda i: (i, 0, 0)),
            pl.BlockSpec((CMP_BLOCK, 128), lambda i: (0, 0)),
            pl.BlockSpec((CMP_BLOCK, 128, 128), lambda i: (0, 0, 0)),
        ],
        out_specs=pl.BlockSpec((1, nb, KV_DIM), lambda i: (i, 0, 0)),
        compiler_params=_cparams(("parallel",)),
        name="compress",
    )(x.reshape(b, 2 * l, 128), pe_t, wbd)


_TK = 512
_ROWS = GROUP * Q_BLOCK
_POS_FEATS = 16


def _softmax_cols(s):
    m = jnp.max(s, axis=0, keepdims=True)
    m = jnp.where(jnp.isfinite(m), m, 0.0)
    p = jnp.exp(s - m)
    return p / jnp.maximum(jnp.sum(p, axis=0, keepdims=True), 1e-30)


def _topk_mask_t(vt, n_pick, sel):
    nb = vt.shape[0]
    bi = lax.broadcasted_iota(jnp.int32, vt.shape, 0)
    for _ in range(n_pick):
        mx = jnp.max(vt, axis=0, keepdims=True)
        idx = jnp.min(jnp.where(vt == mx, bi, nb), axis=0, keepdims=True)
        hit = bi == idx
        sel = jnp.where(hit, 1.0, sel)
        vt = jnp.where(hit, -jnp.inf, vt)
    return sel


def _nsa_kernel(q_ref, gq_ref, gt_ref, ck_ref, cvt_ref, ke_ref, vst_ref, kw_ref, vwt_ref,
                o_ref, a0w_ref, act_ref):
    kv = pl.program_id(0)
    i = pl.program_id(1)
    nb = ck_ref.shape[1]
    n_sel = min(N_SEL, nb)
    wk = WINDOW + Q_BLOCK

    lane = lax.broadcasted_iota(jnp.int32, (1, _ROWS), 1)
    grp = lane // Q_BLOCK
    tl = lane % Q_BLOCK
    slope = jnp.exp2(-0.5 * (kv * GROUP + grp + 1).astype(f32))

    @pl.when(i == 0)
    def _():
        kroww = lax.broadcasted_iota(jnp.int32, (wk, 1), 0)
        dist = tl + WINDOW - kroww
        a0w_ref[...] = jnp.where((dist >= 0) & (dist < WINDOW), slope * dist.astype(f32), -NEG_BIG)

    q_raw = q_ref[...]
    scale = HEAD_DIM ** -0.5
    parts = []
    for g in range(GROUP):
        qg = q_raw[:, g * HEAD_DIM:(g + 1) * HEAD_DIM]
        ms = jnp.sum(qg * qg, axis=-1, keepdims=True) / HEAD_DIM
        parts.append((qg * lax.rsqrt(ms + NORM_EPS) * gq_ref[...] * scale).T)
    qt = jnp.concatenate(parts, axis=1).astype(bf16)
    t0 = i * Q_BLOCK
    tok = t0 + tl

    blk = lax.broadcasted_iota(jnp.int32, (nb, 1), 0)
    blk_mid = (blk * CMP_BLOCK).astype(f32) + (CMP_BLOCK - 1) / 2
    s = jnp.dot(ck_ref[0], qt, preferred_element_type=f32)
    s = s - slope * (tok.astype(f32) - blk_mid)
    s = jnp.where(blk * CMP_BLOCK + (CMP_BLOCK - 1) <= tok, s, -jnp.inf)
    p_c = _softmax_cols(s)
    o_c = jnp.dot(cvt_ref[0], p_c.astype(bf16), preferred_element_type=f32)

    w0 = pl.multiple_of(t0, Q_BLOCK)
    s = jnp.dot(kw_ref[0, pl.ds(w0, wk), :], qt, preferred_element_type=f32) - a0w_ref[...]
    kroww = lax.broadcasted_iota(jnp.int32, (wk, 1), 0)
    s = jnp.where(t0 - WINDOW + kroww >= 0, s, -jnp.inf)
    p_w = jnp.exp(s - jnp.max(s, axis=0, keepdims=True))
    o_w = jnp.dot(vwt_ref[0, :, pl.ds(w0, wk)], p_w.astype(bf16), preferred_element_type=f32)
    o_w = o_w / jnp.maximum(jnp.sum(p_w, axis=0, keepdims=True), 1e-30)

    imp = p_c[:, 0:Q_BLOCK]
    for g in range(1, GROUP):
        imp = imp + p_c[:, g * Q_BLOCK:(g + 1) * Q_BLOCK]
    tq = t0 + lax.broadcasted_iota(jnp.int32, (1, Q_BLOCK), 1)
    cur = tq // CMP_BLOCK
    forced = (blk == 0) | (blk == cur) | (blk == cur - 1)
    imp = jnp.where(forced, -jnp.inf, jnp.where(blk * CMP_BLOCK <= tq, imp, -1.0))
    sel_t = _topk_mask_t(imp, max(n_sel - 3, 0), forced.astype(f32))
    bias_t = jnp.where(sel_t > 0.0, 0.0, NEG_BIG).astype(bf16)
    jd = t0 // _TK
    bpt = _TK // CMP_BLOCK
    n_act = jnp.int32(0)
    for j in range(nb // bpt):
        used = (jnp.max(sel_t[j * bpt:(j + 1) * bpt, :]) > 0.0) & (j < jd)
        act_ref[n_act] = j
        n_act = n_act + used.astype(jnp.int32)
    act_ref[n_act] = jd
    s1 = slope.astype(bf16).astype(f32)
    s2 = (slope - s1).astype(bf16).astype(f32)
    s3 = ((slope - s1) - s2).astype(bf16).astype(f32)
    fr = lax.broadcasted_iota(jnp.int32, (_POS_FEATS, 1), 0)
    pieces = jnp.where(fr < 2, s1, jnp.where(fr < 4, s2, jnp.where(fr < 6, s3, 0.0))).astype(bf16)
    rhs = jnp.concatenate([qt, jnp.concatenate([bias_t] * GROUP, axis=1), pieces], axis=0)

    def scores(j):
        c0 = pl.multiple_of(j * _TK, _TK)
        return jnp.dot(ke_ref[0, pl.ds(c0, _TK), :], rhs, preferred_element_type=f32)

    def update(j, s, carry, causal):
        m, l, acc = carry
        c0 = pl.multiple_of(j * _TK, _TK)
        if causal:
            krow = lax.broadcasted_iota(jnp.int32, (_TK, 1), 0)
            s = jnp.where(tok - (c0 + krow) >= 0, s, -jnp.inf)
        off = slope * (tok - c0).astype(f32)
        m_new = jnp.maximum(m, jnp.max(s, axis=0, keepdims=True) - off)
        p = jnp.exp(s - (m_new + off))
        alpha = jnp.exp(m - m_new)
        l = alpha * l + jnp.sum(p, axis=0, keepdims=True)
        acc = alpha * acc + jnp.dot(vst_ref[0, :, pl.ds(c0, _TK)], p.astype(bf16),
                                    preferred_element_type=f32)
        return m_new, l, acc

    init = (jnp.full((1, _ROWS), -jnp.inf, f32), jnp.zeros((1, _ROWS), f32),
            jnp.zeros((HEAD_DIM, _ROWS), f32))

    def body(n, carry):
        s_next = scores(act_ref[n + 1])
        return update(act_ref[n], carry[3], carry[:3], False) + (s_next,)

    carry = lax.fori_loop(0, n_act, body, init + (scores(act_ref[0]),))
    _, l, acc = update(jd, carry[3], carry[:3], True)
    o_s = acc / jnp.maximum(l, 1e-30)

    gt = gt_ref[0, 0]
    out = gt[0:1] * o_c + gt[1:2] * o_s + gt[2:3] * o_w
    o_ref[...] = jnp.concatenate([out[:, g * Q_BLOCK:(g + 1) * Q_BLOCK].T for g in range(GROUP)],
                                 axis=1).astype(o_ref.dtype)


def _nsa_prompt(z, g_q, gates, ck, cv, ks, vs, kw, vw):
    t = z.shape[0]
    nqb = t // Q_BLOCK
    nb = ck.shape[0]
    qcol = _OFF["q"][0] // (GROUP * HEAD_DIM)
    gt = gates.reshape(nqb, Q_BLOCK, N_KV, GROUP, 3).transpose(2, 0, 4, 3, 1).reshape(N_KV, nqb, 3, _ROWS)
    ckh = ck.astype(bf16).transpose(1, 0, 2)
    cvt = cv.astype(bf16).transpose(1, 2, 0)
    pos = jnp.arange(t)
    onehot = (pos[:, None] // CMP_BLOCK == jnp.arange(nb)[None, :]).astype(bf16)
    off_hi = (pos % _TK) // 16 * 16
    off_lo = pos % 16
    feats = jnp.stack([off_hi, off_lo] * 3 + [jnp.zeros_like(pos)] * (_POS_FEATS - 6), axis=1).astype(bf16)
    ke = jnp.concatenate([ks.astype(bf16).transpose(1, 0, 2),
                          jnp.broadcast_to(jnp.concatenate([onehot, feats], axis=1)[None],
                                           (N_KV, t, nb + _POS_FEATS))], axis=-1)
    vst = vs.astype(bf16).transpose(1, 2, 0)
    kwh = jnp.pad(kw.astype(bf16).transpose(1, 0, 2), ((0, 0), (WINDOW, 0), (0, 0)))
    vwt = jnp.pad(vw.astype(bf16).transpose(1, 2, 0), ((0, 0), (0, 0), (WINDOW, 0)))
    kd = HEAD_DIM + nb + _POS_FEATS
    return pl.pallas_call(
        _nsa_kernel,
        out_shape=jax.ShapeDtypeStruct((t, NSA_DIM), bf16),
        grid=(N_KV, nqb),
        in_specs=[
            pl.BlockSpec((Q_BLOCK, GROUP * HEAD_DIM), lambda k, i: (i, qcol + k)),
            pl.BlockSpec((1, HEAD_DIM), lambda k, i: (0, 0)),
            pl.BlockSpec((1, 1, 3, _ROWS), lambda k, i: (k, i, 0, 0)),
            pl.BlockSpec((1, nb, HEAD_DIM), lambda k, i: (k, 0, 0)),
            pl.BlockSpec((1, HEAD_DIM, nb), lambda k, i: (k, 0, 0)),
            pl.BlockSpec((1, t, kd), lambda k, i: (k, 0, 0)),
            pl.BlockSpec((1, HEAD_DIM, t), lambda k, i: (k, 0, 0)),
            pl.BlockSpec((1, t + WINDOW, HEAD_DIM), lambda k, i: (k, 0, 0)),
            pl.BlockSpec((1, HEAD_DIM, t + WINDOW), lambda k, i: (k, 0, 0)),
        ],
        out_specs=pl.BlockSpec((Q_BLOCK, GROUP * HEAD_DIM), lambda k, i: (i, k)),
        scratch_shapes=[pltpu.VMEM((WINDOW + Q_BLOCK, _ROWS), f32),
                        pltpu.SMEM((nb // (_TK // CMP_BLOCK) + 1,), jnp.int32)],
        compiler_params=_cparams(("arbitrary", "arbitrary")),
        name="nsa_prompt",
    )(z, g_q.reshape(1, HEAD_DIM), gt, ckh, cvt, ke, vst, kwh, vwt)


def _pair_sum(x, low):
    s_lo = jnp.sum(jnp.where(low, x, 0.0), axis=-1, keepdims=True)
    s_hi = jnp.sum(jnp.where(low, 0.0, x), axis=-1, keepdims=True)
    return jnp.where(low, s_lo, s_hi)


def _rw_prep_kernel(shift_rows, *refs):
    cur, refs = refs[:6], refs[6:]
    prv, refs = refs[:6], refs[6:]
    if shift_rows:
        st0, refs = refs[:6], refs[6:]
    mus, refs = refs[:6], refs[6:]
    (w0_ref, w2_ref, a0_ref, a2_ref, g2_ref, kk_ref, ka_ref, rk_ref,
     ro_ref, ko_ref, vo_ref, kko_ref, kao_ref, ldo_ref, go_ref, bo_ref) = refs
    first = pl.program_id(0) == 0

    def shifted(n):
        x = cur[n][...]
        if shift_rows:
            prev_row = jnp.where(first, st0[n][pl.ds(7, 1), :], prv[n][pl.ds(7, 1), :])
            rolled = pltpu.roll(x, 1, axis=0)
            rid = lax.broadcasted_iota(jnp.int32, x.shape, 0)
            xp = jnp.where(rid == 0, prev_row, rolled)
        else:
            xp = prv[n][...]
        return x + (xp - x) * mus[n][...]

    r, k, v, gl, wl, al = (shifted(n) for n in range(6))

    y = -(w0_ref[...] + jnp.dot(jnp.tanh(wl).astype(bf16), w2_ref[...], preferred_element_type=f32))
    softplus = jnp.maximum(y, 0.0) + jnp.log1p(jnp.exp(-jnp.abs(y)))
    w_log = -softplus - 0.5
    ld = -jnp.exp(w_log)
    a = jax.nn.sigmoid(a0_ref[...] + jnp.dot(al.astype(bf16), a2_ref[...], preferred_element_type=f32))
    g = jnp.dot(jax.nn.sigmoid(gl).astype(bf16), g2_ref[...], preferred_element_type=f32)
    kk = k * kk_ref[...]
    k2 = k * (1.0 + (a - 1.0) * ka_ref[...])
    rkr = r * k2 * rk_ref[...]
    low = lax.broadcasted_iota(jnp.int32, (1, 2 * RW_N), 1) < RW_N
    for hp in range(RW_HEADS // 2):
        sl = slice(hp * 2 * RW_N, (hp + 1) * 2 * RW_N)
        kkp = kk[:, sl]
        kkp = kkp / jnp.maximum(jnp.sqrt(_pair_sum(kkp * kkp, low)), 1e-12)
        ro_ref[hp] = r[:, sl]
        ko_ref[hp] = k2[:, sl]
        vo_ref[hp] = v[:, sl]
        kko_ref[hp] = kkp
        kao_ref[hp] = kkp * a[:, sl]
        ldo_ref[hp] = ld[:, sl]
        go_ref[hp] = g[:, sl]
        bo_ref[hp] = _pair_sum(rkr[:, sl], low) * v[:, sl]


def _rw_prep(z, prev, mu_p, w0, w2p, a0, a2p, g2, k_k, k_a, r_k, shift_rows):
    m = z.shape[0]
    tm = _row_tile(m, 256)
    names = ("r", "k", "v", "gl", "wl", "al")

    def col_spec(name, rows, imap):
        o, _, pw = _OFF[name]
        return pl.BlockSpec((rows, pw), functools.partial(imap, o // pw))

    cur = [col_spec(n, tm, lambda c, i: (i, c)) for n in names]
    if shift_rows:
        blk8 = tm // 8
        prv = [col_spec(n, 8, lambda c, i: (jnp.maximum(i * blk8 - 1, 0), c)) for n in names]
        prv += [col_spec(n, 8, lambda c, i: (0, c)) for n in names]
        prev_args = [z] * 6 + [prev] * 6
    else:
        prv = [col_spec(n, tm, lambda c, i: (i, c)) for n in names]
        prev_args = [prev] * 6
    mus = [col_spec(n, 1, lambda c, i: (0, c)) for n in names]
    vec = pl.BlockSpec((1, RW_DIM), lambda i: (0, 0))
    out_spec = pl.BlockSpec((RW_HEADS // 2, tm, 2 * RW_N), lambda i: (0, i, 0))
    outs = pl.pallas_call(
        functools.partial(_rw_prep_kernel, shift_rows),
        out_shape=[jax.ShapeDtypeStruct((RW_HEADS // 2, m, 2 * RW_N), f32)] * 8,
        grid=(m // tm,),
        in_specs=cur + prv + mus + [
            vec,
            pl.BlockSpec((128, RW_DIM), lambda i: (0, 0)),
            vec,
            pl.BlockSpec((128, RW_DIM), lambda i: (0, 0)),
            pl.BlockSpec((G_LORA, RW_DIM), lambda i: (0, 0)),
            vec, vec, vec,
        ],
        out_specs=[out_spec] * 8,
        compiler_params=_cparams(("parallel",)),
        name="rw_prep",
    )(*([z] * 6), *prev_args, *([mu_p] * 6),
      w0.reshape(1, RW_DIM), w2p, a0.reshape(1, RW_DIM), a2p, g2,
      k_k.reshape(1, RW_DIM), k_a.reshape(1, RW_DIM), r_k.reshape(1, RW_DIM))
    return outs


_CH = 64
_NN = (((1,), (0,)), ((), ()))
_NT = (((1,), (1,)), ((), ()))
_TN = (((0,), (0,)), ((), ()))


def _split2(x):
    hi = x.astype(bf16)
    lo = (x - hi.astype(f32)).astype(bf16)
    return hi, lo


def _dot3(a, b, dims=_NN):
    ah, al = _split2(a)
    bh, bl = _split2(b)
    d = functools.partial(lax.dot_general, dimension_numbers=dims, preferred_element_type=f32)
    return d(ah, bh) + (d(ah, bl) + d(al, bh))


def _split3(x):
    x1 = x.astype(bf16)
    r1 = x - x1.astype(f32)
    x2 = r1.astype(bf16)
    return x1, x2, (r1 - x2.astype(f32)).astype(bf16)


def _dot_exact_rhs(a01, b):
    b1, b2, b3 = _split3(b)
    d = functools.partial(jnp.dot, preferred_element_type=f32)
    return d(a01, b1) + (d(a01, b2) + d(a01, b3))


def _dot_exact_lhs_tn(a, b01):
    a1, a2, a3 = _split3(a)
    d = functools.partial(lax.dot_general, dimension_numbers=_TN, preferred_element_type=f32)
    return d(a1, b01) + (d(a2, b01) + d(a3, b01))


def _wkv_chunk_kernel(r_ref, k_ref, v_ref, kk_ref, ka_ref, ld_ref, b_ref, g_ref, lw_ref, lb_ref, s0_ref,
                      o_ref, sT_ref, st_ref):
    c = pl.program_id(0)

    @pl.when(c == 0)
    def _():
        st_ref[...] = s0_ref[...]

    ti = lax.broadcasted_iota(jnp.int32, (_CH, _CH), 0)
    si = lax.broadcasted_iota(jnp.int32, (_CH, _CH), 1)
    incl = ti >= si
    strict = ti > si
    l_incl = incl.astype(bf16)
    eye = (ti == si).astype(f32)
    ones = jnp.ones((_CH, RW_N), bf16)
    same_blk = [(ti // b) == (si // b) for b in (8, 16, 32, 64)]

    def each(f, *lists):
        return [f(*a) for a in zip(*lists)]

    def cat0(a, b):
        return jnp.concatenate([a, b], axis=0)

    def cat1(a, b):
        return jnp.concatenate([a, b], axis=1)

    def heads_step(heads):
        def load(ref):
            return [ref[hp, :, sub * RW_N:(sub + 1) * RW_N] for hp, sub in heads]

        r, k, v, kk, ka, ld = (load(ref) for ref in (r_ref, k_ref, v_ref, kk_ref, ka_ref, ld_ref))
        lp = each(lambda x: _dot_exact_rhs(l_incl, x), ld)
        lp_end = each(lambda x: _dot_exact_lhs_tn(x, ones), ld)
        e_neg = each(lambda x: jnp.exp(-x), lp)
        at = each(lambda a, x, y: -a * jnp.exp(x - y), kk, lp, ld)
        bt = each(jnp.multiply, ka, e_neg)
        kt = each(jnp.multiply, k, e_neg)
        rt = each(lambda a, x: a * jnp.exp(x), r, lp)
        e_end = each(lambda x: jnp.exp(x[_CH - 1:_CH, :] - x), lp)
        bh = each(jnp.multiply, ka, e_end)
        kh = each(jnp.multiply, k, e_end)
        sc = each(lambda a, b: _dot3(a, b, _NT), each(cat0, at, rt), each(cat0, bt, kt))
        a_b = each(lambda x: jnp.where(strict, x[:_CH, :_CH], 0.0), sc)
        a_k = each(lambda x: jnp.where(strict, x[:_CH, _CH:], 0.0), sc)
        g_b = each(lambda x: jnp.where(incl, x[_CH:, :_CH], 0.0), sc)
        g_k = each(lambda x: jnp.where(incl, x[_CH:, _CH:], 0.0), sc)
        pw = each(lambda x: jnp.where(same_blk[0], x, 0.0), a_b)
        tm = each(lambda x: eye + x, pw)
        for _ in range(2):
            pw = each(_dot3, pw, pw)
            tm = each(lambda t_, p_: t_ + _dot3(t_, p_), tm, pw)
        for lvl in range(1, len(same_blk)):
            off = each(lambda x: jnp.where(same_blk[lvl] & ~same_blk[lvl - 1], x, 0.0), a_b)
            tm = each(lambda t_, c_: t_ + _dot3(_dot3(t_, c_), t_), tm, off)
        akv = each(_dot3, a_k, v)
        tx = each(_dot3, tm, each(cat1, at, akv))
        st = [st_ref[2 * hp + sub] for hp, sub in heads]
        ws = each(_dot3, each(lambda x, y: cat0(x[:, :RW_N], y), tx, rt), st)
        u = each(lambda a, x: a[:_CH] + x[:, RW_N:], ws, tx)
        uv = each(cat0, u, v)
        y = each(lambda a, gb, gk, x: a[_CH:] + _dot3(cat1(gb, gk), x), ws, g_b, g_k, uv)
        st_new = each(lambda e, s_, b_, k_, x: jnp.exp(e) * s_ + _dot3(cat0(b_, k_), x, _TN),
                      lp_end, st, bh, kh, uv)
        for (hp, sub), s_ in zip(heads, st_new):
            st_ref[2 * hp + sub] = s_
        return y

    y = heads_step([(hp, sub) for hp in range(RW_HEADS // 2) for sub in range(2)])

    def normed(x):
        mu = jnp.mean(x, axis=-1, keepdims=True)
        var = jnp.mean(jnp.square(x - mu), axis=-1, keepdims=True)
        return (x - mu) * lax.rsqrt(var + GN_EPS)

    for hp in range(RW_HEADS // 2):
        yn = cat1(normed(y[2 * hp]), normed(y[2 * hp + 1])) * lw_ref[hp] + lb_ref[hp]
        o_ref[:, hp * 2 * RW_N:(hp + 1) * 2 * RW_N] = ((yn + b_ref[hp]) * g_ref[hp]).astype(o_ref.dtype)

    @pl.when(c == pl.num_programs(0) - 1)
    def _():
        sT_ref[...] = st_ref[...]


def _wkv_chunks(r, k, v, kk, ka, ld, bonus, g, ln_w, ln_b, s0t):
    t = r.shape[1]
    spec = pl.BlockSpec((RW_HEADS // 2, _CH, 2 * RW_N), lambda c: (0, c, 0))
    pspec = pl.BlockSpec((RW_HEADS // 2, 1, 2 * RW_N), lambda c: (0, 0, 0))
    sspec = pl.BlockSpec((RW_HEADS, RW_N, RW_N), lambda c: (0, 0, 0))
    return pl.pallas_call(
        _wkv_chunk_kernel,
        out_shape=[jax.ShapeDtypeStruct((t, RW_DIM), bf16),
                   jax.ShapeDtypeStruct((RW_HEADS, RW_N, RW_N), f32)],
        grid=(t // _CH,),
        in_specs=[spec] * 8 + [pspec, pspec, sspec],
        out_specs=[pl.BlockSpec((_CH, RW_DIM), lambda c: (c, 0)), sspec],
        scratch_shapes=[pltpu.VMEM((RW_HEADS, RW_N, RW_N), f32)],
        compiler_params=_cparams(("arbitrary",)),
        name="wkv_chunks",
    )(r, k, v, kk, ka, ld, bonus, g, ln_w.reshape(RW_HEADS // 2, 1, 2 * RW_N),
      ln_b.reshape(RW_HEADS // 2, 1, 2 * RW_N), s0t)


def _rw_post_kernel(y_ref, b_ref, g_ref, lw_ref, lb_ref, o_ref):
    low = lax.broadcasted_iota(jnp.int32, (1, 2 * RW_N), 1) < RW_N
    for hp in range(RW_HEADS // 2):
        y = y_ref[hp]
        mu = _pair_sum(y, low) / RW_N
        var = _pair_sum(jnp.square(y - mu), low) / RW_N
        yn = (y - mu) * lax.rsqrt(var + GN_EPS) * lw_ref[hp] + lb_ref[hp]
        o_ref[:, hp * 2 * RW_N:(hp + 1) * 2 * RW_N] = ((yn + b_ref[hp]) * g_ref[hp]).astype(o_ref.dtype)


def _rw_post(y, bonus, g, ln_w, ln_b):
    m = y.shape[1]
    tm = _row_tile(m, 512)
    spec = pl.BlockSpec((RW_HEADS // 2, tm, 2 * RW_N), lambda i: (0, i, 0))
    pspec = pl.BlockSpec((RW_HEADS // 2, 1, 2 * RW_N), lambda i: (0, 0, 0))
    return pl.pallas_call(
        _rw_post_kernel,
        out_shape=jax.ShapeDtypeStruct((m, RW_DIM), bf16),
        grid=(m // tm,),
        in_specs=[spec, spec, spec, pspec, pspec],
        out_specs=pl.BlockSpec((tm, RW_DIM), lambda i: (i, 0)),
        compiler_params=_cparams(("parallel",)),
        name="rw_post",
    )(y, bonus, g, ln_w.reshape(RW_HEADS // 2, 1, 2 * RW_N), ln_b.reshape(RW_HEADS // 2, 1, 2 * RW_N))


def _lora_pad(w):
    return jnp.pad(w, ((0, 128 - w.shape[0]), (0, 0))).astype(bf16)


def _rwkv_prompt(z, shift0_p, s0t, p):
    r, k, v, kk, ka, ld, g, bonus = _rw_prep(
        z, shift0_p, p["mu_p"], p["rw_w0"], p["w2p"], p["rw_a0"], p["a2p"], p["g2"],
        p["rw_k_k"], p["rw_k_a"], p["rw_r_k"], True)
    return _wkv_chunks(r, k, v, kk, ka, ld, bonus, g, p["rw_ln_w"], p["rw_ln_b"], s0t)


def _wkv_step_kernel(s_ref, r_ref, k_ref, vc_ref, kk_ref, ka_ref, ld_ref, y_ref, so_ref):
    for i in range(s_ref.shape[0]):
        s = s_ref[i]
        kk = kk_ref[i][:, None, :]
        sa = jnp.sum(s * (-kk), axis=-1, keepdims=True)
        s = (s * jnp.exp(ld_ref[i])[:, None, :] + sa * ka_ref[i][:, None, :]
             + vc_ref[i] * k_ref[i][:, None, :])
        so_ref[i] = s
        y_ref[i] = jnp.sum(s * r_ref[i][:, None, :], axis=-1, keepdims=True)


def _wkv_step(s0, r, k, v, kk, ka, ld):
    b = s0.shape[0]
    sb = 4 if b % 4 == 0 else 1
    sspec = pl.BlockSpec((sb, RW_HEADS, RW_N, RW_N), lambda i: (i, 0, 0, 0))
    vspec = pl.BlockSpec((sb, RW_HEADS, RW_N), lambda i: (i, 0, 0))
    cspec = pl.BlockSpec((sb, RW_HEADS, RW_N, 1), lambda i: (i, 0, 0, 0))
    y, s1 = pl.pallas_call(
        _wkv_step_kernel,
        out_shape=[jax.ShapeDtypeStruct((b, RW_HEADS, RW_N, 1), f32),
                   jax.ShapeDtypeStruct(s0.shape, f32)],
        grid=(b // sb,),
        in_specs=[sspec, vspec, vspec, cspec, vspec, vspec, vspec],
        out_specs=[cspec, sspec],
        compiler_params=_cparams(("parallel",)),
        name="wkv_step",
    )(s0, r, k, v[..., None], kk, ka, ld)
    return y[..., 0], s1


_GP = 8


def _compress_pages_kernel(npg, pt_ref, pool_ref, pe_ref, w_ref, o_ref, buf0, buf1, sem):
    g = pl.program_id(0)
    n_steps = pl.num_programs(0)

    def page_copies(seq, buf, slot):
        out = []
        for pg in range(npg):
            page_id = pt_ref[seq, pg]
            for kv in range(N_KV):
                out.append(pltpu.make_async_copy(pool_ref.at[page_id, kv], buf.at[:, pg * N_KV + kv, :],
                                                 sem.at[slot]))
        return out

    def start(seq, buf, slot):
        for cp in page_copies(seq, buf, slot):
            cp.start()

    def wait(seq, buf, slot):
        for cp in page_copies(seq, buf, slot):
            cp.wait()

    def compress(buf, out_slot):
        acc = jnp.zeros((npg * N_KV, 2 * CMP_BLOCK), f32)
        for d2 in range(HEAD_DIM // 2):
            x = jnp.concatenate([buf[2 * d2 + e] + pe_ref[pl.ds(2 * d2 + e, 1), :] for e in range(2)], axis=1)
            acc = acc + jnp.dot(x.astype(bf16), w_ref[d2], preferred_element_type=f32)
        o_ref[out_slot] = acc

    @pl.when(g == 0)
    def _():
        start(0, buf0, 0)

    start(2 * g + 1, buf1, 1)
    wait(2 * g, buf0, 0)
    compress(buf0, 0)

    @pl.when(g + 1 < n_steps)
    def _():
        start(2 * g + 2, buf0, 0)

    wait(2 * g + 1, buf1, 1)
    compress(buf1, 1)


def _compress_pages(pool_t, table, pe, w):
    b, npg = table.shape
    page = pool_t.shape[-1]
    assert b % 2 == 0 and page == 2 * CMP_BLOCK
    pe_t = jnp.tile(pe.T, (1, 2))
    wd = _blockdiag2(w.transpose(1, 0, 2)).reshape(HEAD_DIM // 2, 2 * page, page).astype(bf16)
    out = pl.pallas_call(
        functools.partial(_compress_pages_kernel, npg),
        out_shape=jax.ShapeDtypeStruct((b, npg * N_KV, page), f32),
        grid_spec=pltpu.PrefetchScalarGridSpec(
            num_scalar_prefetch=1,
            grid=(b // 2,),
            in_specs=[
                pl.BlockSpec(memory_space=pl.ANY),
                pl.BlockSpec((HEAD_DIM, page), lambda i, pt: (0, 0)),
                pl.BlockSpec((HEAD_DIM // 2, 2 * page, page), lambda i, pt: (0, 0, 0)),
            ],
            out_specs=pl.BlockSpec((2, npg * N_KV, page), lambda i, pt: (i, 0, 0)),
            scratch_shapes=[pltpu.VMEM((HEAD_DIM, npg * N_KV, page), f32),
                            pltpu.VMEM((HEAD_DIM, npg * N_KV, page), f32),
                            pltpu.SemaphoreType.DMA((2,))]),
        compiler_params=_cparams(("arbitrary",)),
        name="compress_pages",
    )(table, pool_t, pe_t, wd)
    out = out.reshape(b, npg, N_KV, 2, HEAD_DIM).transpose(0, 1, 3, 2, 4)
    return out.reshape(b, 2 * npg, N_KV, HEAD_DIM)


def _dec_cmp_kernel(n_pick, t_pos, q_ref, ckt_ref, cv_ref, xk_ref, xv_ref, wk_ref, wv_ref, gkc_ref,
                    oc_ref, idx_ref):
    sb = q_ref.shape[0]
    nb = ckt_ref.shape[-1]
    kd = xk_ref.shape[-1]
    blk = lax.broadcasted_iota(jnp.int32, (1, nb), 1)
    blk_mid = (blk * CMP_BLOCK).astype(f32) + (CMP_BLOCK - 1) / 2
    ck_new = _rms(jnp.dot(xk_ref[...].reshape(sb * _GP, kd).astype(bf16), wk_ref[...],
                          preferred_element_type=f32), gkc_ref[...])
    cv_new = jnp.dot(xv_ref[...].reshape(sb * _GP, kd).astype(bf16), wv_ref[...],
                     preferred_element_type=f32)
    new_mid = float(nb * CMP_BLOCK) + (CMP_BLOCK - 1) / 2
    new_ok = nb * CMP_BLOCK + (CMP_BLOCK - 1) <= t_pos
    lane = lax.broadcasted_iota(jnp.int32, (1, 128), 1)
    imps = []
    for r in range(sb):
        for kv in range(N_KV):
            row = r * _GP + kv
            q = q_ref[r, kv]
            g1 = lax.broadcasted_iota(jnp.int32, (_GP, 1), 0) + (kv * GROUP + 1)
            slope = jnp.exp2(-0.5 * g1.astype(f32))
            s = jnp.dot(q.astype(bf16), ckt_ref[r, kv], preferred_element_type=f32)
            s = s - slope * (float(t_pos) - blk_mid)
            s = jnp.where(blk * CMP_BLOCK + (CMP_BLOCK - 1) <= t_pos, s, -jnp.inf)
            qn = q.astype(bf16).astype(f32)
            s_new = jnp.sum(qn * ck_new[row:row + 1].astype(bf16).astype(f32), axis=-1, keepdims=True)
            s_new = s_new - slope * (float(t_pos) - new_mid)
            s_new = jnp.where(new_ok, s_new, -jnp.inf)
            m = jnp.maximum(jnp.max(s, axis=-1, keepdims=True), s_new)
            m = jnp.where(jnp.isfinite(m), m, 0.0)
            p = jnp.exp(s - m)
            p_new = jnp.exp(s_new - m)
            den = jnp.maximum(jnp.sum(p, axis=-1, keepdims=True) + p_new, 1e-30)
            p = p / den
            p_new = p_new / den
            oc = jnp.dot(p.astype(bf16), cv_ref[r, kv], preferred_element_type=f32)
            oc_ref[r, kv] = oc + p_new * cv_new[row:row + 1]
            imps.append(jnp.sum(p[:GROUP], axis=0, keepdims=True))
    imp = jnp.concatenate(imps, axis=0)
    cur = t_pos // CMP_BLOCK
    forced = (blk == 0) | (blk == cur) | (blk == cur - 1)
    v = jnp.where(forced, FORCED_SCORE, jnp.where(blk * CMP_BLOCK <= t_pos, imp, -1.0))
    out = jnp.zeros((sb * N_KV, 128), jnp.int32)
    for it in range(n_pick):
        mx = jnp.max(v, axis=-1, keepdims=True)
        idx = jnp.min(jnp.where(v == mx, blk, nb), axis=-1, keepdims=True)
        out = jnp.where(lane == it, idx, out)
        v = jnp.where(blk == idx, -jnp.inf, v)
    idx_ref[...] = out.reshape(sb, N_KV, 128)


def _dec_cmp(q, ck, cv, xk, xv, wk, wv, g_kc, t_pos, n_pick):
    b, nb = ck.shape[:2]
    ckt = ck.astype(bf16).transpose(0, 2, 3, 1)
    cvh = cv.astype(bf16).transpose(0, 2, 1, 3)
    kd = CMP_BLOCK * HEAD_DIM
    sb = 8 if b % 8 == 0 else b
    return pl.pallas_call(
        functools.partial(_dec_cmp_kernel, n_pick, t_pos),
        out_shape=[jax.ShapeDtypeStruct((b, N_KV, _GP, HEAD_DIM), f32),
                   jax.ShapeDtypeStruct((b, N_KV, 128), jnp.int32)],
        grid=(b // sb,),
        in_specs=[
            pl.BlockSpec((sb, N_KV, _GP, HEAD_DIM), lambda i: (i, 0, 0, 0)),
            pl.BlockSpec((sb, N_KV, HEAD_DIM, nb), lambda i: (i, 0, 0, 0)),
            pl.BlockSpec((sb, N_KV, nb, HEAD_DIM), lambda i: (i, 0, 0, 0)),
            pl.BlockSpec((sb, _GP, kd), lambda i: (i, 0, 0)),
            pl.BlockSpec((sb, _GP, kd), lambda i: (i, 0, 0)),
            pl.BlockSpec((kd, HEAD_DIM), lambda i: (0, 0)),
            pl.BlockSpec((kd, HEAD_DIM), lambda i: (0, 0)),
            pl.BlockSpec((1, HEAD_DIM), lambda i: (0, 0)),
        ],
        out_specs=[pl.BlockSpec((sb, N_KV, _GP, HEAD_DIM), lambda i: (i, 0, 0, 0)),
                   pl.BlockSpec((sb, N_KV, 128), lambda i: (i, 0, 0))],
        compiler_params=_cparams(("parallel",)),
        name="dec_cmp",
    )(q, ckt, cvh, xk, xv, wk, wv, g_kc.reshape(1, HEAD_DIM))


def _dec_sel_kernel(t_pos, n_pick, pt_ref, idx_ref, q_ref, gt_ref, oc_ref, ksn_ref, vsn_ref, kwn_ref,
                    vwn_ref, wk_ref, wv_ref, pk_ref, pv_ref, o_ref, kb0, vb0, kb1, vb1, sem):
    g = pl.program_id(0)
    n_steps = pl.num_programs(0)
    n_buf = wk_ref.shape[-1]
    page = 2 * CMP_BLOCK

    def copies(seq, kb, vb, slot):
        out = []
        for kv in range(N_KV):
            for s in range(n_pick):
                pg = pt_ref[seq, idx_ref[seq, kv * 128 + s] // 2]
                dst = pl.ds(s * page, page)
                out.append(pltpu.make_async_copy(pk_ref.at[pg, kv], kb.at[kv, :, dst], sem.at[slot]))
                out.append(pltpu.make_async_copy(pv_ref.at[pg, kv], vb.at[kv, :, dst], sem.at[slot]))
        return out

    def start(seq, kb, vb, slot):
        for cp in copies(seq, kb, vb, slot):
            cp.start()

    def wait(seq, kb, vb, slot):
        for cp in copies(seq, kb, vb, slot):
            cp.wait()

    def slopes(kv):
        g1 = lax.broadcasted_iota(jnp.int32, (_GP, 1), 0) + (kv * GROUP + 1)
        return jnp.exp2(-0.5 * g1.astype(f32))

    lane = lax.broadcasted_iota(jnp.int32, (1, page), 1)
    c = lax.broadcasted_iota(jnp.int32, (1, n_buf), 1)
    kpos = t_pos - n_buf + c
    distw = t_pos - kpos
    okw = (kpos >= 0) & (distw >= 0) & (distw < WINDOW)

    def attend(seq, r, kb, vb):
        for kv in range(N_KV):
            q = q_ref[r, kv].astype(bf16)
            qf = q.astype(f32)
            slope = slopes(kv)
            dist, ok = [], []
            for s in range(n_pick):
                blk = idx_ref[seq, kv * 128 + s]
                d = t_pos - ((blk // 2) * page + lane)
                dist.append(d)
                ok.append((lane // CMP_BLOCK == blk % 2) & (d >= 0))
            dist = jnp.concatenate(dist, axis=1)
            ok = jnp.concatenate(ok, axis=1)
            s_sel = jnp.dot(q, kb[kv].astype(bf16), preferred_element_type=f32)
            s_sel = jnp.where(ok, s_sel - slope * dist.astype(f32), -jnp.inf)
            s_new = jnp.sum(qf * ksn_ref[r, kv:kv + 1].astype(bf16).astype(f32), axis=-1, keepdims=True)
            m = jnp.maximum(jnp.max(s_sel, axis=-1, keepdims=True), s_new)
            p = jnp.exp(s_sel - m)
            p_new = jnp.exp(s_new - m)
            den = jnp.maximum(jnp.sum(p, axis=-1, keepdims=True) + p_new, 1e-30)
            o_s = lax.dot_general((p / den).astype(bf16), vb[kv].astype(bf16), _NT, preferred_element_type=f32)
            o_s = o_s + (p_new / den).astype(bf16).astype(f32) * vsn_ref[r, kv:kv + 1].astype(bf16).astype(f32)
            sw = jnp.dot(q, wk_ref[r, kv].astype(bf16), preferred_element_type=f32)
            sw = jnp.where(okw, sw - slope * distw.astype(f32), -jnp.inf)
            sw_new = jnp.sum(qf * kwn_ref[r, kv:kv + 1].astype(bf16).astype(f32), axis=-1, keepdims=True)
            mw = jnp.maximum(jnp.max(sw, axis=-1, keepdims=True), sw_new)
            pw = jnp.exp(sw - mw)
            pw_new = jnp.exp(sw_new - mw)
            denw = jnp.maximum(jnp.sum(pw, axis=-1, keepdims=True) + pw_new, 1e-30)
            o_w = lax.dot_general((pw / denw).astype(bf16), wv_ref[r, kv].astype(bf16), _NT,
                                  preferred_element_type=f32)
            o_w = o_w + (pw_new / denw).astype(bf16).astype(f32) * vwn_ref[r, kv:kv + 1].astype(bf16).astype(f32)
            gt = gt_ref[r, kv]
            o_ref[r, kv] = gt[:, 0:1] * oc_ref[r, kv] + gt[:, 1:2] * o_s + gt[:, 2:3] * o_w

    @pl.when(g == 0)
    def _():
        start(0, kb0, vb0, 0)

    start(2 * g + 1, kb1, vb1, 1)
    wait(2 * g, kb0, vb0, 0)
    attend(2 * g, 0, kb0, vb0)

    @pl.when(g + 1 < n_steps)
    def _():
        start(2 * g + 2, kb0, vb0, 0)

    wait(2 * g + 1, kb1, vb1, 1)
    attend(2 * g + 1, 1, kb1, vb1)


def _dec_sel(q, gates, o_c, idx, table, pool_k, pool_v, ks_new, vs_new, kw_new, vw_new, win_k, win_v,
             t_pos, n_pick):
    b = q.shape[0]
    assert b % 2 == 0
    n_buf = win_k.shape[-1]
    page = pool_k.shape[-1]
    hspec = pl.BlockSpec((2, N_KV, _GP, HEAD_DIM), lambda i, pt, ix: (i, 0, 0, 0))
    nspec = pl.BlockSpec((2, N_KV, HEAD_DIM), lambda i, pt, ix: (i, 0, 0))
    wspec = pl.BlockSpec((2, N_KV, HEAD_DIM, n_buf), lambda i, pt, ix: (i, 0, 0, 0))
    anyspec = pl.BlockSpec(memory_space=pl.ANY)
    gbuf = pltpu.VMEM((N_KV, HEAD_DIM, n_pick * page), f32)
    return pl.pallas_call(
        functools.partial(_dec_sel_kernel, t_pos, n_pick),
        out_shape=jax.ShapeDtypeStruct((b, N_KV, _GP, HEAD_DIM), f32),
        grid_spec=pltpu.PrefetchScalarGridSpec(
            num_scalar_prefetch=2,
            grid=(b // 2,),
            in_specs=[hspec, pl.BlockSpec((2, N_KV, _GP, 3), lambda i, pt, ix: (i, 0, 0, 0)), hspec,
                      nspec, nspec, nspec, nspec, wspec, wspec, anyspec, anyspec],
            out_specs=hspec,
            scratch_shapes=[gbuf, gbuf, gbuf, gbuf, pltpu.SemaphoreType.DMA((2,))]),
        compiler_params=_cparams(("arbitrary",)),
        name="dec_sel",
    )(table, idx, q, gates, o_c, ks_new, vs_new, kw_new, vw_new, win_k, win_v, pool_k, pool_v)


def _head_rms(x, g):
    ms = jnp.mean(x * x, axis=-1, keepdims=True)
    return x * lax.rsqrt(ms + NORM_EPS) * g


def _prepare(p):
    q = dict(p)
    for n in ("ffn1_gate", "ffn1_up", "ffn1_down", "ffn2_gate", "ffn2_up", "ffn2_down", "w_pa", "w_pb", "w_out"):
        q[n] = p[n].astype(bf16)
    q["w_in_p"] = _pad_cols(p["w_in"]).astype(bf16)
    q["mu_p"] = _pad_rw_cols(p["rw_mu"][None])
    q["w2p"] = _lora_pad(p["rw_w2"])
    q["a2p"] = _lora_pad(p["rw_a2"])
    q["g2"] = p["rw_g2"].astype(bf16)
    return q


def _nsa_proj(z, p):
    m = z.shape[0]
    kvs = (m, N_KV, HEAD_DIM)
    q = _head_rms(_seg(z, "q").reshape(m, N_HEADS, HEAD_DIM), p["g_q"])
    kc = _seg(z, "kc").reshape(kvs)
    vc = _seg(z, "vc").reshape(kvs)
    ks = _head_rms(_seg(z, "ks").reshape(kvs), p["g_ks"])
    vs = _seg(z, "vs").reshape(kvs)
    kw = _head_rms(_seg(z, "kw").reshape(kvs), p["g_kw"])
    vw = _seg(z, "vw").reshape(kvs)
    gates = jax.nn.sigmoid(_seg(z, "ga").reshape(m, N_HEADS, 3))
    return q, gates, kc, vc, ks, vs, kw, vw


def _layer_prompt(x, p):
    t = x.shape[0]
    x = _ffn(x, p["n_ffn1"], p["ffn1_gate"], p["ffn1_up"], p["ffn1_down"])
    z = _inproj(x, p["n_mix"], p["w_in_p"])
    q, gates, kc, vc, ks, vs, kw, vw = _nsa_proj(z, p)
    ck = _head_rms(_compress(kc.reshape(1, t, KV_DIM), p["pe_cmp_k"], p["w_cmp_k"])[0]
                   .reshape(-1, N_KV, HEAD_DIM), p["g_kc"])
    cv = _compress(vc.reshape(1, t, KV_DIM), p["pe_cmp_v"], p["w_cmp_v"])[0].reshape(-1, N_KV, HEAD_DIM)
    o_a = _nsa_prompt(z, p["g_q"], gates, ck, cv, ks, vs, kw, vw)
    shift0 = jnp.zeros((8, _Z_COLS), f32)
    s0t = jnp.zeros((RW_HEADS, RW_N, RW_N), f32)
    o_b, st = _rwkv_prompt(z, shift0, s0t, p)
    x = _merge(o_a, o_b, z, x, p["w_pa"], p["w_pb"], p["w_out"])
    x = _ffn(x, p["n_ffn2"], p["ffn2_gate"], p["ffn2_up"], p["ffn2_down"])
    n_buf = min(WINDOW, t)
    states = (kc, vc, ks, vs, kw[-n_buf:], vw[-n_buf:], _rw_cols(z[-1:]), st.transpose(0, 2, 1)[None])
    return x, states


def _layer_sample(x, p, past):
    b = x.shape[0]
    table = past["page_table"]
    page = past["cmp_k"].shape[1]
    t_pos = table.shape[1] * page
    x = _ffn(x, p["n_ffn1"], p["ffn1_gate"], p["ffn1_up"], p["ffn1_down"])
    z = _inproj(x, p["n_mix"], p["w_in_p"])
    q, gates, kc, vc, ks, vs, kw, vw = _nsa_proj(z, p)

    fm = lambda a: a.transpose(0, 2, 3, 1)

    nbp = t_pos // CMP_BLOCK
    ck = _head_rms(_compress_pages(fm(past["cmp_k"]), table, p["pe_cmp_k"], p["w_cmp_k"]), p["g_kc"])
    cv = _compress_pages(fm(past["cmp_v"]), table, p["pe_cmp_v"], p["w_cmp_v"])

    def new_block_rows(k_new, pe):
        first = k_new + pe[0]
        rest = jnp.broadcast_to(pe[1:].reshape(1, 1, -1), (b, N_KV, (CMP_BLOCK - 1) * HEAD_DIM))
        rows = jnp.concatenate([first, rest], axis=-1)
        return jnp.pad(rows, ((0, 0), (0, _GP - N_KV), (0, 0)))

    n_sel = min(N_SEL, nbp + 1)
    n_pick = n_sel - 1
    scale = HEAD_DIM ** -0.5
    pad_g = lambda a: jnp.pad(a, ((0, 0), (0, 0), (0, _GP - GROUP), (0, 0)))
    qh = pad_g((q * scale).reshape(b, N_KV, GROUP, HEAD_DIM))
    gth = pad_g(gates.reshape(b, N_KV, GROUP, 3))
    kd = CMP_BLOCK * HEAD_DIM
    o_c, idx = _dec_cmp(qh, ck, cv, new_block_rows(kc, p["pe_cmp_k"]), new_block_rows(vc, p["pe_cmp_v"]),
                        p["w_cmp_k"].reshape(kd, HEAD_DIM).astype(bf16),
                        p["w_cmp_v"].reshape(kd, HEAD_DIM).astype(bf16), p["g_kc"], t_pos, n_pick)
    o_a = _dec_sel(qh, gth, o_c, idx.reshape(b, N_KV * 128), table, fm(past["slc_k"]), fm(past["slc_v"]),
                   ks, vs, kw, vw, fm(past["win_k"]), fm(past["win_v"]), t_pos, n_pick)
    o_a = o_a[:, :, :GROUP].reshape(b, NSA_DIM).astype(bf16)

    prev = _pad_rw_cols(past["shift"])
    r, k, v, kk, ka, ld, g, bonus = _rw_prep(
        z, prev, p["mu_p"], p["rw_w0"], p["w2p"], p["rw_a0"], p["a2p"], p["g2"],
        p["rw_k_k"], p["rw_k_a"], p["rw_r_k"], False)
    tb = lambda a: a.transpose(1, 0, 2).reshape(b, RW_HEADS, RW_N)
    y, wkv = _wkv_step(past["wkv"], tb(r), tb(k), tb(v), tb(kk), tb(ka), tb(ld))
    y = y.reshape(b, RW_HEADS // 2, 2 * RW_N).transpose(1, 0, 2)
    o_b = _rw_post(y, bonus, g, p["rw_ln_w"], p["rw_ln_b"])

    x = _merge(o_a, o_b, z, x, p["w_pa"], p["w_pb"], p["w_out"])
    x = _ffn(x, p["n_ffn2"], p["ffn2_gate"], p["ffn2_up"], p["ffn2_down"])
    kvs = lambda a: a.reshape(b, 1, N_KV, HEAD_DIM)
    win_k = jnp.concatenate([past["win_k"][:, 1:], kvs(kw)], axis=1)
    win_v = jnp.concatenate([past["win_v"][:, 1:], kvs(vw)], axis=1)
    states = (kvs(kc), kvs(vc), kvs(ks), kvs(vs), win_k, win_v, _rw_cols(z), wkv)
    return x, states


def kernel(x_prompt, x_sample, cache_cmp_k, cache_cmp_v, cache_slc_k, cache_slc_v, cache_win_k, cache_win_v,
           state_shift, state_wkv, page_table,
           n_ffn1, ffn1_gate, ffn1_up, ffn1_down, n_mix, w_in, g_q, g_kc, g_ks, g_kw,
           w_cmp_k, pe_cmp_k, w_cmp_v, pe_cmp_v,
           rw_mu, rw_w0, rw_w2, rw_a0, rw_a2, rw_g2, rw_k_k, rw_k_a, rw_r_k, rw_ln_w, rw_ln_b,
           w_pa, w_pb, w_out, n_ffn2, ffn2_gate, ffn2_up, ffn2_down):
    assert x_prompt.shape[0] == 1 and x_sample.shape[1] == 1 and n_ffn1.shape[0] == 1
    l = 0
    p = _prepare(dict(
        n_ffn1=n_ffn1[l], ffn1_gate=ffn1_gate[l], ffn1_up=ffn1_up[l], ffn1_down=ffn1_down[l],
        n_mix=n_mix[l], w_in=w_in[l], g_q=g_q[l], g_kc=g_kc[l], g_ks=g_ks[l], g_kw=g_kw[l],
        w_cmp_k=w_cmp_k[l], pe_cmp_k=pe_cmp_k[l], w_cmp_v=w_cmp_v[l], pe_cmp_v=pe_cmp_v[l],
        rw_mu=rw_mu[l], rw_w0=rw_w0[l], rw_w2=rw_w2[l], rw_a0=rw_a0[l], rw_a2=rw_a2[l], rw_g2=rw_g2[l],
        rw_k_k=rw_k_k[l], rw_k_a=rw_k_a[l], rw_r_k=rw_r_k[l], rw_ln_w=rw_ln_w[l], rw_ln_b=rw_ln_b[l],
        w_pa=w_pa[l], w_pb=w_pb[l], w_out=w_out[l],
        n_ffn2=n_ffn2[l], ffn2_gate=ffn2_gate[l], ffn2_up=ffn2_up[l], ffn2_down=ffn2_down[l]))
    t = x_prompt.shape[1]
    y_p, sp = _layer_prompt(x_prompt[0], p)
    past = dict(page_table=page_table, cmp_k=cache_cmp_k[l], cmp_v=cache_cmp_v[l], slc_k=cache_slc_k[l],
                slc_v=cache_slc_v[l], win_k=cache_win_k[l], win_v=cache_win_v[l], shift=state_shift[l],
                wkv=state_wkv[l])
    y_s, ss = _layer_sample(x_sample[:, 0], p, past)
    kvp = lambda a: a.reshape(1, 1, -1, N_KV, HEAD_DIM)
    outs_p = (kvp(sp[0]), kvp(sp[1]), kvp(sp[2]), kvp(sp[3]), kvp(sp[4]), kvp(sp[5]), sp[6][None], sp[7][None])
    outs_s = tuple(a[None] for a in ss)
    return (y_p.reshape(1, t, D_MODEL), y_s[:, None, :]) + outs_p + outs_s
```
